```python
import math
import jax
import jax.numpy as jnp
from jax import lax
import numpy as np

D_MODEL = 2048
BATCH = 32
SEQ = 256
DEPTH = 4
DEC_BATCH = 8
DEC_SEQ = 2048
PAST_LEN = 512

GRID_W = 64
N_MIXERS = 2
EPS = 1e-6
DN_K_HEADS = 16
DN_V_HEADS = 32
DN_HEAD_DIM = 128
DN_CONV_K = 4
DN_CHUNK = 64
DN_QK_DIM = DN_K_HEADS * DN_HEAD_DIM
DN_V_DIM = DN_V_HEADS * DN_HEAD_DIM
DN_CONV_DIM = 2 * DN_QK_DIM + DN_V_DIM
DN_IN_DIM = DN_CONV_DIM + DN_V_DIM + 4 * DN_V_HEADS
ATT_Q_HEADS = 16
ATT_KV_HEADS = 2
ATT_HEAD_DIM = 256
ATT_IN_DIM = (ATT_Q_HEADS + 2 * ATT_KV_HEADS) * ATT_HEAD_DIM
Q_BLOCK = 128
ROPE_THETA = 10000.0
FFN_DENSE = 5632
N_EXPERTS = 8
TOP_K = 2
FFN_EXPERT = 7168
N_DN_LAYERS = (DEPTH + 1) // 2
N_ATT_LAYERS = DEPTH // 2
N_DENSE_LAYERS = (DEPTH + 1) // 2
N_MOE_LAYERS = DEPTH // 2

kernel_name = "hybrid_deltanet_gqa_diffusion_step"

F32 = jnp.float32


def rms_norm(x, w):
    xf = x.astype(F32)
    y = xf * lax.rsqrt(jnp.mean(xf * xf, axis=-1, keepdims=True) + EPS)
    return (y * w.astype(F32)).astype(x.dtype)


def l2_norm(x):
    return x * lax.rsqrt(jnp.sum(x * x, axis=-1, keepdims=True) + EPS)


def adaln(cvec, w_mod, b_mod):
    m = jax.nn.silu(cvec) @ w_mod + b_mod
    return m.reshape(cvec.shape[0], 6, 1, -1)


def modulate(h, shift, scale):
    return h * (1 + scale) + shift


def short_conv(x, w):
    K = w.shape[0]
    T = x.shape[1]
    xp = jnp.pad(x, ((0, 0), ((K - 1) // 2, K // 2), (0, 0)))
    y = sum(xp[:, j:j + T] * w[j] for j in range(K))
    return jax.nn.silu(y)


def axial_rope_tables(rows, head_dim):
    n_freq = head_dim // 4
    inv = ROPE_THETA ** (-jnp.arange(n_freq, dtype=F32) / n_freq)
    r = jnp.repeat(jnp.arange(rows, dtype=F32), GRID_W)
    cl = jnp.tile(jnp.arange(GRID_W, dtype=F32), rows)
    ang = jnp.concatenate([r[:, None] * inv, cl[:, None] * inv], axis=-1)
    return jnp.cos(ang), jnp.sin(ang)


def apply_rope(x, cos, sin):
    xf = x.astype(F32)
    half = x.shape[-1] // 2
    x1, x2 = xf[..., :half], xf[..., half:]
    c = cos[None, :, None, :]
    s = sin[None, :, None, :]
    return jnp.concatenate([x1 * c - x2 * s, x2 * c + x1 * s], axis=-1).astype(x.dtype)


def block_attention(q, k, v):
    B, T, H, Dh = q.shape
    KV = k.shape[2]
    G = H // KV
    nb = T // Q_BLOCK
    scale = Dh ** -0.5
    qb = jnp.moveaxis(q.reshape(B, nb, Q_BLOCK, KV, G, Dh), 1, 0)

    def one_block(qblk):
        s = jnp.einsum('bqkgd,bskd->bkgqs', qblk, k, preferred_element_type=F32) * scale
        p = jax.nn.softmax(s, axis=-1)
        return jnp.einsum('bkgqs,bskd->bqkgd', p.astype(v.dtype), v)

    o = lax.map(one_block, qb)
    return jnp.moveaxis(o, 0, 1).reshape(B, T, H, Dh)


def attn_project(h, w_in, q_norm, k_norm):
    B, T, _ = h.shape
    qkv = h @ w_in
    qd = ATT_Q_HEADS * ATT_HEAD_DIM
    kd = ATT_KV_HEADS * ATT_HEAD_DIM
    q = rms_norm(qkv[..., :qd].reshape(B, T, ATT_Q_HEADS, ATT_HEAD_DIM), q_norm)
    k = rms_norm(qkv[..., qd:qd + kd].reshape(B, T, ATT_KV_HEADS, ATT_HEAD_DIM), k_norm)
    v = qkv[..., qd + kd:].reshape(B, T, ATT_KV_HEADS, ATT_HEAD_DIM)
    return q, k, v


def attention_mixer(hc, hx, ck, cv, cos, sin, w_in, q_norm, k_norm, w_out):
    qc, kc, vc = attn_project(hc, w_in, q_norm, k_norm)
    oc = block_attention(qc, kc, vc)
    qx, kx, vx = attn_project(hx, w_in, q_norm, k_norm)
    qx = apply_rope(qx, cos, sin)
    kx = apply_rope(kx, cos, sin)
    k_all = jnp.concatenate([ck.astype(kx.dtype), kx], axis=1)
    v_all = jnp.concatenate([cv.astype(vx.dtype), vx], axis=1)
    ox = block_attention(qx, k_all, v_all)
    Bc, Tc = hc.shape[:2]
    Bx, Tx = hx.shape[:2]
    return (oc.reshape(Bc, Tc, -1) @ w_out, ox.reshape(Bx, Tx, -1) @ w_out, kc, vc)


def chunk_gated_delta(q, k, v, g, beta, S0):
    B, T, H, dk = q.shape
    C = DN_CHUNK
    N = T // C

    def chunks(x):
        return jnp.swapaxes(x.reshape((B, N, C) + x.shape[2:]), 2, 3)

    q = chunks(q) * (dk ** -0.5)
    k = chunks(k)
    v = chunks(v)
    gc = jnp.cumsum(chunks(g), axis=-1)
    beta = chunks(beta)
    tri = jnp.tril(jnp.ones((C, C), dtype=bool))
    strict = jnp.tril(jnp.ones((C, C), dtype=bool), -1)
    decay = jnp.exp(jnp.where(tri, gc[..., :, None] - gc[..., None, :], -jnp.inf))
    kb = k * beta[..., None]
    vb = v * beta[..., None]
    L = jnp.where(strict, jnp.einsum('bnhid,bnhjd->bnhij', kb, k) * decay, 0.0)
    eye = jnp.eye(C, dtype=F32)
    Tinv = lax.linalg.triangular_solve(eye + L, jnp.broadcast_to(eye, L.shape),
                                       left_side=True, lower=True, unit_diagonal=True)
    u = Tinv @ vb
    w = Tinv @ (kb * jnp.exp(gc)[..., None])
    a_intra = jnp.where(tri, jnp.einsum('bnhid,bnhjd->bnhij', q, k) * decay, 0.0)

    def step(S, xs):
        qi, ki, ui, wi, gi, ai = xs
        v_new = ui - wi @ S
        o = (qi * jnp.exp(gi)[..., None]) @ S + ai @ v_new
        g_last = gi[..., -1]
        S = S * jnp.exp(g_last)[..., None, None] + jnp.einsum(
            'bhcd,bhce->bhde', ki * jnp.exp(g_last[..., None] - gi)[..., None], v_new)
        return S, o

    xs = tuple(jnp.swapaxes(x, 0, 1) for x in (q, k, u, w, gc, a_intra))
    S, o = lax.scan(step, S0, xs)
    o = o.transpose(1, 0, 3, 2, 4).reshape(B, T, H, -1)
    return o, S


def delta_bidir(q, k, v, g, beta, S0):
    of, Sf = chunk_gated_delta(q, k, v, g[:, :, 0], beta[:, :, 0], S0[:, 0])
    fl = lambda x: jnp.flip(x, axis=1)
    ob, Sb = chunk_gated_delta(fl(q), fl(k), fl(v), fl(g[:, :, 1]), fl(beta[:, :, 1]), S0[:, 1])
    return of + fl(ob), jnp.stack([Sf, Sb], axis=1)


def deltanet_branch(h, S0, w_in, conv_w, A_log, dt_bias, out_norm, w_out):
    B, T, _ = h.shape
    proj = h @ w_in
    qkv = short_conv(proj[..., :DN_CONV_DIM], conv_w).astype(F32)
    q = l2_norm(qkv[..., :DN_QK_DIM].reshape(B, T, DN_K_HEADS, DN_HEAD_DIM))
    k = l2_norm(qkv[..., DN_QK_DIM:2 * DN_QK_DIM].reshape(B, T, DN_K_HEADS, DN_HEAD_DIM))
    v = qkv[..., 2 * DN_QK_DIM:].reshape(B, T, DN_V_HEADS, DN_HEAD_DIM)
    rep = DN_V_HEADS // DN_K_HEADS
    q = jnp.repeat(q, rep, axis=2)
    k = jnp.repeat(k, rep, axis=2)
    z = proj[..., DN_CONV_DIM:DN_CONV_DIM + DN_V_DIM].astype(F32).reshape(B, T, DN_V_HEADS, DN_HEAD_DIM)
    ba = proj[..., DN_CONV_DIM + DN_V_DIM:].astype(F32).reshape(B, T, 2, 2, DN_V_HEADS)
    beta = jax.nn.sigmoid(ba[:, :, :, 0])
    g = -jnp.exp(A_log.astype(F32)) * jax.nn.softplus(ba[:, :, :, 1] + dt_bias.astype(F32))
    o, S = delta_bidir(q, k, v, g, beta, S0)
    o = rms_norm(o, out_norm) * jax.nn.silu(z)
    return o.reshape(B, T, -1).astype(h.dtype) @ w_out, S


def deltanet_mixer(hc, hx, S_ctx_cached, w_in, conv_w, A_log, dt_bias, out_norm, w_out):
    S0c = jnp.zeros((hc.shape[0], 2, DN_V_HEADS, DN_HEAD_DIM, DN_HEAD_DIM), F32)
    oc, Sc = deltanet_branch(hc, S0c, w_in, conv_w, A_log, dt_bias, out_norm, w_out)
    ox, _ = deltanet_branch(hx, S_ctx_cached.astype(F32), w_in, conv_w, A_log, dt_bias, out_norm, w_out)
    return oc, ox, Sc


def swiglu(h, w_gate, w_up, w_down):
    return (jax.nn.silu(h @ w_gate) * (h @ w_up)) @ w_down


def moe_swiglu(h, w_router, w_gate, w_up, w_down):
    B, T, D = h.shape
    x = h.reshape(-1, D)
    logits = (x @ w_router).astype(F32)
    top_v, top_i = lax.top_k(logits, TOP_K)
    wts = jax.nn.softmax(top_v, axis=-1)
    gates = jnp.sum(jax.nn.one_hot(top_i, N_EXPERTS, dtype=F32) * wts[..., None], axis=1).astype(x.dtype)
    y = jnp.zeros_like(x)
    for e in range(N_EXPERTS):
        y = y + gates[:, e:e + 1] * swiglu(x, w_gate[e], w_up[e], w_down[e])
    return y.reshape(B, T, D)


def setup_inputs(seed: int = 0) -> dict:
    key = jax.random.key(seed)
    ks = iter(jax.random.split(key, 40))
    D = D_MODEL

    def nrm(shape, scale):
        return jax.random.normal(next(ks), shape, F32) * scale

    def gain(shape):
        return 1.0 + 0.02 * jax.random.normal(next(ks), shape, F32)

    x_prompt = nrm((BATCH, SEQ, D), 1.0)
    x_sample = nrm((DEC_BATCH, DEC_SEQ, D), 1.0)
    state_dn = nrm((DEC_BATCH, N_DN_LAYERS, 2, DN_V_HEADS, DN_HEAD_DIM, DN_HEAD_DIM), 0.1)
    cache_k = nrm((DEC_BATCH, N_ATT_LAYERS, PAST_LEN, ATT_KV_HEADS, ATT_HEAD_DIM), 1.0)
    cache_v = nrm((DEC_BATCH, N_ATT_LAYERS, PAST_LEN, ATT_KV_HEADS, ATT_HEAD_DIM), 1.0)
    c = nrm((DEC_BATCH, D), 1.0)
    c_ctx = nrm((D,), 1.0)
    w_mod = nrm((DEPTH, D, 6 * D), D ** -0.5)
    b_mod = nrm((DEPTH, 6 * D), 0.02)
    norm_mix = gain((DEPTH, D))
    norm_ffn = gain((DEPTH, D))
    norm_final = gain((D,))
    dn_w_in = nrm((N_DN_LAYERS, D, DN_IN_DIM), D ** -0.5)
    dn_conv = nrm((N_DN_LAYERS, DN_CONV_K, DN_CONV_DIM), DN_CONV_K ** -0.5)
    dn_A_log = jnp.log(jax.random.uniform(next(ks), (N_DN_LAYERS, 2, DN_V_HEADS), F32, 1.0, 16.0))
    dt = jnp.exp(jax.random.uniform(next(ks), (N_DN_LAYERS, 2, DN_V_HEADS), F32,
                                    math.log(1e-3), math.log(1e-1)))
    dn_dt_bias = dt + jnp.log(-jnp.expm1(-dt))
    dn_out_norm = gain((N_DN_LAYERS, DN_HEAD_DIM))
    dn_w_out = nrm((N_DN_LAYERS, DN_V_DIM, D), DN_V_DIM ** -0.5)
    att_w_in = nrm((N_ATT_LAYERS, D, ATT_IN_DIM), D ** -0.5)
    att_q_norm = gain((N_ATT_LAYERS, ATT_HEAD_DIM))
    att_k_norm = gain((N_ATT_LAYERS, ATT_HEAD_DIM))
    att_w_out = nrm((N_ATT_LAYERS, ATT_Q_HEADS * ATT_HEAD_DIM, D), (ATT_Q_HEADS * ATT_HEAD_DIM) ** -0.5)
    ffn_w_gate = nrm((N_DENSE_LAYERS, D, FFN_DENSE), D ** -0.5)
    ffn_w_up = nrm((N_DENSE_LAYERS, D, FFN_DENSE), D ** -0.5)
    ffn_w_down = nrm((N_DENSE_LAYERS, FFN_DENSE, D), FFN_DENSE ** -0.5)
    moe_router = nrm((N_MOE_LAYERS, D, N_EXPERTS), D ** -0.5)
    moe_w_gate = nrm((N_MOE_LAYERS, N_EXPERTS, D, FFN_EXPERT), D ** -0.5)
    moe_w_up = nrm((N_MOE_LAYERS, N_EXPERTS, D, FFN_EXPERT), D ** -0.5)
    moe_w_down = nrm((N_MOE_LAYERS, N_EXPERTS, FFN_EXPERT, D), FFN_EXPERT ** -0.5)
    return {
        "x_prompt": x_prompt, "x_sample": x_sample,
        "state_dn": state_dn, "cache_k": cache_k, "cache_v": cache_v,
        "c": c, "c_ctx": c_ctx,
        "w_mod": w_mod, "b_mod": b_mod, "norm_mix": norm_mix, "norm_ffn": norm_ffn,
        "norm_final": norm_final,
        "dn_w_in": dn_w_in, "dn_conv": dn_conv, "dn_A_log": dn_A_log, "dn_dt_bias": dn_dt_bias,
        "dn_out_norm": dn_out_norm, "dn_w_out": dn_w_out,
        "att_w_in": att_w_in, "att_q_norm": att_q_norm, "att_k_norm": att_k_norm,
        "att_w_out": att_w_out,
        "ffn_w_gate": ffn_w_gate, "ffn_w_up": ffn_w_up, "ffn_w_down": ffn_w_down,
        "moe_router": moe_router, "moe_w_gate": moe_w_gate, "moe_w_up": moe_w_up,
        "moe_w_down": moe_w_down,
    }


def reference(x_prompt, x_sample, state_dn, cache_k, cache_v, c, c_ctx,
              w_mod, b_mod, norm_mix, norm_ffn, norm_final,
              dn_w_in, dn_conv, dn_A_log, dn_dt_bias, dn_out_norm, dn_w_out,
              att_w_in, att_q_norm, att_k_norm, att_w_out,
              ffn_w_gate, ffn_w_up, ffn_w_down,
              moe_router, moe_w_gate, moe_w_up, moe_w_down):
    rows = x_sample.shape[1] // GRID_W
    cos, sin = axial_rope_tables(rows, ATT_HEAD_DIM)
    xc = x_prompt
    xx = x_sample
    new_dn, new_k, new_v = [], [], []
    for i in range(DEPTH):
        j = i // N_MIXERS
        mc = adaln(c_ctx[None, :], w_mod[i], b_mod[i])
        mx = adaln(c, w_mod[i], b_mod[i])
        hc = modulate(rms_norm(xc, norm_mix[i]), mc[:, 0], mc[:, 1])
        hx = modulate(rms_norm(xx, norm_mix[i]), mx[:, 0], mx[:, 1])
        if i % N_MIXERS == 0:
            oc, ox, Sc = deltanet_mixer(hc, hx, state_dn[:, j], dn_w_in[j], dn_conv[j], dn_A_log[j],
                                        dn_dt_bias[j], dn_out_norm[j], dn_w_out[j])
            new_dn.append(Sc.astype(x_prompt.dtype))
        else:
            oc, ox, kc, vc = attention_mixer(hc, hx, cache_k[:, j], cache_v[:, j], cos, sin,
                                             att_w_in[j], att_q_norm[j], att_k_norm[j], att_w_out[j])
            new_k.append(kc)
            new_v.append(vc)
        xc = xc + mc[:, 2] * oc
        xx = xx + mx[:, 2] * ox
        hc = modulate(rms_norm(xc, norm_ffn[i]), mc[:, 3], mc[:, 4])
        hx = modulate(rms_norm(xx, norm_ffn[i]), mx[:, 3], mx[:, 4])
        if i % 2 == 0:
            fc = swiglu(hc, ffn_w_gate[j], ffn_w_up[j], ffn_w_down[j])
            fx = swiglu(hx, ffn_w_gate[j], ffn_w_up[j], ffn_w_down[j])
        else:
            fc = moe_swiglu(hc, moe_router[j], moe_w_gate[j], moe_w_up[j], moe_w_down[j])
            fx = moe_swiglu(hx, moe_router[j], moe_w_gate[j], moe_w_up[j], moe_w_down[j])
        xc = xc + mc[:, 5] * fc
        xx = xx + mx[:, 5] * fx
    y_prompt = rms_norm(xc, norm_final)
    y_sample = rms_norm(xx, norm_final)
    new_state_dn = jnp.stack(new_dn, axis=1)
    new_cache_k = jnp.stack(new_k, axis=1)
    new_cache_v = jnp.stack(new_v, axis=1)
    return (y_prompt, y_sample, new_state_dn, new_cache_k, new_cache_v)
```

```python
import functools
import math

import numpy as np
import jax
import jax.numpy as jnp
from jax import lax
from jax.experimental import pallas as pl
from jax.experimental.pallas import tpu as pltpu

F32 = jnp.float32
BF16 = jnp.bfloat16
EPS = 1e-6

GRID_W = 64
ROPE_THETA = 10000.0
DN_K_HEADS = 16
DN_V_HEADS = 32
DN_HEAD_DIM = 128
DN_CHUNK = 64
ATT_Q_HEADS = 16
ATT_KV_HEADS = 2
ATT_HEAD_DIM = 256
N_EXPERTS = 8
TOP_K = 2

LANES = 128
VMEM_LIMIT_BYTES = 56 * 1024 * 1024

DN_HEADS_PER_STEP = 8


def _cp(*sem):
    return pltpu.CompilerParams(dimension_semantics=sem, vmem_limit_bytes=VMEM_LIMIT_BYTES)


def _silu(x):
    return x * jax.nn.sigmoid(x)


def _group_of_tile(i, tm, n_ctx, t_lat):
    row = i * tm
    return jnp.where(row < n_ctx, 0, 1 + (row - n_ctx) // t_lat)


def _adaln_kernel(c_ref, w_ref, b_ref, o_ref):
    s = _silu(c_ref[...]).astype(BF16)
    o_ref[0] = jnp.dot(s, w_ref[0].astype(BF16), preferred_element_type=F32) + b_ref[0]


def adaln_all(cvec, w_mod, b_mod, tn=1024):
    n_layers, d, n = w_mod.shape
    g = cvec.shape[0]
    return pl.pallas_call(
        _adaln_kernel,
        grid=(n_layers, n // tn),
        in_specs=[
            pl.BlockSpec((g, d), lambda l, j: (0, 0)),
            pl.BlockSpec((1, d, tn), lambda l, j: (l, 0, j)),
            pl.BlockSpec((1, 1, tn), lambda l, j: (l, 0, j)),
        ],
        out_specs=pl.BlockSpec((1, g, tn), lambda l, j: (l, 0, j)),
        out_shape=jax.ShapeDtypeStruct((n_layers, g, n), F32),
        compiler_params=_cp("parallel", "parallel"),
        name="adaln",
    )(cvec, w_mod, b_mod.reshape(n_layers, 1, n))


def _norm_mod_kernel(x_ref, w_ref, *rest, modulated):
    x = x_ref[...]
    y = x * lax.rsqrt(jnp.mean(x * x, axis=-1, keepdims=True) + EPS) * w_ref[...]
    if modulated:
        sh_ref, sc_ref, o_ref = rest
        y = y * (1.0 + sc_ref[0]) + sh_ref[0]
    else:
        (o_ref,) = rest
    o_ref[...] = y.astype(o_ref.dtype)


def norm_mod(x, w, modtab, rows, n_ctx, t_lat, out_dtype, tm=512):
    m, d = x.shape
    in_specs = [pl.BlockSpec((tm, d), lambda i: (i, 0)), pl.BlockSpec((1, d), lambda i: (0, 0))]
    args = [x, w.reshape(1, d)]
    if modtab is not None:
        for r in rows:
            in_specs.append(pl.BlockSpec(
                (1, 1, d), lambda i, r=r: (_group_of_tile(i, tm, n_ctx, t_lat) * 6 + r, 0, 0)))
            args.append(modtab)
    return pl.pallas_call(
        functools.partial(_norm_mod_kernel, modulated=modtab is not None),
        grid=(m // tm,),
        in_specs=in_specs,
        out_specs=pl.BlockSpec((tm, d), lambda i: (i, 0)),
        out_shape=jax.ShapeDtypeStruct((m, d), out_dtype),
        compiler_params=_cp("parallel"),
        name="norm_mod",
    )(*args)


def _mm_kernel(x_ref, w_ref, o_ref):
    o_ref[...] = jnp.dot(x_ref[...], w_ref[...], preferred_element_type=F32).astype(o_ref.dtype)


def mm(x, w, out_dtype, tm=1024, tn=1024):
    m, k = x.shape
    n = w.shape[1]
    tm, tn = min(tm, m), min(tn, n)
    return pl.pallas_call(
        _mm_kernel,
        grid=(m // tm, n // tn),
        in_specs=[pl.BlockSpec((tm, k), lambda i, j: (i, 0)),
                  pl.BlockSpec((k, tn), lambda i, j: (0, j))],
        out_specs=pl.BlockSpec((tm, tn), lambda i, j: (i, j)),
        out_shape=jax.ShapeDtypeStruct((m, n), out_dtype),
        compiler_params=_cp("parallel", "parallel"),
        name="mm",
    )(x, w)


def _swiglu_kernel(te_ref, nv_ref, x_ref, wg_ref, wu_ref, o_ref):
    i = pl.program_id(0)

    @pl.when(i < nv_ref[0])
    def _():
        x = x_ref[...].astype(BF16)
        g = jnp.dot(x, wg_ref[0], preferred_element_type=F32)
        u = jnp.dot(x, wu_ref[0], preferred_element_type=F32)
        o_ref[...] = (_silu(g) * u).astype(o_ref.dtype)

    @pl.when(i >= nv_ref[0])
    def _():
        o_ref[...] = jnp.zeros_like(o_ref)


def mm_swiglu(x, wg, wu, tile_expert, n_valid, tm, tn):
    m, k = x.shape
    f = wg.shape[2]

    def xmap(i, j, te, nv):
        return (jnp.minimum(i, nv[0] - 1), 0)

    def wmap(i, j, te, nv):
        return (te[jnp.minimum(i, nv[0] - 1)], 0, j)

    return pl.pallas_call(
        _swiglu_kernel,
        grid_spec=pltpu.PrefetchScalarGridSpec(
            num_scalar_prefetch=2,
            grid=(m // tm, f // tn),
            in_specs=[pl.BlockSpec((tm, k), xmap),
                      pl.BlockSpec((1, k, tn), wmap),
                      pl.BlockSpec((1, k, tn), wmap)],
            out_specs=pl.BlockSpec((tm, tn), lambda i, j, te, nv: (i, j)),
        ),
        out_shape=jax.ShapeDtypeStruct((m, f), BF16),
        compiler_params=_cp("parallel", "arbitrary"),
        name="mm_swiglu",
    )(tile_expert, n_valid, x, wg, wu)


def _mmk_kernel(te_ref, nv_ref, x_ref, w_ref, *rest, nk, has_res):
    if has_res:
        res_ref, g_ref, o_ref = rest
    else:
        (o_ref,) = rest
    i = pl.program_id(0)
    k = pl.program_id(2)

    def finish(acc):
        if has_res:
            return res_ref[...] + g_ref[0] * acc
        return acc

    @pl.when(i < nv_ref[0])
    def _():
        part = jnp.dot(x_ref[...].astype(BF16), w_ref[0], preferred_element_type=F32)
        if nk == 1:
            o_ref[...] = finish(part)
        else:
            @pl.when(k == 0)
            def _():
                o_ref[...] = part

            if nk > 2:
                @pl.when((k > 0) & (k < nk - 1))
                def _():
                    o_ref[...] += part

            @pl.when(k == nk - 1)
            def _():
                o_ref[...] = finish(o_ref[...] + part)

    @pl.when(i >= nv_ref[0])
    def _():
        o_ref[...] = jnp.zeros_like(o_ref)


def mm_k(x, w, tile_expert, n_valid, tm, tn, tk, res=None, modtab=None, gate_row=None,
         n_ctx=0, t_lat=1):
    m, k = x.shape
    n = w.shape[2]
    nk = k // tk

    def xmap(i, j, kk, te, nv):
        return (jnp.minimum(i, nv[0] - 1), kk)

    def wmap(i, j, kk, te, nv):
        return (te[jnp.minimum(i, nv[0] - 1)], kk, j)

    in_specs = [pl.BlockSpec((tm, tk), xmap), pl.BlockSpec((1, tk, tn), wmap)]
    args = [x, w]
    if res is not None:
        in_specs.append(pl.BlockSpec((tm, tn), lambda i, j, kk, te, nv: (i, j)))
        in_specs.append(pl.BlockSpec(
            (1, 1, tn),
            lambda i, j, kk, te, nv: (_group_of_tile(i, tm, n_ctx, t_lat) * 6 + gate_row, 0, j)))
        args += [res, modtab]
    return pl.pallas_call(
        functools.partial(_mmk_kernel, nk=nk, has_res=res is not None),
        grid_spec=pltpu.PrefetchScalarGridSpec(
            num_scalar_prefetch=2,
            grid=(m // tm, n // tn, nk),
            in_specs=in_specs,
            out_specs=pl.BlockSpec((tm, tn), lambda i, j, kk, te, nv: (i, j)),
        ),
        out_shape=jax.ShapeDtypeStruct((m, n), F32),
        compiler_params=_cp("parallel", "parallel", "arbitrary"),
        name="mm_k",
    )(tile_expert, n_valid, *args)


def _one_expert_tables(m, tm):
    return jnp.zeros((m // tm,), jnp.int32), jnp.full((1,), m // tm, jnp.int32)


def rope_tables(rows, head_dim):
    n_freq = head_dim // 4
    inv = ROPE_THETA ** (-jnp.arange(n_freq, dtype=F32) / n_freq)
    r = jnp.repeat(jnp.arange(rows, dtype=F32), GRID_W)
    cl = jnp.tile(jnp.arange(GRID_W, dtype=F32), rows)
    ang = jnp.concatenate([r[:, None] * inv, cl[:, None] * inv], axis=-1)
    return jnp.cos(ang), jnp.sin(ang)


def _attn_prep_kernel(x_ref, cos_ref, sin_ref, qn_ref, kn_ref, q_ref, k_ref, v_ref, kf_ref,
                      *, hq, hkv, dh, n_ctx, tm):
    is_lat = pl.program_id(0) * tm >= n_ctx
    c = jnp.where(is_lat, cos_ref[...], 1.0)
    s = jnp.where(is_lat, sin_ref[...], 0.0)
    half = dh // 2
    for h in range(hq + hkv):
        xh = x_ref[:, h * dh:(h + 1) * dh]
        w = qn_ref[...] if h < hq else kn_ref[...]
        y = xh * lax.rsqrt(jnp.mean(xh * xh, axis=-1, keepdims=True) + EPS) * w
        y1, y2 = y[:, :half], y[:, half:]
        o1 = y1 * c - y2 * s
        o2 = y2 * c + y1 * s
        if h < hq:
            scale = dh ** -0.5
            q_ref[:, h * dh:h * dh + half] = (o1 * scale).astype(BF16)
            q_ref[:, h * dh + half:(h + 1) * dh] = (o2 * scale).astype(BF16)
        else:
            b = (h - hq) * dh
            k_ref[:, b:b + half] = o1.astype(BF16)
            k_ref[:, b + half:b + dh] = o2.astype(BF16)
            kf_ref[:, b:b + half] = o1
            kf_ref[:, b + half:b + dh] = o2
    v_ref[...] = x_ref[:, (hq + hkv) * dh:].astype(BF16)


def attn_prep(qkv, cos, sin, q_norm, k_norm, n_ctx, t_lat, hq, hkv, dh, tm=256):
    m = qkv.shape[0]
    half = dh // 2

    def posmap(i):
        row = i * tm
        return (jnp.where(row >= n_ctx, ((row - n_ctx) % t_lat) // tm, 0), 0)

    return pl.pallas_call(
        functools.partial(_attn_prep_kernel, hq=hq, hkv=hkv, dh=dh, n_ctx=n_ctx, tm=tm),
        grid=(m // tm,),
        in_specs=[pl.BlockSpec((tm, (hq + 2 * hkv) * dh), lambda i: (i, 0)),
                  pl.BlockSpec((tm, half), posmap),
                  pl.BlockSpec((tm, half), posmap),
                  pl.BlockSpec((1, dh), lambda i: (0, 0)),
                  pl.BlockSpec((1, dh), lambda i: (0, 0))],
        out_specs=[pl.BlockSpec((tm, hq * dh), lambda i: (i, 0)),
                   pl.BlockSpec((tm, hkv * dh), lambda i: (i, 0)),
                   pl.BlockSpec((tm, hkv * dh), lambda i: (i, 0)),
                   pl.BlockSpec((tm, hkv * dh), lambda i: (i, 0))],
        out_shape=[jax.ShapeDtypeStruct((m, hq * dh), BF16),
                   jax.ShapeDtypeStruct((m, hkv * dh), BF16),
                   jax.ShapeDtypeStruct((m, hkv * dh), BF16),
                   jax.ShapeDtypeStruct((m, hkv * dh), F32)],
        compiler_params=_cp("parallel"),
        name="attn_prep",
    )(qkv, cos, sin, q_norm.reshape(1, dh), k_norm.reshape(1, dh))


def _attn_kernel(*refs, n_src, n_group, dh):
    q_ref = refs[0]
    kv = refs[1:1 + 2 * n_src]
    o_ref = refs[-1]
    nt = (((1,), (1,)), ((), ()))
    for g in range(n_group):
        qg = q_ref[:, g * dh:(g + 1) * dh]
        ss = [lax.dot_general(qg, kv[2 * i][...], nt, preferred_element_type=F32)
              for i in range(n_src)]
        mx = functools.reduce(jnp.maximum, [jnp.max(s, axis=-1, keepdims=True) for s in ss])
        ps = [jnp.exp(s - mx) for s in ss]
        den = functools.reduce(jnp.add, [jnp.sum(p, axis=-1, keepdims=True) for p in ps])
        o = functools.reduce(jnp.add, [
            jnp.dot(p.astype(BF16), kv[2 * i + 1][...], preferred_element_type=F32)
            for i, p in enumerate(ps)])
        o_ref[:, g * dh:(g + 1) * dh] = (o / den).astype(o_ref.dtype)


def attention(q, k, v, cache, *, batch, seq, row0, hq, hkv, dh, tq):
    n_group = hq // hkv
    nq = seq // tq
    qb0, kb0 = row0 // tq, row0 // seq
    in_specs = [pl.BlockSpec((tq, n_group * dh), lambda b, h, t: (qb0 + b * nq + t, h))]
    args = [q]
    if cache is not None:
        p_len = cache[0].shape[0] // batch
        in_specs += [pl.BlockSpec((p_len, dh), lambda b, h, t: (b, h))] * 2
        args += list(cache)
    in_specs += [pl.BlockSpec((seq, dh), lambda b, h, t: (kb0 + b, h))] * 2
    args += [k, v]
    n_src = (len(args) - 1) // 2
    return pl.pallas_call(
        functools.partial(_attn_kernel, n_src=n_src, n_group=n_group, dh=dh),
        grid=(batch, hkv, nq),
        in_specs=in_specs,
        out_specs=pl.BlockSpec((tq, n_group * dh), lambda b, h, t: (b * nq + t, h)),
        out_shape=jax.ShapeDtypeStruct((batch * seq, hq * dh), BF16),
        compiler_params=_cp("parallel", "parallel", "arbitrary"),
        name="attention",
    )(*args)


def _dn_conv_kernel(x_ref, w_ref, o_ref, pad_ref, *, t, tc, n_ctx_blocks, l_ctx, l_lat,
                    n_q_tiles, n_qk_tiles, dk):
    j = pl.program_id(1)
    zeros8 = jnp.zeros((8, tc), F32)
    pad_ref[0:8, :] = zeros8
    pad_ref[t + 8:t + 16, :] = zeros8
    pad_ref[8:t + 8, :] = x_ref[...]
    w = w_ref[...]
    seq_len = jnp.where(pl.program_id(0) < n_ctx_blocks, l_ctx, l_lat)
    pos = lax.broadcasted_iota(jnp.int32, (t, tc), 0) & (seq_len - 1)
    y = (w[0:1] * jnp.where(pos >= 1, pad_ref[7:t + 7, :], 0.0)
         + w[1:2] * pad_ref[8:t + 8, :]
         + w[2:3] * jnp.where(pos <= seq_len - 2, pad_ref[9:t + 9, :], 0.0)
         + w[3:4] * jnp.where(pos <= seq_len - 3, pad_ref[10:t + 10, :], 0.0))
    y = _silu(y)
    is_qk = j < n_qk_tiles
    q_scale = jnp.where(j < n_q_tiles, dk ** -0.5, 1.0)
    for h in range(tc // LANES):
        yh = y[:, h * LANES:(h + 1) * LANES]
        ss = jnp.sum(yh * yh, axis=-1, keepdims=True)
        inv = jnp.where(is_qk, lax.rsqrt(ss + EPS) * q_scale, 1.0)
        o_ref[:, h * LANES:(h + 1) * LANES] = (yh * inv).astype(o_ref.dtype)


def dn_short_conv(proj, conv_w, *, n_ctx, l_ctx, l_lat, qk_dim, conv_dim, dk, tc=512):
    m = proj.shape[0]
    assert l_lat % l_ctx == 0 and n_ctx % l_lat == 0 and m % l_lat == 0
    assert l_ctx & (l_ctx - 1) == 0 and l_lat & (l_lat - 1) == 0 and l_ctx >= 4
    return pl.pallas_call(
        functools.partial(_dn_conv_kernel, t=l_lat, tc=tc, n_ctx_blocks=n_ctx // l_lat,
                          l_ctx=l_ctx, l_lat=l_lat, n_q_tiles=qk_dim // tc,
                          n_qk_tiles=2 * qk_dim // tc, dk=dk),
        grid=(m // l_lat, conv_dim // tc),
        in_specs=[pl.BlockSpec((l_lat, tc), lambda b, j: (b, j)),
                  pl.BlockSpec((conv_w.shape[0], tc), lambda b, j: (0, j))],
        out_specs=pl.BlockSpec((l_lat, tc), lambda b, j: (b, j)),
        out_shape=jax.ShapeDtypeStruct((m, conv_dim), BF16),
        scratch_shapes=[pltpu.VMEM((l_lat + 16, tc), F32)],
        compiler_params=_cp("parallel", "parallel"),
        name="dn_conv",
    )(proj, conv_w)


def _softplus(x):
    return jnp.maximum(x, 0.0) + jnp.log1p(jnp.exp(-jnp.abs(x)))


def _split_bf16(a):
    hi = a.astype(BF16)
    return hi, (a - hi.astype(F32)).astype(BF16)


def _mm_3pass(a, b):
    ah, al = _split_bf16(a)
    bh, bl = _split_bf16(b)
    return (jnp.dot(ah, bh, preferred_element_type=F32)
            + jnp.dot(al, bh, preferred_element_type=F32)
            + jnp.dot(ah, bl, preferred_element_type=F32))


def _pair_masks(n):
    r = lax.broadcasted_iota(jnp.int32, (n, n), 0)
    c = lax.broadcasted_iota(jnp.int32, (n, n), 1)
    return [((r >> (k + 1)) == (c >> (k + 1))) & ((r >> k) != (c >> k))
            for k in range(int(math.log2(n)))]


def _unit_tri_inverse(low, eye, masks):
    d = eye - jnp.where(masks[0], low, 0.0)
    for mask in masks[1:]:
        c = jnp.where(mask, low, 0.0)
        d = d - _mm_3pass(_mm_3pass(d, c), d)
    return d


def _dn_scan_kernel(rowf_ref, rowb_ref, flag_ref, seq_ref,
                    qf_ref, kf_ref, vf_ref, qb_ref, kb_ref, vb_ref, baf_ref, bab_ref,
                    nega_ref, dtb_ref, s0_ref, of_ref, ob_ref, sout_ref, s_scr, *, hb, chunk):
    step = pl.program_id(1)
    flag = flag_ref[step]
    first = (flag & 1) == 1
    last = (flag & 2) == 2
    is_lat = (flag & 4) == 4
    dh = DN_HEAD_DIM
    nt = (((1,), (1,)), ((), ()))

    @pl.when(first)
    def _():
        s_scr[...] = jnp.where(is_lat, s0_ref[0], 0.0)

    r = lax.broadcasted_iota(jnp.int32, (chunk, chunk), 0)
    c = lax.broadcasted_iota(jnp.int32, (chunk, chunk), 1)
    eye = (r == c).astype(F32)
    masks = _pair_masks(chunk)

    def direction(d, q_ref, k_ref, v_ref, ba_ref, o_ref):
        incl = (r >= c) if d == 0 else (r <= c)
        strict = (r > c) if d == 0 else (r < c)
        last_row = chunk - 1 if d == 0 else 0
        ba = ba_ref[...]
        sig = jax.nn.sigmoid(ba)
        g = nega_ref[0] * _softplus(ba + dtb_ref[0])
        gc = jnp.dot(incl.astype(F32), g, preferred_element_type=F32,
                     precision=lax.Precision.HIGHEST)
        gct = gc.T
        egc = jnp.exp(gc)
        glast = gc[last_row:last_row + 1, :]
        eglast = jnp.exp(glast)
        ekg = jnp.exp(glast - gc)
        for kh in range(hb // 2):
            qh = q_ref[:, kh * dh:(kh + 1) * dh]
            kk_ = k_ref[:, kh * dh:(kh + 1) * dh]
            kf32 = kk_.astype(F32)
            qf32 = qh.astype(F32)
            kkt = lax.dot_general(kk_, kk_, nt, preferred_element_type=F32)
            qkt = lax.dot_general(qh, kk_, nt, preferred_element_type=F32)
            for rr in range(2):
                hh = kh * 2 + rr
                cb = d * 2 * hb + hh
                ca = d * 2 * hb + hb + hh
                bcol = sig[:, cb:cb + 1]
                gcol = gc[:, ca:ca + 1]
                grow = gct[ca:ca + 1, :]
                decay = jnp.exp(jnp.where(incl, gcol - grow, -jnp.inf))
                low = jnp.where(strict, bcol * kkt * decay, 0.0)
                tinv = _unit_tri_inverse(low, eye, masks).astype(BF16)
                vbm = (v_ref[:, hh * dh:(hh + 1) * dh].astype(F32) * bcol).astype(BF16)
                kbg = (kf32 * (bcol * egc[:, ca:ca + 1])).astype(BF16)
                u = jnp.dot(tinv, vbm, preferred_element_type=F32)
                w = jnp.dot(tinv, kbg, preferred_element_type=F32)
                a = jnp.where(incl, qkt * decay, 0.0).astype(BF16)
                s = s_scr[d, hh]
                sb = s.astype(BF16)
                vnew = u - jnp.dot(w.astype(BF16), sb, preferred_element_type=F32)
                vnb = vnew.astype(BF16)
                qg = (qf32 * egc[:, ca:ca + 1]).astype(BF16)
                o = (jnp.dot(qg, sb, preferred_element_type=F32)
                     + jnp.dot(a, vnb, preferred_element_type=F32))
                o_ref[:, hh * dh:(hh + 1) * dh] = o
                kgt = (kf32 * ekg[:, ca:ca + 1]).T.astype(BF16)
                s_scr[d, hh] = s * eglast[:, ca:ca + 1] + jnp.dot(
                    kgt, vnb, preferred_element_type=F32)

    direction(0, qf_ref, kf_ref, vf_ref, baf_ref, of_ref)
    direction(1, qb_ref, kb_ref, vb_ref, bab_ref, ob_ref)

    @pl.when(last)
    def _():
        sout_ref[0] = s_scr[...]


def dn_scan(qkvc, ba, nega, dtb, s0, seq_lens, *, qk_dim, n_vheads, hb=DN_HEADS_PER_STEP):
    m = qkvc.shape[0]
    chunk = DN_CHUNK
    dh = DN_HEAD_DIM
    n_hg = n_vheads // hb
    rowf, rowb, flags, seqs = [], [], [], []
    row, sid, n_zero_seq = 0, 0, 0
    for n_seq, length, uses_s0 in seq_lens:
        n_chunks = length // chunk
        for _ in range(n_seq):
            for n in range(n_chunks):
                rowf.append(row + n)
                rowb.append(row + n_chunks - 1 - n)
                flags.append((n == 0) * 1 + (n == n_chunks - 1) * 2 + (4 if uses_s0 else 0))
                seqs.append(sid)
            row += n_chunks
            sid += 1
        if not uses_s0:
            n_zero_seq += n_seq
    n_steps = len(rowf)
    n_seq_total = sid
    tabs = [jnp.asarray(np.array(t, np.int32)) for t in (rowf, rowb, flags, seqs)]
    qw, vw = (hb // 2) * dh, hb * dh
    kb0, vb0 = qk_dim // qw, 2 * qk_dim // vw

    def spec(width, col0, rows_idx):
        return pl.BlockSpec((chunk, width),
                            lambda hg, s, rf, rb, fl, sq: ((rf, rb)[rows_idx][s], col0 + hg))

    in_specs = [spec(qw, 0, 0), spec(qw, kb0, 0), spec(vw, vb0, 0),
                spec(qw, 0, 1), spec(qw, kb0, 1), spec(vw, vb0, 1),
                pl.BlockSpec((chunk, LANES), lambda hg, s, rf, rb, fl, sq: (rf[s], hg)),
                pl.BlockSpec((chunk, LANES), lambda hg, s, rf, rb, fl, sq: (rb[s], hg)),
                pl.BlockSpec((1, 1, LANES), lambda hg, s, rf, rb, fl, sq: (hg, 0, 0)),
                pl.BlockSpec((1, 1, LANES), lambda hg, s, rf, rb, fl, sq: (hg, 0, 0)),
                pl.BlockSpec((1, 2, hb, dh, dh),
                             lambda hg, s, rf, rb, fl, sq:
                             (jnp.maximum(sq[s] - n_zero_seq, 0), 0, hg, 0, 0))]
    out_specs = [pl.BlockSpec((chunk, vw), lambda hg, s, rf, rb, fl, sq: (rf[s], hg)),
                 pl.BlockSpec((chunk, vw), lambda hg, s, rf, rb, fl, sq: (rb[s], hg)),
                 pl.BlockSpec((1, 2, hb, dh, dh),
                              lambda hg, s, rf, rb, fl, sq: (sq[s], 0, hg, 0, 0))]
    return pl.pallas_call(
        functools.partial(_dn_scan_kernel, hb=hb, chunk=chunk),
        grid_spec=pltpu.PrefetchScalarGridSpec(
            num_scalar_prefetch=4,
            grid=(n_hg, n_steps),
            in_specs=in_specs,
            out_specs=out_specs,
            scratch_shapes=[pltpu.VMEM((2, hb, dh, dh), F32)],
        ),
        out_shape=[jax.ShapeDtypeStruct((m, n_vheads * dh), F32),
                   jax.ShapeDtypeStruct((m, n_vheads * dh), F32),
                   jax.ShapeDtypeStruct((n_seq_total, 2, n_vheads, dh, dh), F32)],
        compiler_params=_cp("parallel", "arbitrary"),
        name="dn_scan",
    )(*tabs, qkvc, qkvc, qkvc, qkvc, qkvc, qkvc, ba, ba, nega, dtb, s0)


def _dn_gate_kernel(of_ref, ob_ref, z_ref, w_ref, o_ref, *, n_heads):
    w = w_ref[...]
    for h in range(n_heads):
        sl = slice(h * DN_HEAD_DIM, (h + 1) * DN_HEAD_DIM)
        o = of_ref[:, sl] + ob_ref[:, sl]
        y = o * lax.rsqrt(jnp.mean(o * o, axis=-1, keepdims=True) + EPS) * w
        o_ref[:, sl] = (y * _silu(z_ref[:, sl])).astype(o_ref.dtype)


def dn_gate(o_f, o_b, proj, out_norm, z_col0, tm=256, tn=1024):
    m, n = o_f.shape
    zb0 = z_col0 // tn
    return pl.pallas_call(
        functools.partial(_dn_gate_kernel, n_heads=tn // DN_HEAD_DIM),
        grid=(m // tm, n // tn),
        in_specs=[pl.BlockSpec((tm, tn), lambda i, j: (i, j)),
                  pl.BlockSpec((tm, tn), lambda i, j: (i, j)),
                  pl.BlockSpec((tm, tn), lambda i, j: (i, zb0 + j)),
                  pl.BlockSpec((1, DN_HEAD_DIM), lambda i, j: (0, 0))],
        out_specs=pl.BlockSpec((tm, tn), lambda i, j: (i, j)),
        out_shape=jax.ShapeDtypeStruct((m, n), BF16),
        compiler_params=_cp("parallel", "parallel"),
        name="dn_gate",
    )(o_f, o_b, proj, out_norm.reshape(1, DN_HEAD_DIM))


def _router_kernel(x_ref, whi_ref, wlo_ref, g_ref, s_ref, *, n_experts):
    x = x_ref[...]
    xhi = x.astype(BF16)
    xlo = (x - xhi.astype(F32)).astype(BF16)
    lg = (jnp.dot(xhi, whi_ref[...], preferred_element_type=F32)
          + jnp.dot(xlo, whi_ref[...], preferred_element_type=F32)
          + jnp.dot(xhi, wlo_ref[...], preferred_element_type=F32))
    lane = lax.broadcasted_iota(jnp.int32, lg.shape, 1).astype(F32)
    neg = -jnp.inf
    lg = jnp.where(lane < n_experts, lg, neg)
    m1 = jnp.max(lg, axis=-1, keepdims=True)
    i1 = jnp.min(jnp.where(lg == m1, lane, float(LANES)), axis=-1, keepdims=True)
    mk1 = lane == i1
    lg2 = jnp.where(mk1, neg, lg)
    m2 = jnp.max(lg2, axis=-1, keepdims=True)
    i2 = jnp.min(jnp.where(lg2 == m2, lane, float(LANES)), axis=-1, keepdims=True)
    mk2 = lane == i2
    e = jnp.exp(m2 - m1)
    w1 = 1.0 / (1.0 + e)
    g_ref[...] = jnp.where(mk1, w1, 0.0) + jnp.where(mk2, e * w1, 0.0)
    s_ref[...] = jnp.where(mk1, 1.0, 0.0) + jnp.where(mk2, 1.0, 0.0)


def router(x, w_router, tm=512):
    m, d = x.shape
    e = w_router.shape[1]
    wp = jnp.zeros((d, LANES), F32).at[:, :e].set(w_router)
    whi = wp.astype(BF16)
    wlo = (wp - whi.astype(F32)).astype(BF16)
    return pl.pallas_call(
        functools.partial(_router_kernel, n_experts=e),
        grid=(m // tm,),
        in_specs=[pl.BlockSpec((tm, d), lambda i: (i, 0)),
                  pl.BlockSpec((d, LANES), lambda i: (0, 0)),
                  pl.BlockSpec((d, LANES), lambda i: (0, 0))],
        out_specs=[pl.BlockSpec((tm, LANES), lambda i: (i, 0))] * 2,
        out_shape=[jax.ShapeDtypeStruct((m, LANES), F32)] * 2,
        compiler_params=_cp("parallel"),
        name="router",
    )(x, whi, wlo)


def _row_copy(src_hbm, row, dst_ref, r, sem):
    return pltpu.make_async_copy(src_hbm.at[pl.ds(row, 1), :], dst_ref.at[pl.ds(r, 1), :], sem)


def _gather_kernel(idx_ref, x_hbm, o_ref, sem, *, tg):
    def start(r, carry):
        _row_copy(x_hbm, idx_ref[0, 0, r], o_ref, r, sem).start()
        return carry

    def wait(r, carry):
        _row_copy(x_hbm, idx_ref[0, 0, r], o_ref, r, sem).wait()
        return carry

    lax.fori_loop(0, tg, start, 0)
    lax.fori_loop(0, tg, wait, 0)


def gather_rows(x, idx, tg=256):
    ms = idx.shape[0]
    d = x.shape[1]
    return pl.pallas_call(
        functools.partial(_gather_kernel, tg=tg),
        grid=(ms // tg,),
        in_specs=[pl.BlockSpec((1, 1, tg), lambda i: (i, 0, 0), memory_space=pltpu.SMEM),
                  pl.BlockSpec(memory_space=pl.ANY)],
        out_specs=pl.BlockSpec((tg, d), lambda i: (i, 0)),
        out_shape=jax.ShapeDtypeStruct((ms, d), x.dtype),
        scratch_shapes=[pltpu.SemaphoreType.DMA(())],
        compiler_params=_cp("arbitrary"),
        name="gather_rows",
    )(idx.reshape(ms // tg, 1, tg), x)


def _combine_kernel(pos_ref, ys_hbm, g_ref, res_ref, gm_ref, o_ref, buf, sem, *, tc, n_sel):
    def start(r, carry):
        for j in range(n_sel):
            _row_copy(ys_hbm, pos_ref[0, 0, r * n_sel + j], buf.at[j], r, sem).start()
        return carry

    def wait(r, carry):
        for j in range(n_sel):
            _row_copy(ys_hbm, pos_ref[0, 0, r * n_sel + j], buf.at[j], r, sem).wait()
        return carry

    lax.fori_loop(0, tc, start, 0)
    lax.fori_loop(0, tc, wait, 0)
    acc = g_ref[:, 0:1] * buf[0]
    for j in range(1, n_sel):
        acc = acc + g_ref[:, j:j + 1] * buf[j]
    o_ref[...] = res_ref[...] + gm_ref[0] * acc


def moe_combine(ys, pos, gates, res, modtab, gate_row, n_ctx, t_lat, tc=128):
    m, d = res.shape
    n_sel = pos.shape[1]
    return pl.pallas_call(
        functools.partial(_combine_kernel, tc=tc, n_sel=n_sel),
        grid=(m // tc,),
        in_specs=[pl.BlockSpec((1, 1, tc * n_sel), lambda i: (i, 0, 0), memory_space=pltpu.SMEM),
                  pl.BlockSpec(memory_space=pl.ANY),
                  pl.BlockSpec((tc, n_sel), lambda i: (i, 0)),
                  pl.BlockSpec((tc, d), lambda i: (i, 0)),
                  pl.BlockSpec((1, 1, d),
                               lambda i: (_group_of_tile(i, tc, n_ctx, t_lat) * 6 + gate_row, 0, 0))],
        out_specs=pl.BlockSpec((tc, d), lambda i: (i, 0)),
        out_shape=jax.ShapeDtypeStruct((m, d), F32),
        scratch_shapes=[pltpu.VMEM((n_sel, tc, d), F32), pltpu.SemaphoreType.DMA(())],
        compiler_params=_cp("arbitrary"),
        name="moe_combine",
    )(pos.reshape(m // tc, 1, tc * n_sel), ys, gates, res, modtab)


def routing_tables(gates, n_experts, tm):
    m = gates.shape[0]
    g = gates[:, :n_experts]
    pairs = jnp.ones((m, n_experts), bool)
    pi = pairs.astype(jnp.int32)
    cnt = jnp.sum(pi, axis=0)
    padded = ((cnt + tm - 1) // tm) * tm
    ends = jnp.cumsum(padded)
    starts = ends - padded
    rank = jnp.cumsum(pi, axis=0) - pi
    pos = starts[None, :] + rank
    ms = n_experts * m
    tok = jnp.broadcast_to(jnp.arange(m, dtype=jnp.int32)[:, None], pos.shape)
    tok_sorted = jnp.zeros((ms,), jnp.int32).at[jnp.where(pairs, pos, ms).reshape(-1)].set(
        tok.reshape(-1), mode="drop")
    tile_start = jnp.arange(ms // tm, dtype=jnp.int32) * tm
    tile_expert = jnp.minimum(
        jnp.sum(tile_start[:, None] >= ends[None, :], axis=1), n_experts - 1).astype(jnp.int32)
    n_valid = (ends[-1] // tm).astype(jnp.int32).reshape(1)
    return tok_sorted, tile_expert, n_valid, pos.astype(jnp.int32), g


def moe_layer(h, x_res, w_router, wg, wu, wd, modtab, gate_row, n_ctx, t_lat, tm=512):
    n_experts = wg.shape[0]
    f = wg.shape[2]
    gates, _ = router(h, w_router)
    tok_sorted, te, nv, pos_sel, g_sel = routing_tables(gates, n_experts, tm)
    xs = gather_rows(h, tok_sorted)
    act = mm_swiglu(xs, wg, wu, te, nv, tm=tm, tn=1024)
    ys = mm_k(act, wd, te, nv, tm=tm, tn=min(1024, wd.shape[2]), tk=f // 2)
    return moe_combine(ys, pos_sel, g_sel, x_res, modtab, gate_row, n_ctx, t_lat)


def _dn_ba_layout(hb):
    n_hg = DN_V_HEADS // hb
    idx = np.full((n_hg, LANES), -1, np.int64)
    for hg in range(n_hg):
        for d in range(2):
            for ab in range(2):
                for hh in range(hb):
                    idx[hg, d * 2 * hb + ab * hb + hh] = d * 2 * DN_V_HEADS + ab * DN_V_HEADS + hg * hb + hh
    return idx.reshape(-1)


def _permute_cols(a, idx):
    valid = jnp.asarray(idx >= 0)
    return jnp.where(valid, jnp.take(a, jnp.asarray(np.maximum(idx, 0)), axis=-1), 0.0)


def kernel(x_prompt, x_sample, state_dn, cache_k, cache_v, c, c_ctx, w_mod, b_mod, norm_mix, norm_ffn, norm_final, dn_w_in, dn_conv, dn_A_log, dn_dt_bias, dn_out_norm, dn_w_out, att_w_in, att_q_norm, att_k_norm, att_w_out, ffn_w_gate, ffn_w_up, ffn_w_down, moe_router, moe_w_gate, moe_w_up, moe_w_down):
    bc, tc_, d = x_prompt.shape
    bx, tx, _ = x_sample.shape
    depth = w_mod.shape[0]
    n_ctx = bc * tc_
    m = n_ctx + bx * tx
    past = cache_k.shape[2]
    qk_dim = DN_K_HEADS * DN_HEAD_DIM
    v_dim = DN_V_HEADS * DN_HEAD_DIM
    conv_dim = 2 * qk_dim + v_dim
    kv_dim = ATT_KV_HEADS * ATT_HEAD_DIM
    q_dim = ATT_Q_HEADS * ATT_HEAD_DIM

    x = jnp.concatenate([x_prompt.reshape(n_ctx, d), x_sample.reshape(bx * tx, d)], axis=0)
    n_groups = 16
    cvec = jnp.zeros((n_groups, d), F32).at[0].set(c_ctx).at[1:1 + bx].set(c)
    mods = adaln_all(cvec, w_mod, b_mod)
    cos, sin = rope_tables(tx // GRID_W, ATT_HEAD_DIM)
    ba_idx = _dn_ba_layout(DN_HEADS_PER_STEP)
    n_hg = DN_V_HEADS // DN_HEADS_PER_STEP

    new_dn, new_k, new_v = [], [], []
    for i in range(depth):
        j = i // 2
        modtab = mods[i].reshape(n_groups * 6, 1, d)
        h = norm_mod(x, norm_mix[i], modtab, (0, 1), n_ctx, tx, BF16)
        if i % 2 == 0:
            w_in = dn_w_in[j]
            proj = mm(h, w_in[:, :conv_dim + v_dim].astype(BF16), F32)
            ba = mm(h, _permute_cols(w_in[:, conv_dim + v_dim:], ba_idx).astype(BF16), F32)
            old = jnp.zeros((2, 2, DN_V_HEADS), F32)
            nega_old = old.at[:, 1].set(-jnp.exp(dn_A_log[j].astype(F32))).reshape(-1)
            dtb_old = old.at[:, 1].set(dn_dt_bias[j].astype(F32)).reshape(-1)
            nega = _permute_cols(nega_old, ba_idx).reshape(n_hg, 1, LANES)
            dtb = _permute_cols(dtb_old, ba_idx).reshape(n_hg, 1, LANES)
            qkvc = dn_short_conv(proj, dn_conv[j], n_ctx=n_ctx, l_ctx=tc_, l_lat=tx,
                                 qk_dim=qk_dim, conv_dim=conv_dim, dk=DN_HEAD_DIM)
            o_f, o_b, s_out = dn_scan(qkvc, ba, nega, dtb, state_dn[:, j].astype(F32),
                                      [(bc, tc_, False), (bx, tx, True)],
                                      qk_dim=qk_dim, n_vheads=DN_V_HEADS)
            new_dn.append(s_out[:bc])
            og = dn_gate(o_f, o_b, proj, dn_out_norm[j], conv_dim)
            te, nv = _one_expert_tables(m, 512)
            x = mm_k(og, dn_w_out[j].astype(BF16)[None], te, nv, 512, 1024, v_dim,
                     res=x, modtab=modtab, gate_row=2, n_ctx=n_ctx, t_lat=tx)
        else:
            qkv = mm(h, att_w_in[j].astype(BF16), F32)
            q, k, v, kf = attn_prep(qkv, cos, sin, att_q_norm[j], att_k_norm[j], n_ctx, tx,
                                    ATT_Q_HEADS, ATT_KV_HEADS, ATT_HEAD_DIM)
            o_ctx = attention(q, k, v, None, batch=bc, seq=tc_, row0=0,
                              hq=ATT_Q_HEADS, hkv=ATT_KV_HEADS, dh=ATT_HEAD_DIM, tq=tc_)
            ck = cache_k[:, j].reshape(bx * past, kv_dim).astype(BF16)
            cv = cache_v[:, j].reshape(bx * past, kv_dim).astype(BF16)
            o_lat = attention(q, k, v, (ck, cv), batch=bx, seq=tx, row0=n_ctx,
                              hq=ATT_Q_HEADS, hkv=ATT_KV_HEADS, dh=ATT_HEAD_DIM, tq=256)
            o = jnp.concatenate([o_ctx, o_lat], axis=0)
            new_k.append(kf[:n_ctx].reshape(bc, tc_, ATT_KV_HEADS, ATT_HEAD_DIM))
            new_v.append(qkv[:n_ctx, q_dim + kv_dim:].reshape(bc, tc_, ATT_KV_HEADS, ATT_HEAD_DIM))
            te, nv = _one_expert_tables(m, 512)
            x = mm_k(o, att_w_out[j].astype(BF16)[None], te, nv, 512, 1024, q_dim,
                     res=x, modtab=modtab, gate_row=2, n_ctx=n_ctx, t_lat=tx)
        if i % 2 == 0:
            h = norm_mod(x, norm_ffn[i], modtab, (3, 4), n_ctx, tx, BF16)
            f = ffn_w_gate.shape[2]
            te, nv = _one_expert_tables(m, 1024)
            act = mm_swiglu(h, ffn_w_gate[j].astype(BF16)[None], ffn_w_up[j].astype(BF16)[None],
                            te, nv, tm=1024, tn=512)
            te, nv = _one_expert_tables(m, 512)
            x = mm_k(act, ffn_w_down[j].astype(BF16)[None], te, nv, 512, 1024, f // 2,
                     res=x, modtab=modtab, gate_row=5, n_ctx=n_ctx, t_lat=tx)
        else:
            h = norm_mod(x, norm_ffn[i], modtab, (3, 4), n_ctx, tx, F32)
            x = moe_layer(h, x, moe_router[j], moe_w_gate[j].astype(BF16),
                          moe_w_up[j].astype(BF16), moe_w_down[j].astype(BF16),
                          modtab, 5, n_ctx, tx)
    y = norm_mod(x, norm_final, None, (), n_ctx, tx, F32)
    y_prompt = y[:n_ctx].reshape(bc, tc_, d)
    y_sample = y[n_ctx:].reshape(bx, tx, d)
    return (y_prompt, y_sample, jnp.stack(new_dn, axis=1),
            jnp.stack(new_k, axis=1), jnp.stack(new_v, axis=1))
```

```python
import functools
import math

import numpy as np
import jax
import jax.numpy as jnp
from jax import lax
from jax.experimental import pallas as pl
from jax.experimental.pallas import tpu as pltpu

F32 = jnp.float32
BF16 = jnp.bfloat16
EPS = 1e-6

GRID_W = 64
ROPE_THETA = 10000.0
DN_K_HEADS = 16
DN_V_HEADS = 32
DN_HEAD_DIM = 128
DN_CHUNK = 64
ATT_Q_HEADS = 16
ATT_KV_HEADS = 2
ATT_HEAD_DIM = 256
N_EXPERTS = 8
TOP_K = 2

LANES = 128
VMEM_LIMIT_BYTES = 56 * 1024 * 1024

DN_HEADS_PER_STEP = 8


def _cp(*sem):
    return pltpu.CompilerParams(dimension_semantics=sem, vmem_limit_bytes=VMEM_LIMIT_BYTES)


def _silu(x):
    return x * jax.nn.sigmoid(x)


def _group_of_tile(i, tm, n_ctx, t_lat):
    row = i * tm
    return jnp.where(row < n_ctx, 0, 1 + (row - n_ctx) // t_lat)


def _adaln_kernel(c_ref, w_ref, b_ref, o_ref):
    s = _silu(c_ref[...]).astype(BF16)
    o_ref[0] = jnp.dot(s, w_ref[0].astype(BF16), preferred_element_type=F32) + b_ref[0]


def adaln_all(cvec, w_mod, b_mod, tn=1024):
    n_layers, d, n = w_mod.shape
    g = cvec.shape[0]
    return pl.pallas_call(
        _adaln_kernel,
        grid=(n_layers, n // tn),
        in_specs=[
            pl.BlockSpec((g, d), lambda l, j: (0, 0)),
            pl.BlockSpec((1, d, tn), lambda l, j: (l, 0, j)),
            pl.BlockSpec((1, 1, tn), lambda l, j: (l, 0, j)),
        ],
        out_specs=pl.BlockSpec((1, g, tn), lambda l, j: (l, 0, j)),
        out_shape=jax.ShapeDtypeStruct((n_layers, g, n), F32),
        compiler_params=_cp("parallel", "parallel"),
        name="adaln",
    )(cvec, w_mod, b_mod.reshape(n_layers, 1, n))


def _norm_mod_kernel(x_ref, w_ref, *rest, modulated):
    x = x_ref[...]
    y = x * lax.rsqrt(jnp.mean(x * x, axis=-1, keepdims=True) + EPS) * w_ref[...]
    if modulated:
        sh_ref, sc_ref, o_ref = rest
        y = y * (1.0 + sc_ref[0]) + sh_ref[0]
    else:
        (o_ref,) = rest
    o_ref[...] = y.astype(o_ref.dtype)


def norm_mod(x, w, modtab, rows, n_ctx, t_lat, out_dtype, tm=512):
    m, d = x.shape
    in_specs = [pl.BlockSpec((tm, d), lambda i: (i, 0)), pl.BlockSpec((1, d), lambda i: (0, 0))]
    args = [x, w.reshape(1, d)]
    if modtab is not None:
        for r in rows:
            in_specs.append(pl.BlockSpec(
                (1, 1, d), lambda i, r=r: (_group_of_tile(i, tm, n_ctx, t_lat) * 6 + r, 0, 0)))
            args.append(modtab)
    return pl.pallas_call(
        functools.partial(_norm_mod_kernel, modulated=modtab is not None),
        grid=(m // tm,),
        in_specs=in_specs,
        out_specs=pl.BlockSpec((tm, d), lambda i: (i, 0)),
        out_shape=jax.ShapeDtypeStruct((m, d), out_dtype),
        compiler_params=_cp("parallel"),
        name="norm_mod",
    )(*args)


def _mm_kernel(x_ref, w_ref, o_ref):
    o_ref[...] = jnp.dot(x_ref[...], w_ref[...], preferred_element_type=F32).astype(o_ref.dtype)


def mm(x, w, out_dtype, tm=1024, tn=1024):
    m, k = x.shape
    n = w.shape[1]
    tm, tn = min(tm, m), min(tn, n)
    return pl.pallas_call(
        _mm_kernel,
        grid=(m // tm, n // tn),
        in_specs=[pl.BlockSpec((tm, k), lambda i, j: (i, 0)),
                  pl.BlockSpec((k, tn), lambda i, j: (0, j))],
        out_specs=pl.BlockSpec((tm, tn), lambda i, j: (i, j)),
        out_shape=jax.ShapeDtypeStruct((m, n), out_dtype),
        compiler_params=_cp("parallel", "parallel"),
        name="mm",
    )(x, w)


def _swiglu_kernel(te_ref, nv_ref, x_ref, wg_ref, wu_ref, o_ref):
    i = pl.program_id(0)

    @pl.when(i < nv_ref[0])
    def _():
        x = x_ref[...].astype(BF16)
        g = jnp.dot(x, wg_ref[0], preferred_element_type=F32)
        u = jnp.dot(x, wu_ref[0], preferred_element_type=F32)
        o_ref[...] = (_silu(g) * u).astype(o_ref.dtype)

    @pl.when(i >= nv_ref[0])
    def _():
        o_ref[...] = jnp.zeros_like(o_ref)


def mm_swiglu(x, wg, wu, tile_expert, n_valid, tm, tn):
    m, k = x.shape
    f = wg.shape[2]

    def xmap(i, j, te, nv):
        return (jnp.minimum(i, nv[0] - 1), 0)

    def wmap(i, j, te, nv):
        return (te[jnp.minimum(i, nv[0] - 1)], 0, j)

    return pl.pallas_call(
        _swiglu_kernel,
        grid_spec=pltpu.PrefetchScalarGridSpec(
            num_scalar_prefetch=2,
            grid=(m // tm, f // tn),
            in_specs=[pl.BlockSpec((tm, k), xmap),
                      pl.BlockSpec((1, k, tn), wmap),
                      pl.BlockSpec((1, k, tn), wmap)],
            out_specs=pl.BlockSpec((tm, tn), lambda i, j, te, nv: (i, j)),
        ),
        out_shape=jax.ShapeDtypeStruct((m, f), BF16),
        compiler_params=_cp("parallel", "arbitrary"),
        name="mm_swiglu",
    )(tile_expert, n_valid, x, wg, wu)


def _mmk_kernel(te_ref, nv_ref, x_ref, w_ref, *rest, nk, has_res):
    if has_res:
        res_ref, g_ref, o_ref = rest
    else:
        (o_ref,) = rest
    i = pl.program_id(0)
    k = pl.program_id(2)

    def finish(acc):
        if has_res:
            return res_ref[...] + g_ref[0] * acc
        return acc

    @pl.when(i < nv_ref[0])
    def _():
        part = jnp.dot(x_ref[...].astype(BF16), w_ref[0], preferred_element_type=F32)
        if nk == 1:
            o_ref[...] = finish(part)
        else:
            @pl.when(k == 0)
            def _():
                o_ref[...] = part

            if nk > 2:
                @pl.when((k > 0) & (k < nk - 1))
                def _():
                    o_ref[...] += part

            @pl.when(k == nk - 1)
            def _():
                o_ref[...] = finish(o_ref[...] + part)

    @pl.when(i >= nv_ref[0])
    def _():
        o_ref[...] = jnp.zeros_like(o_ref)


def mm_k(x, w, tile_expert, n_valid, tm, tn, tk, res=None, modtab=None, gate_row=None,
         n_ctx=0, t_lat=1):
    m, k = x.shape
    n = w.shape[2]
    nk = k // tk

    def xmap(i, j, kk, te, nv):
        return (jnp.minimum(i, nv[0] - 1), kk)

    def wmap(i, j, kk, te, nv):
        return (te[jnp.minimum(i, nv[0] - 1)], kk, j)

    in_specs = [pl.BlockSpec((tm, tk), xmap), pl.BlockSpec((1, tk, tn), wmap)]
    args = [x, w]
    if res is not None:
        in_specs.append(pl.BlockSpec((tm, tn), lambda i, j, kk, te, nv: (i, j)))
        in_specs.append(pl.BlockSpec(
            (1, 1, tn),
            lambda i, j, kk, te, nv: (_group_of_tile(i, tm, n_ctx, t_lat) * 6 + gate_row, 0, j)))
        args += [res, modtab]
    return pl.pallas_call(
        functools.partial(_mmk_kernel, nk=nk, has_res=res is not None),
        grid_spec=pltpu.PrefetchScalarGridSpec(
            num_scalar_prefetch=2,
            grid=(m // tm, n // tn, nk),
            in_specs=in_specs,
            out_specs=pl.BlockSpec((tm, tn), lambda i, j, kk, te, nv: (i, j)),
        ),
        out_shape=jax.ShapeDtypeStruct((m, n), F32),
        compiler_params=_cp("parallel", "parallel", "arbitrary"),
        name="mm_k",
    )(tile_expert, n_valid, *args)


def _one_expert_tables(m, tm):
    return jnp.zeros((m // tm,), jnp.int32), jnp.full((1,), m // tm, jnp.int32)


def rope_tables(rows, head_dim):
    n_freq = head_dim // 4
    inv = ROPE_THETA ** (-jnp.arange(n_freq, dtype=F32) / n_freq)
    r = jnp.repeat(jnp.arange(rows, dtype=F32), GRID_W)
    cl = jnp.tile(jnp.arange(GRID_W, dtype=F32), rows)
    ang = jnp.concatenate([r[:, None] * inv, cl[:, None] * inv], axis=-1)
    return jnp.cos(ang), jnp.sin(ang)


def _attn_prep_kernel(x_ref, cos_ref, sin_ref, qn_ref, kn_ref, q_ref, k_ref, v_ref, kf_ref,
                      *, hq, hkv, dh, n_ctx, tm):
    is_lat = pl.program_id(0) * tm >= n_ctx
    c = jnp.where(is_lat, cos_ref[...], 1.0)
    s = jnp.where(is_lat, sin_ref[...], 0.0)
    half = dh // 2
    for h in range(hq + hkv):
        xh = x_ref[:, h * dh:(h + 1) * dh]
        w = qn_ref[...] if h < hq else kn_ref[...]
        y = xh * lax.rsqrt(jnp.mean(xh * xh, axis=-1, keepdims=True) + EPS) * w
        y1, y2 = y[:, :half], y[:, half:]
        o1 = y1 * c - y2 * s
        o2 = y2 * c + y1 * s
        if h < hq:
            scale = dh ** -0.5
            q_ref[:, h * dh:h * dh + half] = (o1 * scale).astype(BF16)
            q_ref[:, h * dh + half:(h + 1) * dh] = (o2 * scale).astype(BF16)
        else:
            b = (h - hq) * dh
            k_ref[:, b:b + half] = o1.astype(BF16)
            k_ref[:, b + half:b + dh] = o2.astype(BF16)
            kf_ref[:, b:b + half] = o1
            kf_ref[:, b + half:b + dh] = o2
    v_ref[...] = x_ref[:, (hq + hkv) * dh:].astype(BF16)


def attn_prep(qkv, cos, sin, q_norm, k_norm, n_ctx, t_lat, hq, hkv, dh, tm=256):
    m = qkv.shape[0]
    half = dh // 2

    def posmap(i):
        row = i * tm
        return (jnp.where(row >= n_ctx, ((row - n_ctx) % t_lat) // tm, 0), 0)

    return pl.pallas_call(
        functools.partial(_attn_prep_kernel, hq=hq, hkv=hkv, dh=dh, n_ctx=n_ctx, tm=tm),
        grid=(m // tm,),
        in_specs=[pl.BlockSpec((tm, (hq + 2 * hkv) * dh), lambda i: (i, 0)),
                  pl.BlockSpec((tm, half), posmap),
                  pl.BlockSpec((tm, half), posmap),
                  pl.BlockSpec((1, dh), lambda i: (0, 0)),
                  pl.BlockSpec((1, dh), lambda i: (0, 0))],
        out_specs=[pl.BlockSpec((tm, hq * dh), lambda i: (i, 0)),
                   pl.BlockSpec((tm, hkv * dh), lambda i: (i, 0)),
                   pl.BlockSpec((tm, hkv * dh), lambda i: (i, 0)),
                   pl.BlockSpec((tm, hkv * dh), lambda i: (i, 0))],
        out_shape=[jax.ShapeDtypeStruct((m, hq * dh), BF16),
                   jax.ShapeDtypeStruct((m, hkv * dh), BF16),
                   jax.ShapeDtypeStruct((m, hkv * dh), BF16),
                   jax.ShapeDtypeStruct((m, hkv * dh), F32)],
        compiler_params=_cp("parallel"),
        name="attn_prep",
    )(qkv, cos, sin, q_norm.reshape(1, dh), k_norm.reshape(1, dh))


def _attn_kernel(*refs, n_src, n_group, dh):
    q_ref = refs[0]
    kv = refs[1:1 + 2 * n_src]
    o_ref = refs[-1]
    nt = (((1,), (1,)), ((), ()))
    for g in range(n_group):
        qg = q_ref[:, g * dh:(g + 1) * dh]
        ss = [lax.dot_general(qg, kv[2 * i][...], nt, preferred_element_type=F32)
              for i in range(n_src)]
        mx = functools.reduce(jnp.maximum, [jnp.max(s, axis=-1, keepdims=True) for s in ss])
        ps = [jnp.exp(s - mx) for s in ss]
        den = functools.reduce(jnp.add, [jnp.sum(p, axis=-1, keepdims=True) for p in ps])
        o = functools.reduce(jnp.add, [
            jnp.dot(p.astype(BF16), kv[2 * i + 1][...], preferred_element_type=F32)
            for i, p in enumerate(ps)])
        o_ref[:, g * dh:(g + 1) * dh] = (o / den).astype(o_ref.dtype)


def attention(q, k, v, cache, *, batch, seq, row0, hq, hkv, dh, tq):
    n_group = hq // hkv
    nq = seq // tq
    qb0, kb0 = row0 // tq, row0 // seq
    in_specs = [pl.BlockSpec((tq, n_group * dh), lambda b, h, t: (qb0 + b * nq + t, h))]
    args = [q]
    if cache is not None:
        p_len = cache[0].shape[0] // batch
        in_specs += [pl.BlockSpec((p_len, dh), lambda b, h, t: (b, h))] * 2
        args += list(cache)
    in_specs += [pl.BlockSpec((seq, dh), lambda b, h, t: (kb0 + b, h))] * 2
    args += [k, v]
    n_src = (len(args) - 1) // 2
    return pl.pallas_call(
        functools.partial(_attn_kernel, n_src=n_src, n_group=n_group, dh=dh),
        grid=(batch, hkv, nq),
        in_specs=in_specs,
        out_specs=pl.BlockSpec((tq, n_group * dh), lambda b, h, t: (b * nq + t, h)),
        out_shape=jax.ShapeDtypeStruct((batch * seq, hq * dh), BF16),
        compiler_params=_cp("parallel", "parallel", "arbitrary"),
        name="attention",
    )(*args)


def _dn_conv_kernel(x_ref, w_ref, o_ref, pad_ref, *, t, tc, n_ctx_blocks, l_ctx, l_lat,
                    n_q_tiles, n_qk_tiles, dk):
    j = pl.program_id(1)
    zeros8 = jnp.zeros((8, tc), F32)
    pad_ref[0:8, :] = zeros8
    pad_ref[t + 8:t + 16, :] = zeros8
    pad_ref[8:t + 8, :] = x_ref[...]
    w = w_ref[...]
    seq_len = jnp.where(pl.program_id(0) < n_ctx_blocks, l_ctx, l_lat)
    pos = lax.broadcasted_iota(jnp.int32, (t, tc), 0) & (seq_len - 1)
    y = (w[0:1] * jnp.where(pos >= 1, pad_ref[7:t + 7, :], 0.0)
         + w[1:2] * pad_ref[8:t + 8, :]
         + w[2:3] * jnp.where(pos <= seq_len - 2, pad_ref[9:t + 9, :], 0.0)
         + w[3:4] * jnp.where(pos <= seq_len - 3, pad_ref[10:t + 10, :], 0.0))
    y = _silu(y)
    is_qk = j < n_qk_tiles
    q_scale = jnp.where(j < n_q_tiles, dk ** -0.5, 1.0)
    for h in range(tc // LANES):
        yh = y[:, h * LANES:(h + 1) * LANES]
        ss = jnp.sum(yh * yh, axis=-1, keepdims=True)
        inv = jnp.where(is_qk, lax.rsqrt(ss + EPS) * q_scale, 1.0)
        o_ref[:, h * LANES:(h + 1) * LANES] = (yh * inv).astype(o_ref.dtype)


def dn_short_conv(proj, conv_w, *, n_ctx, l_ctx, l_lat, qk_dim, conv_dim, dk, tc=512):
    m = proj.shape[0]
    assert l_lat % l_ctx == 0 and n_ctx % l_lat == 0 and m % l_lat == 0
    assert l_ctx & (l_ctx - 1) == 0 and l_lat & (l_lat - 1) == 0 and l_ctx >= 4
    return pl.pallas_call(
        functools.partial(_dn_conv_kernel, t=l_lat, tc=tc, n_ctx_blocks=n_ctx // l_lat,
                          l_ctx=l_ctx, l_lat=l_lat, n_q_tiles=qk_dim // tc,
                          n_qk_tiles=2 * qk_dim // tc, dk=dk),
        grid=(m // l_lat, conv_dim // tc),
        in_specs=[pl.BlockSpec((l_lat, tc), lambda b, j: (b, j)),
                  pl.BlockSpec((conv_w.shape[0], tc), lambda b, j: (0, j))],
        out_specs=pl.BlockSpec((l_lat, tc), lambda b, j: (b, j)),
        out_shape=jax.ShapeDtypeStruct((m, conv_dim), BF16),
        scratch_shapes=[pltpu.VMEM((l_lat + 16, tc), F32)],
        compiler_params=_cp("parallel", "parallel"),
        name="dn_conv",
    )(proj, conv_w)


def _softplus(x):
    return jnp.maximum(x, 0.0) + jnp.log1p(jnp.exp(-jnp.abs(x)))


def _split_bf16(a):
    hi = a.astype(BF16)
    return hi, (a - hi.astype(F32)).astype(BF16)


def _mm_3pass(a, b):
    ah, al = a
    bh, bl = b
    return (jnp.dot(ah, bh, preferred_element_type=F32)
            + jnp.dot(al, bh, preferred_element_type=F32)
            + jnp.dot(ah, bl, preferred_element_type=F32))


def _pair_masks(n):
    r = lax.broadcasted_iota(jnp.int32, (n, n), 0)
    c = lax.broadcasted_iota(jnp.int32, (n, n), 1)
    return [((r >> (k + 1)) == (c >> (k + 1))) & ((r >> k) != (c >> k))
            for k in range(int(math.log2(n)))]


def _unit_tri_inverses(lows, eye, masks):
    ds = [eye - jnp.where(masks[0], low, 0.0) for low in lows]
    for mask in masks[1:]:
        dsp = [_split_bf16(d) for d in ds]
        ts = [_mm_3pass(d, _split_bf16(jnp.where(mask, low, 0.0))) for d, low in zip(dsp, lows)]
        ds = [d - _mm_3pass(_split_bf16(t), dp) for d, t, dp in zip(ds, ts, dsp)]
    return ds


def _dn_scan_kernel(rowf_ref, rowb_ref, flag_ref, seq_ref,
                    qf_ref, kf_ref, vf_ref, qb_ref, kb_ref, vb_ref, baf_ref, bab_ref,
                    nega_ref, dtb_ref, s0_ref, of_ref, ob_ref, sout_ref, s_scr, *, hb, chunk):
    step = pl.program_id(1)
    flag = flag_ref[step]
    first = (flag & 1) == 1
    last = (flag & 2) == 2
    is_lat = (flag & 4) == 4
    dh = DN_HEAD_DIM
    nt = (((1,), (1,)), ((), ()))

    @pl.when(first)
    def _():
        s_scr[...] = jnp.where(is_lat, s0_ref[0], 0.0)

    r = lax.broadcasted_iota(jnp.int32, (chunk, chunk), 0)
    c = lax.broadcasted_iota(jnp.int32, (chunk, chunk), 1)
    eye = (r == c).astype(F32)
    masks = _pair_masks(chunk)

    heads = []
    lows = []
    for d, q_ref, k_ref, v_ref, ba_ref, o_ref in ((0, qf_ref, kf_ref, vf_ref, baf_ref, of_ref),
                                                   (1, qb_ref, kb_ref, vb_ref, bab_ref, ob_ref)):
        incl = (r >= c) if d == 0 else (r <= c)
        strict = (r > c) if d == 0 else (r < c)
        last_row = chunk - 1 if d == 0 else 0
        ba = ba_ref[...]
        sig = jax.nn.sigmoid(ba)
        g = nega_ref[0] * _softplus(ba + dtb_ref[0])
        gc = jnp.dot(incl.astype(F32), g, preferred_element_type=F32,
                     precision=lax.Precision.HIGHEST)
        gct = gc.T
        egc = jnp.exp(gc)
        glast = gc[last_row:last_row + 1, :]
        eglast = jnp.exp(glast)
        ekg = jnp.exp(glast - gc)
        for kh in range(hb // 2):
            qh = q_ref[:, kh * dh:(kh + 1) * dh]
            kk_ = k_ref[:, kh * dh:(kh + 1) * dh]
            kf32 = kk_.astype(F32)
            qf32 = qh.astype(F32)
            kkt = lax.dot_general(kk_, kk_, nt, preferred_element_type=F32)
            qkt = lax.dot_general(qh, kk_, nt, preferred_element_type=F32)
            for rr in range(2):
                hh = kh * 2 + rr
                cb = d * 2 * hb + hh
                ca = d * 2 * hb + hb + hh
                bcol = sig[:, cb:cb + 1]
                ecol = egc[:, ca:ca + 1]
                decay = jnp.exp(jnp.where(incl, gc[:, ca:ca + 1] - gct[ca:ca + 1, :], -jnp.inf))
                lows.append(jnp.where(strict, bcol * kkt * decay, 0.0))
                heads.append(dict(
                    d=d, hh=hh, o_ref=o_ref,
                    vbm=(v_ref[:, hh * dh:(hh + 1) * dh].astype(F32) * bcol).astype(BF16),
                    kbg=(kf32 * (bcol * ecol)).astype(BF16),
                    a=jnp.where(incl, qkt * decay, 0.0).astype(BF16),
                    qg=(qf32 * ecol).astype(BF16),
                    kgt=(kf32 * ekg[:, ca:ca + 1]).T.astype(BF16),
                    egl=eglast[:, ca:ca + 1]))

    tinvs = [t.astype(BF16) for t in _unit_tri_inverses(lows, eye, masks)]
    us = [jnp.dot(t, h["vbm"], preferred_element_type=F32) for t, h in zip(tinvs, heads)]
    ws = [jnp.dot(t, h["kbg"], preferred_element_type=F32) for t, h in zip(tinvs, heads)]
    ss = [s_scr[h["d"], h["hh"]] for h in heads]
    sbs = [s.astype(BF16) for s in ss]
    vnbs = [(u - jnp.dot(w.astype(BF16), sb, preferred_element_type=F32)).astype(BF16)
            for u, w, sb in zip(us, ws, sbs)]
    for h, sb, vnb in zip(heads, sbs, vnbs):
        hh = h["hh"]
        h["o_ref"][:, hh * dh:(hh + 1) * dh] = (
            jnp.dot(h["qg"], sb, preferred_element_type=F32)
            + jnp.dot(h["a"], vnb, preferred_element_type=F32))
    for h, s, vnb in zip(heads, ss, vnbs):
        s_scr[h["d"], h["hh"]] = s * h["egl"] + jnp.dot(h["kgt"], vnb,
                                                        preferred_element_type=F32)

    @pl.when(last)
    def _():
        sout_ref[0] = s_scr[...]


def dn_scan(qkvc, ba, nega, dtb, s0, seq_lens, *, qk_dim, n_vheads, hb=DN_HEADS_PER_STEP):
    m = qkvc.shape[0]
    chunk = DN_CHUNK
    dh = DN_HEAD_DIM
    n_hg = n_vheads // hb
    rowf, rowb, flags, seqs = [], [], [], []
    row, sid, n_zero_seq = 0, 0, 0
    for n_seq, length, uses_s0 in seq_lens:
        n_chunks = length // chunk
        for _ in range(n_seq):
            for n in range(n_chunks):
                rowf.append(row + n)
                rowb.append(row + n_chunks - 1 - n)
                flags.append((n == 0) * 1 + (n == n_chunks - 1) * 2 + (4 if uses_s0 else 0))
                seqs.append(sid)
            row += n_chunks
            sid += 1
        if not uses_s0:
            n_zero_seq += n_seq
    n_steps = len(rowf)
    n_seq_total = sid
    tabs = [jnp.asarray(np.array(t, np.int32)) for t in (rowf, rowb, flags, seqs)]
    qw, vw = (hb // 2) * dh, hb * dh
    kb0, vb0 = qk_dim // qw, 2 * qk_dim // vw

    def spec(width, col0, rows_idx):
        return pl.BlockSpec((chunk, width),
                            lambda hg, s, rf, rb, fl, sq: ((rf, rb)[rows_idx][s], col0 + hg))

    in_specs = [spec(qw, 0, 0), spec(qw, kb0, 0), spec(vw, vb0, 0),
                spec(qw, 0, 1), spec(qw, kb0, 1), spec(vw, vb0, 1),
                pl.BlockSpec((chunk, LANES), lambda hg, s, rf, rb, fl, sq: (rf[s], hg)),
                pl.BlockSpec((chunk, LANES), lambda hg, s, rf, rb, fl, sq: (rb[s], hg)),
                pl.BlockSpec((1, 1, LANES), lambda hg, s, rf, rb, fl, sq: (hg, 0, 0)),
                pl.BlockSpec((1, 1, LANES), lambda hg, s, rf, rb, fl, sq: (hg, 0, 0)),
                pl.BlockSpec((1, 2, hb, dh, dh),
                             lambda hg, s, rf, rb, fl, sq:
                             (jnp.maximum(sq[s] - n_zero_seq, 0), 0, hg, 0, 0))]
    out_specs = [pl.BlockSpec((chunk, vw), lambda hg, s, rf, rb, fl, sq: (rf[s], hg)),
                 pl.BlockSpec((chunk, vw), lambda hg, s, rf, rb, fl, sq: (rb[s], hg)),
                 pl.BlockSpec((1, 2, hb, dh, dh),
                              lambda hg, s, rf, rb, fl, sq: (sq[s], 0, hg, 0, 0))]
    return pl.pallas_call(
        functools.partial(_dn_scan_kernel, hb=hb, chunk=chunk),
        grid_spec=pltpu.PrefetchScalarGridSpec(
            num_scalar_prefetch=4,
            grid=(n_hg, n_steps),
            in_specs=in_specs,
            out_specs=out_specs,
            scratch_shapes=[pltpu.VMEM((2, hb, dh, dh), F32)],
        ),
        out_shape=[jax.ShapeDtypeStruct((m, n_vheads * dh), F32),
                   jax.ShapeDtypeStruct((m, n_vheads * dh), F32),
                   jax.ShapeDtypeStruct((n_seq_total, 2, n_vheads, dh, dh), F32)],
        compiler_params=_cp("parallel", "arbitrary"),
        name="dn_scan",
    )(*tabs, qkvc, qkvc, qkvc, qkvc, qkvc, qkvc, ba, ba, nega, dtb, s0)


def _dn_gate_kernel(of_ref, ob_ref, z_ref, w_ref, o_ref, *, n_heads):
    w = w_ref[...]
    for h in range(n_heads):
        sl = slice(h * DN_HEAD_DIM, (h + 1) * DN_HEAD_DIM)
        o = of_ref[:, sl] + ob_ref[:, sl]
        y = o * lax.rsqrt(jnp.mean(o * o, axis=-1, keepdims=True) + EPS) * w
        o_ref[:, sl] = (y * _silu(z_ref[:, sl])).astype(o_ref.dtype)


def dn_gate(o_f, o_b, proj, out_norm, z_col0, tm=256, tn=1024):
    m, n = o_f.shape
    zb0 = z_col0 // tn
    return pl.pallas_call(
        functools.partial(_dn_gate_kernel, n_heads=tn // DN_HEAD_DIM),
        grid=(m // tm, n // tn),
        in_specs=[pl.BlockSpec((tm, tn), lambda i, j: (i, j)),
                  pl.BlockSpec((tm, tn), lambda i, j: (i, j)),
                  pl.BlockSpec((tm, tn), lambda i, j: (i, zb0 + j)),
                  pl.BlockSpec((1, DN_HEAD_DIM), lambda i, j: (0, 0))],
        out_specs=pl.BlockSpec((tm, tn), lambda i, j: (i, j)),
        out_shape=jax.ShapeDtypeStruct((m, n), BF16),
        compiler_params=_cp("parallel", "parallel"),
        name="dn_gate",
    )(o_f, o_b, proj, out_norm.reshape(1, DN_HEAD_DIM))


def _router_kernel(x_ref, whi_ref, wlo_ref, g_ref, s_ref, *, n_experts):
    x = x_ref[...]
    xhi = x.astype(BF16)
    xlo = (x - xhi.astype(F32)).astype(BF16)
    lg = (jnp.dot(xhi, whi_ref[...], preferred_element_type=F32)
          + jnp.dot(xlo, whi_ref[...], preferred_element_type=F32)
          + jnp.dot(xhi, wlo_ref[...], preferred_element_type=F32))
    lane = lax.broadcasted_iota(jnp.int32, lg.shape, 1).astype(F32)
    neg = -jnp.inf
    lg = jnp.where(lane < n_experts, lg, neg)
    m1 = jnp.max(lg, axis=-1, keepdims=True)
    i1 = jnp.min(jnp.where(lg == m1, lane, float(LANES)), axis=-1, keepdims=True)
    mk1 = lane == i1
    lg2 = jnp.where(mk1, neg, lg)
    m2 = jnp.max(lg2, axis=-1, keepdims=True)
    i2 = jnp.min(jnp.where(lg2 == m2, lane, float(LANES)), axis=-1, keepdims=True)
    mk2 = lane == i2
    e = jnp.exp(m2 - m1)
    w1 = 1.0 / (1.0 + e)
    g_ref[...] = jnp.where(mk1, w1, 0.0) + jnp.where(mk2, e * w1, 0.0)
    s_ref[...] = jnp.where(mk1, 1.0, 0.0) + jnp.where(mk2, 1.0, 0.0)


def router(x, w_router, tm=512):
    m, d = x.shape
    e = w_router.shape[1]
    wp = jnp.zeros((d, LANES), F32).at[:, :e].set(w_router)
    whi = wp.astype(BF16)
    wlo = (wp - whi.astype(F32)).astype(BF16)
    return pl.pallas_call(
        functools.partial(_router_kernel, n_experts=e),
        grid=(m // tm,),
        in_specs=[pl.BlockSpec((tm, d), lambda i: (i, 0)),
                  pl.BlockSpec((d, LANES), lambda i: (0, 0)),
                  pl.BlockSpec((d, LANES), lambda i: (0, 0))],
        out_specs=[pl.BlockSpec((tm, LANES), lambda i: (i, 0))] * 2,
        out_shape=[jax.ShapeDtypeStruct((m, LANES), F32)] * 2,
        compiler_params=_cp("parallel"),
        name="router",
    )(x, whi, wlo)


def _row_copy(src_hbm, row, dst_ref, r, sem):
    return pltpu.make_async_copy(src_hbm.at[pl.ds(row, 1), :], dst_ref.at[pl.ds(r, 1), :], sem)


def _gather_kernel(idx_ref, x_hbm, o_ref, sem, *, tg):
    def start(r, carry):
        _row_copy(x_hbm, idx_ref[0, 0, r], o_ref, r, sem).start()
        return carry

    def wait(r, carry):
        _row_copy(x_hbm, idx_ref[0, 0, r], o_ref, r, sem).wait()
        return carry

    lax.fori_loop(0, tg, start, 0)
    lax.fori_loop(0, tg, wait, 0)


def gather_rows(x, idx, tg=256):
    ms = idx.shape[0]
    d = x.shape[1]
    return pl.pallas_call(
        functools.partial(_gather_kernel, tg=tg),
        grid=(ms // tg,),
        in_specs=[pl.BlockSpec((1, 1, tg), lambda i: (i, 0, 0), memory_space=pltpu.SMEM),
                  pl.BlockSpec(memory_space=pl.ANY)],
        out_specs=pl.BlockSpec((tg, d), lambda i: (i, 0)),
        out_shape=jax.ShapeDtypeStruct((ms, d), x.dtype),
        scratch_shapes=[pltpu.SemaphoreType.DMA(())],
        compiler_params=_cp("arbitrary"),
        name="gather_rows",
    )(idx.reshape(ms // tg, 1, tg), x)


def _combine_kernel(pos_ref, ys_hbm, g_ref, res_ref, gm_ref, o_ref, buf, sem, *, tc, n_sel):
    def start(r, carry):
        for j in range(n_sel):
            _row_copy(ys_hbm, pos_ref[0, 0, r * n_sel + j], buf.at[j], r, sem).start()
        return carry

    def wait(r, carry):
        for j in range(n_sel):
            _row_copy(ys_hbm, pos_ref[0, 0, r * n_sel + j], buf.at[j], r, sem).wait()
        return carry

    lax.fori_loop(0, tc, start, 0)
    lax.fori_loop(0, tc, wait, 0)
    acc = g_ref[:, 0:1] * buf[0]
    for j in range(1, n_sel):
        acc = acc + g_ref[:, j:j + 1] * buf[j]
    o_ref[...] = res_ref[...] + gm_ref[0] * acc


def moe_combine(ys, pos, gates, res, modtab, gate_row, n_ctx, t_lat, tc=128):
    m, d = res.shape
    n_sel = pos.shape[1]
    return pl.pallas_call(
        functools.partial(_combine_kernel, tc=tc, n_sel=n_sel),
        grid=(m // tc,),
        in_specs=[pl.BlockSpec((1, 1, tc * n_sel), lambda i: (i, 0, 0), memory_space=pltpu.SMEM),
                  pl.BlockSpec(memory_space=pl.ANY),
                  pl.BlockSpec((tc, n_sel), lambda i: (i, 0)),
                  pl.BlockSpec((tc, d), lambda i: (i, 0)),
                  pl.BlockSpec((1, 1, d),
                               lambda i: (_group_of_tile(i, tc, n_ctx, t_lat) * 6 + gate_row, 0, 0))],
        out_specs=pl.BlockSpec((tc, d), lambda i: (i, 0)),
        out_shape=jax.ShapeDtypeStruct((m, d), F32),
        scratch_shapes=[pltpu.VMEM((n_sel, tc, d), F32), pltpu.SemaphoreType.DMA(())],
        compiler_params=_cp("arbitrary"),
        name="moe_combine",
    )(pos.reshape(m // tc, 1, tc * n_sel), ys, gates, res, modtab)


def routing_tables(gates, n_experts, tm):
    m = gates.shape[0]
    g = gates[:, :n_experts]
    pairs = jnp.ones((m, n_experts), bool)
    pi = pairs.astype(jnp.int32)
    cnt = jnp.sum(pi, axis=0)
    padded = ((cnt + tm - 1) // tm) * tm
    ends = jnp.cumsum(padded)
    starts = ends - padded
    rank = jnp.cumsum(pi, axis=0) - pi
    pos = starts[None, :] + rank
    ms = n_experts * m
    tok = jnp.broadcast_to(jnp.arange(m, dtype=jnp.int32)[:, None], pos.shape)
    tok_sorted = jnp.zeros((ms,), jnp.int32).at[jnp.where(pairs, pos, ms).reshape(-1)].set(
        tok.reshape(-1), mode="drop")
    tile_start = jnp.arange(ms // tm, dtype=jnp.int32) * tm
    tile_expert = jnp.minimum(
        jnp.sum(tile_start[:, None] >= ends[None, :], axis=1), n_experts - 1).astype(jnp.int32)
    n_valid = (ends[-1] // tm).astype(jnp.int32).reshape(1)
    return tok_sorted, tile_expert, n_valid, pos.astype(jnp.int32), g


def moe_layer(h, x_res, w_router, wg, wu, wd, modtab, gate_row, n_ctx, t_lat, tm=512):
    n_experts = wg.shape[0]
    f = wg.shape[2]
    gates, _ = router(h, w_router)
    tok_sorted, te, nv, pos_sel, g_sel = routing_tables(gates, n_experts, tm)
    xs = gather_rows(h, tok_sorted)
    act = mm_swiglu(xs, wg, wu, te, nv, tm=tm, tn=1024)
    ys = mm_k(act, wd, te, nv, tm=tm, tn=min(1024, wd.shape[2]), tk=f // 2)
    return moe_combine(ys, pos_sel, g_sel, x_res, modtab, gate_row, n_ctx, t_lat)


def _dn_ba_layout(hb):
    n_hg = DN_V_HEADS // hb
    idx = np.full((n_hg, LANES), -1, np.int64)
    for hg in range(n_hg):
        for d in range(2):
            for ab in range(2):
                for hh in range(hb):
                    idx[hg, d * 2 * hb + ab * hb + hh] = d * 2 * DN_V_HEADS + ab * DN_V_HEADS + hg * hb + hh
    return idx.reshape(-1)


def _permute_cols(a, idx):
    valid = jnp.asarray(idx >= 0)
    return jnp.where(valid, jnp.take(a, jnp.asarray(np.maximum(idx, 0)), axis=-1), 0.0)


def kernel(x_prompt, x_sample, state_dn, cache_k, cache_v, c, c_ctx, w_mod, b_mod, norm_mix, norm_ffn, norm_final, dn_w_in, dn_conv, dn_A_log, dn_dt_bias, dn_out_norm, dn_w_out, att_w_in, att_q_norm, att_k_norm, att_w_out, ffn_w_gate, ffn_w_up, ffn_w_down, moe_router, moe_w_gate, moe_w_up, moe_w_down):
    bc, tc_, d = x_prompt.shape
    bx, tx, _ = x_sample.shape
    depth = w_mod.shape[0]
    n_ctx = bc * tc_
    m = n_ctx + bx * tx
    past = cache_k.shape[2]
    qk_dim = DN_K_HEADS * DN_HEAD_DIM
    v_dim = DN_V_HEADS * DN_HEAD_DIM
    conv_dim = 2 * qk_dim + v_dim
    kv_dim = ATT_KV_HEADS * ATT_HEAD_DIM
    q_dim = ATT_Q_HEADS * ATT_HEAD_DIM

    x = jnp.concatenate([x_prompt.reshape(n_ctx, d), x_sample.reshape(bx * tx, d)], axis=0)
    n_groups = 16
    cvec = jnp.zeros((n_groups, d), F32).at[0].set(c_ctx).at[1:1 + bx].set(c)
    mods = adaln_all(cvec, w_mod, b_mod)
    cos, sin = rope_tables(tx // GRID_W, ATT_HEAD_DIM)
    ba_idx = _dn_ba_layout(DN_HEADS_PER_STEP)
    n_hg = DN_V_HEADS // DN_HEADS_PER_STEP

    new_dn, new_k, new_v = [], [], []
    for i in range(depth):
        j = i // 2
        modtab = mods[i].reshape(n_groups * 6, 1, d)
        h = norm_mod(x, norm_mix[i], modtab, (0, 1), n_ctx, tx, BF16)
        if i % 2 == 0:
            w_in = dn_w_in[j]
            proj = mm(h, w_in[:, :conv_dim + v_dim].astype(BF16), F32)
            ba = mm(h, _permute_cols(w_in[:, conv_dim + v_dim:], ba_idx).astype(BF16), F32)
            old = jnp.zeros((2, 2, DN_V_HEADS), F32)
            nega_old = old.at[:, 1].set(-jnp.exp(dn_A_log[j].astype(F32))).reshape(-1)
            dtb_old = old.at[:, 1].set(dn_dt_bias[j].astype(F32)).reshape(-1)
            nega = _permute_cols(nega_old, ba_idx).reshape(n_hg, 1, LANES)
            dtb = _permute_cols(dtb_old, ba_idx).reshape(n_hg, 1, LANES)
            qkvc = dn_short_conv(proj, dn_conv[j], n_ctx=n_ctx, l_ctx=tc_, l_lat=tx,
                                 qk_dim=qk_dim, conv_dim=conv_dim, dk=DN_HEAD_DIM)
            o_f, o_b, s_out = dn_scan(qkvc, ba, nega, dtb, state_dn[:, j].astype(F32),
                                      [(bc, tc_, False), (bx, tx, True)],
                                      qk_dim=qk_dim, n_vheads=DN_V_HEADS)
            new_dn.append(s_out[:bc])
            og = dn_gate(o_f, o_b, proj, dn_out_norm[j], conv_dim)
            te, nv = _one_expert_tables(m, 512)
            x = mm_k(og, dn_w_out[j].astype(BF16)[None], te, nv, 512, 1024, v_dim,
                     res=x, modtab=modtab, gate_row=2, n_ctx=n_ctx, t_lat=tx)
        else:
            qkv = mm(h, att_w_in[j].astype(BF16), F32)
            q, k, v, kf = attn_prep(qkv, cos, sin, att_q_norm[j], att_k_norm[j], n_ctx, tx,
                                    ATT_Q_HEADS, ATT_KV_HEADS, ATT_HEAD_DIM)
            o_ctx = attention(q, k, v, None, batch=bc, seq=tc_, row0=0,
                              hq=ATT_Q_HEADS, hkv=ATT_KV_HEADS, dh=ATT_HEAD_DIM, tq=tc_)
            ck = cache_k[:, j].reshape(bx * past, kv_dim).astype(BF16)
            cv = cache_v[:, j].reshape(bx * past, kv_dim).astype(BF16)
            o_lat = attention(q, k, v, (ck, cv), batch=bx, seq=tx, row0=n_ctx,
                              hq=ATT_Q_HEADS, hkv=ATT_KV_HEADS, dh=ATT_HEAD_DIM, tq=256)
            o = jnp.concatenate([o_ctx, o_lat], axis=0)
            new_k.append(kf[:n_ctx].reshape(bc, tc_, ATT_KV_HEADS, ATT_HEAD_DIM))
            new_v.append(qkv[:n_ctx, q_dim + kv_dim:].reshape(bc, tc_, ATT_KV_HEADS, ATT_HEAD_DIM))
            te, nv = _one_expert_tables(m, 512)
            x = mm_k(o, att_w_out[j].astype(BF16)[None], te, nv, 512, 1024, q_dim,
                     res=x, modtab=modtab, gate_row=2, n_ctx=n_ctx, t_lat=tx)
        if i % 2 == 0:
            h = norm_mod(x, norm_ffn[i], modtab, (3, 4), n_ctx, tx, BF16)
            f = ffn_w_gate.shape[2]
            te, nv = _one_expert_tables(m, 1024)
            act = mm_swiglu(h, ffn_w_gate[j].astype(BF16)[None], ffn_w_up[j].astype(BF16)[None],
                            te, nv, tm=1024, tn=512)
            te, nv = _one_expert_tables(m, 512)
            x = mm_k(act, ffn_w_down[j].astype(BF16)[None], te, nv, 512, 1024, f // 2,
                     res=x, modtab=modtab, gate_row=5, n_ctx=n_ctx, t_lat=tx)
        else:
            h = norm_mod(x, norm_ffn[i], modtab, (3, 4), n_ctx, tx, F32)
            x = moe_layer(h, x, moe_router[j], moe_w_gate[j].astype(BF16),
                          moe_w_up[j].astype(BF16), moe_w_down[j].astype(BF16),
                          modtab, 5, n_ctx, tx)
    y = norm_mod(x, norm_final, None, (), n_ctx, tx, F32)
    y_prompt = y[:n_ctx].reshape(bc, tc_, d)
    y_sample = y[n_ctx:].reshape(bx, tx, d)
    return (y_prompt, y_sample, jnp.stack(new_dn, axis=1),
            jnp.stack(new_k, axis=1), jnp.stack(new_v, axis=1))
```

```python
import functools
import math

import numpy as np
import jax
import jax.numpy as jnp
from jax import lax
from jax.experimental import pallas as pl
from jax.experimental.pallas import tpu as pltpu

F32 = jnp.float32
BF16 = jnp.bfloat16
EPS = 1e-6

GRID_W = 64
ROPE_THETA = 10000.0
DN_K_HEADS = 16
DN_V_HEADS = 32
DN_HEAD_DIM = 128
DN_CHUNK = 64
ATT_Q_HEADS = 16
ATT_KV_HEADS = 2
ATT_HEAD_DIM = 256
N_EXPERTS = 8
TOP_K = 2

LANES = 128
VMEM_LIMIT_BYTES = 56 * 1024 * 1024

DN_HEADS_PER_STEP = 8


def _cp(*sem):
    return pltpu.CompilerParams(dimension_semantics=sem, vmem_limit_bytes=VMEM_LIMIT_BYTES)


def _silu(x):
    return x * jax.nn.sigmoid(x)


def _group_of_tile(i, tm, n_ctx, t_lat):
    row = i * tm
    return jnp.where(row < n_ctx, 0, 1 + (row - n_ctx) // t_lat)


def _adaln_kernel(c_ref, w_ref, b_ref, o_ref):
    s = _silu(c_ref[...]).astype(BF16)
    o_ref[0] = jnp.dot(s, w_ref[0].astype(BF16), preferred_element_type=F32) + b_ref[0]


def adaln_all(cvec, w_mod, b_mod, tn=1024):
    n_layers, d, n = w_mod.shape
    g = cvec.shape[0]
    return pl.pallas_call(
        _adaln_kernel,
        grid=(n_layers, n // tn),
        in_specs=[
            pl.BlockSpec((g, d), lambda l, j: (0, 0)),
            pl.BlockSpec((1, d, tn), lambda l, j: (l, 0, j)),
            pl.BlockSpec((1, 1, tn), lambda l, j: (l, 0, j)),
        ],
        out_specs=pl.BlockSpec((1, g, tn), lambda l, j: (l, 0, j)),
        out_shape=jax.ShapeDtypeStruct((n_layers, g, n), F32),
        compiler_params=_cp("parallel", "parallel"),
        name="adaln",
    )(cvec, w_mod, b_mod.reshape(n_layers, 1, n))


def _norm_mod_kernel(x_ref, w_ref, *rest, modulated):
    x = x_ref[...]
    y = x * lax.rsqrt(jnp.mean(x * x, axis=-1, keepdims=True) + EPS) * w_ref[...]
    if modulated:
        sh_ref, sc_ref, o_ref = rest
        y = y * (1.0 + sc_ref[0]) + sh_ref[0]
    else:
        (o_ref,) = rest
    o_ref[...] = y.astype(o_ref.dtype)


def norm_mod(x, w, modtab, rows, n_ctx, t_lat, out_dtype, tm=512):
    m, d = x.shape
    in_specs = [pl.BlockSpec((tm, d), lambda i: (i, 0)), pl.BlockSpec((1, d), lambda i: (0, 0))]
    args = [x, w.reshape(1, d)]
    if modtab is not None:
        for r in rows:
            in_specs.append(pl.BlockSpec(
                (1, 1, d), lambda i, r=r: (_group_of_tile(i, tm, n_ctx, t_lat) * 6 + r, 0, 0)))
            args.append(modtab)
    return pl.pallas_call(
        functools.partial(_norm_mod_kernel, modulated=modtab is not None),
        grid=(m // tm,),
        in_specs=in_specs,
        out_specs=pl.BlockSpec((tm, d), lambda i: (i, 0)),
        out_shape=jax.ShapeDtypeStruct((m, d), out_dtype),
        compiler_params=_cp("parallel"),
        name="norm_mod",
    )(*args)


def _mm_kernel(x_ref, w_ref, o_ref):
    o_ref[...] = jnp.dot(x_ref[...], w_ref[...], preferred_element_type=F32).astype(o_ref.dtype)


def mm(x, w, out_dtype, tm=1024, tn=1024):
    m, k = x.shape
    n = w.shape[1]
    tm, tn = min(tm, m), min(tn, n)
    return pl.pallas_call(
        _mm_kernel,
        grid=(m // tm, n // tn),
        in_specs=[pl.BlockSpec((tm, k), lambda i, j: (i, 0)),
                  pl.BlockSpec((k, tn), lambda i, j: (0, j))],
        out_specs=pl.BlockSpec((tm, tn), lambda i, j: (i, j)),
        out_shape=jax.ShapeDtypeStruct((m, n), out_dtype),
        compiler_params=_cp("parallel", "parallel"),
        name="mm",
    )(x, w)


def _swiglu_kernel(te_ref, nv_ref, x_ref, wg_ref, wu_ref, o_ref):
    i = pl.program_id(0)

    @pl.when(i < nv_ref[0])
    def _():
        x = x_ref[...].astype(BF16)
        g = jnp.dot(x, wg_ref[0], preferred_element_type=F32)
        u = jnp.dot(x, wu_ref[0], preferred_element_type=F32)
        o_ref[...] = (_silu(g) * u).astype(o_ref.dtype)

    @pl.when(i >= nv_ref[0])
    def _():
        o_ref[...] = jnp.zeros_like(o_ref)


def mm_swiglu(x, wg, wu, tile_expert, n_valid, tm, tn):
    m, k = x.shape
    f = wg.shape[2]

    def xmap(i, j, te, nv):
        return (jnp.minimum(i, nv[0] - 1), 0)

    def wmap(i, j, te, nv):
        return (te[jnp.minimum(i, nv[0] - 1)], 0, j)

    return pl.pallas_call(
        _swiglu_kernel,
        grid_spec=pltpu.PrefetchScalarGridSpec(
            num_scalar_prefetch=2,
            grid=(m // tm, f // tn),
            in_specs=[pl.BlockSpec((tm, k), xmap),
                      pl.BlockSpec((1, k, tn), wmap),
                      pl.BlockSpec((1, k, tn), wmap)],
            out_specs=pl.BlockSpec((tm, tn), lambda i, j, te, nv: (i, j)),
        ),
        out_shape=jax.ShapeDtypeStruct((m, f), BF16),
        compiler_params=_cp("parallel", "arbitrary"),
        name="mm_swiglu",
    )(tile_expert, n_valid, x, wg, wu)


def _mmk_kernel(te_ref, nv_ref, x_ref, w_ref, *rest, nk, has_res):
    if has_res:
        res_ref, g_ref, o_ref = rest
    else:
        (o_ref,) = rest
    i = pl.program_id(0)
    k = pl.program_id(2)

    def finish(acc):
        if has_res:
            return res_ref[...] + g_ref[0] * acc
        return acc

    @pl.when(i < nv_ref[0])
    def _():
        part = jnp.dot(x_ref[...].astype(BF16), w_ref[0], preferred_element_type=F32)
        if nk == 1:
            o_ref[...] = finish(part)
        else:
            @pl.when(k == 0)
            def _():
                o_ref[...] = part

            if nk > 2:
                @pl.when((k > 0) & (k < nk - 1))
                def _():
                    o_ref[...] += part

            @pl.when(k == nk - 1)
            def _():
                o_ref[...] = finish(o_ref[...] + part)

    @pl.when(i >= nv_ref[0])
    def _():
        o_ref[...] = jnp.zeros_like(o_ref)


def mm_k(x, w, tile_expert, n_valid, tm, tn, tk, res=None, modtab=None, gate_row=None,
         n_ctx=0, t_lat=1):
    m, k = x.shape
    n = w.shape[2]
    nk = k // tk

    def xmap(i, j, kk, te, nv):
        return (jnp.minimum(i, nv[0] - 1), kk)

    def wmap(i, j, kk, te, nv):
        return (te[jnp.minimum(i, nv[0] - 1)], kk, j)

    in_specs = [pl.BlockSpec((tm, tk), xmap), pl.BlockSpec((1, tk, tn), wmap)]
    args = [x, w]
    if res is not None:
        in_specs.append(pl.BlockSpec((tm, tn), lambda i, j, kk, te, nv: (i, j)))
        in_specs.append(pl.BlockSpec(
            (1, 1, tn),
            lambda i, j, kk, te, nv: (_group_of_tile(i, tm, n_ctx, t_lat) * 6 + gate_row, 0, j)))
        args += [res, modtab]
    return pl.pallas_call(
        functools.partial(_mmk_kernel, nk=nk, has_res=res is not None),
        grid_spec=pltpu.PrefetchScalarGridSpec(
            num_scalar_prefetch=2,
            grid=(m // tm, n // tn, nk),
            in_specs=in_specs,
            out_specs=pl.BlockSpec((tm, tn), lambda i, j, kk, te, nv: (i, j)),
        ),
        out_shape=jax.ShapeDtypeStruct((m, n), F32),
        compiler_params=_cp("parallel", "parallel", "arbitrary"),
        name="mm_k",
    )(tile_expert, n_valid, *args)


def _one_expert_tables(m, tm):
    return jnp.zeros((m // tm,), jnp.int32), jnp.full((1,), m // tm, jnp.int32)


def rope_tables(rows, head_dim):
    n_freq = head_dim // 4
    inv = ROPE_THETA ** (-jnp.arange(n_freq, dtype=F32) / n_freq)
    r = jnp.repeat(jnp.arange(rows, dtype=F32), GRID_W)
    cl = jnp.tile(jnp.arange(GRID_W, dtype=F32), rows)
    ang = jnp.concatenate([r[:, None] * inv, cl[:, None] * inv], axis=-1)
    return jnp.cos(ang), jnp.sin(ang)


def _attn_prep_kernel(x_ref, cos_ref, sin_ref, qn_ref, kn_ref, q_ref, k_ref, v_ref, kf_ref,
                      *, hq, hkv, dh, n_ctx, tm):
    is_lat = pl.program_id(0) * tm >= n_ctx
    c = jnp.where(is_lat, cos_ref[...], 1.0)
    s = jnp.where(is_lat, sin_ref[...], 0.0)
    half = dh // 2
    for h in range(hq + hkv):
        xh = x_ref[:, h * dh:(h + 1) * dh]
        w = qn_ref[...] if h < hq else kn_ref[...]
        y = xh * lax.rsqrt(jnp.mean(xh * xh, axis=-1, keepdims=True) + EPS) * w
        y1, y2 = y[:, :half], y[:, half:]
        o1 = y1 * c - y2 * s
        o2 = y2 * c + y1 * s
        if h < hq:
            scale = dh ** -0.5
            q_ref[:, h * dh:h * dh + half] = (o1 * scale).astype(BF16)
            q_ref[:, h * dh + half:(h + 1) * dh] = (o2 * scale).astype(BF16)
        else:
            b = (h - hq) * dh
            k_ref[:, b:b + half] = o1.astype(BF16)
            k_ref[:, b + half:b + dh] = o2.astype(BF16)
            kf_ref[:, b:b + half] = o1
            kf_ref[:, b + half:b + dh] = o2
    v_ref[...] = x_ref[:, (hq + hkv) * dh:].astype(BF16)


def attn_prep(qkv, cos, sin, q_norm, k_norm, n_ctx, t_lat, hq, hkv, dh, tm=256):
    m = qkv.shape[0]
    half = dh // 2

    def posmap(i):
        row = i * tm
        return (jnp.where(row >= n_ctx, ((row - n_ctx) % t_lat) // tm, 0), 0)

    return pl.pallas_call(
        functools.partial(_attn_prep_kernel, hq=hq, hkv=hkv, dh=dh, n_ctx=n_ctx, tm=tm),
        grid=(m // tm,),
        in_specs=[pl.BlockSpec((tm, (hq + 2 * hkv) * dh), lambda i: (i, 0)),
                  pl.BlockSpec((tm, half), posmap),
                  pl.BlockSpec((tm, half), posmap),
                  pl.BlockSpec((1, dh), lambda i: (0, 0)),
                  pl.BlockSpec((1, dh), lambda i: (0, 0))],
        out_specs=[pl.BlockSpec((tm, hq * dh), lambda i: (i, 0)),
                   pl.BlockSpec((tm, hkv * dh), lambda i: (i, 0)),
                   pl.BlockSpec((tm, hkv * dh), lambda i: (i, 0)),
                   pl.BlockSpec((tm, hkv * dh), lambda i: (i, 0))],
        out_shape=[jax.ShapeDtypeStruct((m, hq * dh), BF16),
                   jax.ShapeDtypeStruct((m, hkv * dh), BF16),
                   jax.ShapeDtypeStruct((m, hkv * dh), BF16),
                   jax.ShapeDtypeStruct((m, hkv * dh), F32)],
        compiler_params=_cp("parallel"),
        name="attn_prep",
    )(qkv, cos, sin, q_norm.reshape(1, dh), k_norm.reshape(1, dh))


def _attn_kernel(*refs, n_src, n_group, dh):
    q_ref = refs[0]
    kv = refs[1:1 + 2 * n_src]
    o_ref = refs[-1]
    nt = (((1,), (1,)), ((), ()))
    for g in range(n_group):
        qg = q_ref[:, g * dh:(g + 1) * dh]
        ss = [lax.dot_general(qg, kv[2 * i][...], nt, preferred_element_type=F32)
              for i in range(n_src)]
        mx = functools.reduce(jnp.maximum, [jnp.max(s, axis=-1, keepdims=True) for s in ss])
        ps = [jnp.exp(s - mx) for s in ss]
        den = functools.reduce(jnp.add, [jnp.sum(p, axis=-1, keepdims=True) for p in ps])
        o = functools.reduce(jnp.add, [
            jnp.dot(p.astype(BF16), kv[2 * i + 1][...], preferred_element_type=F32)
            for i, p in enumerate(ps)])
        o_ref[:, g * dh:(g + 1) * dh] = (o / den).astype(o_ref.dtype)


def attention(q, k, v, cache, *, batch, seq, row0, hq, hkv, dh, tq):
    n_group = hq // hkv
    nq = seq // tq
    qb0, kb0 = row0 // tq, row0 // seq
    in_specs = [pl.BlockSpec((tq, n_group * dh), lambda b, h, t: (qb0 + b * nq + t, h))]
    args = [q]
    if cache is not None:
        p_len = cache[0].shape[0] // batch
        in_specs += [pl.BlockSpec((p_len, dh), lambda b, h, t: (b, h))] * 2
        args += list(cache)
    in_specs += [pl.BlockSpec((seq, dh), lambda b, h, t: (kb0 + b, h))] * 2
    args += [k, v]
    n_src = (len(args) - 1) // 2
    return pl.pallas_call(
        functools.partial(_attn_kernel, n_src=n_src, n_group=n_group, dh=dh),
        grid=(batch, hkv, nq),
        in_specs=in_specs,
        out_specs=pl.BlockSpec((tq, n_group * dh), lambda b, h, t: (b * nq + t, h)),
        out_shape=jax.ShapeDtypeStruct((batch * seq, hq * dh), BF16),
        compiler_params=_cp("parallel", "parallel", "arbitrary"),
        name="attention",
    )(*args)


def _dn_conv_kernel(x_ref, w_ref, o_ref, pad_ref, *, t, tc, n_ctx_blocks, l_ctx, l_lat,
                    n_q_tiles, n_qk_tiles, dk):
    j = pl.program_id(1)
    zeros8 = jnp.zeros((8, tc), F32)
    pad_ref[0:8, :] = zeros8
    pad_ref[t + 8:t + 16, :] = zeros8
    pad_ref[8:t + 8, :] = x_ref[...]
    w = w_ref[...]
    seq_len = jnp.where(pl.program_id(0) < n_ctx_blocks, l_ctx, l_lat)
    pos = lax.broadcasted_iota(jnp.int32, (t, tc), 0) & (seq_len - 1)
    y = (w[0:1] * jnp.where(pos >= 1, pad_ref[7:t + 7, :], 0.0)
         + w[1:2] * pad_ref[8:t + 8, :]
         + w[2:3] * jnp.where(pos <= seq_len - 2, pad_ref[9:t + 9, :], 0.0)
         + w[3:4] * jnp.where(pos <= seq_len - 3, pad_ref[10:t + 10, :], 0.0))
    y = _silu(y)
    is_qk = j < n_qk_tiles
    q_scale = jnp.where(j < n_q_tiles, dk ** -0.5, 1.0)
    for h in range(tc // LANES):
        yh = y[:, h * LANES:(h + 1) * LANES]
        ss = jnp.sum(yh * yh, axis=-1, keepdims=True)
        inv = jnp.where(is_qk, lax.rsqrt(ss + EPS) * q_scale, 1.0)
        o_ref[:, h * LANES:(h + 1) * LANES] = (yh * inv).astype(o_ref.dtype)


def dn_short_conv(proj, conv_w, *, n_ctx, l_ctx, l_lat, qk_dim, conv_dim, dk, tc=512):
    m = proj.shape[0]
    assert l_lat % l_ctx == 0 and n_ctx % l_lat == 0 and m % l_lat == 0
    assert l_ctx & (l_ctx - 1) == 0 and l_lat & (l_lat - 1) == 0 and l_ctx >= 4
    return pl.pallas_call(
        functools.partial(_dn_conv_kernel, t=l_lat, tc=tc, n_ctx_blocks=n_ctx // l_lat,
                          l_ctx=l_ctx, l_lat=l_lat, n_q_tiles=qk_dim // tc,
                          n_qk_tiles=2 * qk_dim // tc, dk=dk),
        grid=(m // l_lat, conv_dim // tc),
        in_specs=[pl.BlockSpec((l_lat, tc), lambda b, j: (b, j)),
                  pl.BlockSpec((conv_w.shape[0], tc), lambda b, j: (0, j))],
        out_specs=pl.BlockSpec((l_lat, tc), lambda b, j: (b, j)),
        out_shape=jax.ShapeDtypeStruct((m, conv_dim), BF16),
        scratch_shapes=[pltpu.VMEM((l_lat + 16, tc), F32)],
        compiler_params=_cp("parallel", "parallel"),
        name="dn_conv",
    )(proj, conv_w)


def _softplus(x):
    return jnp.maximum(x, 0.0) + jnp.log1p(jnp.exp(-jnp.abs(x)))


def _split_bf16(a):
    hi = a.astype(BF16)
    return hi, (a - hi.astype(F32)).astype(BF16)


def _mm_3pass(a, b):
    ah, al = a
    bh, bl = b
    n = ah.shape[0]
    both = jnp.dot(jnp.concatenate([ah, al], axis=0), bh, preferred_element_type=F32)
    return both[:n] + both[n:] + jnp.dot(ah, bl, preferred_element_type=F32)


def _pair_masks(n):
    r = lax.broadcasted_iota(jnp.int32, (n, n), 0)
    c = lax.broadcasted_iota(jnp.int32, (n, n), 1)
    return [((r >> (k + 1)) == (c >> (k + 1))) & ((r >> k) != (c >> k))
            for k in range(int(math.log2(n)))]


def _unit_tri_inverses(lows, eye, masks):
    ds = [eye - jnp.where(masks[0], low, 0.0) for low in lows]
    for mask in masks[1:]:
        dsp = [_split_bf16(d) for d in ds]
        ts = [_mm_3pass(d, _split_bf16(jnp.where(mask, low, 0.0))) for d, low in zip(dsp, lows)]
        ds = [d - _mm_3pass(_split_bf16(t), dp) for d, t, dp in zip(ds, ts, dsp)]
    return ds


def _dn_scan_kernel(rowf_ref, rowb_ref, flag_ref, seq_ref,
                    qf_ref, kf_ref, vf_ref, qb_ref, kb_ref, vb_ref, baf_ref, bab_ref,
                    nega_ref, dtb_ref, s0_ref, of_ref, ob_ref, sout_ref, s_scr, *, hb, chunk):
    step = pl.program_id(1)
    flag = flag_ref[step]
    first = (flag & 1) == 1
    last = (flag & 2) == 2
    is_lat = (flag & 4) == 4
    dh = DN_HEAD_DIM
    nt = (((1,), (1,)), ((), ()))

    @pl.when(first)
    def _():
        s_scr[...] = jnp.where(is_lat, s0_ref[0], 0.0)

    r = lax.broadcasted_iota(jnp.int32, (chunk, chunk), 0)
    c = lax.broadcasted_iota(jnp.int32, (chunk, chunk), 1)
    eye = (r == c).astype(F32)
    masks = _pair_masks(chunk)

    heads = []
    lows = []
    for d, q_ref, k_ref, v_ref, ba_ref, o_ref in ((0, qf_ref, kf_ref, vf_ref, baf_ref, of_ref),
                                                   (1, qb_ref, kb_ref, vb_ref, bab_ref, ob_ref)):
        incl = (r >= c) if d == 0 else (r <= c)
        strict = (r > c) if d == 0 else (r < c)
        last_row = chunk - 1 if d == 0 else 0
        ba = ba_ref[...]
        sig = jax.nn.sigmoid(ba)
        g = nega_ref[0] * _softplus(ba + dtb_ref[0])
        gc = jnp.dot(incl.astype(F32), g, preferred_element_type=F32,
                     precision=lax.Precision.HIGHEST)
        gct = gc.T
        egc = jnp.exp(gc)
        glast = gc[last_row:last_row + 1, :]
        eglast = jnp.exp(glast)
        ekg = jnp.exp(glast - gc)
        for kh in range(hb // 2):
            qh = q_ref[:, kh * dh:(kh + 1) * dh]
            kk_ = k_ref[:, kh * dh:(kh + 1) * dh]
            kf32 = kk_.astype(F32)
            qf32 = qh.astype(F32)
            kkt = lax.dot_general(kk_, kk_, nt, preferred_element_type=F32)
            qkt = lax.dot_general(qh, kk_, nt, preferred_element_type=F32)
            for rr in range(2):
                hh = kh * 2 + rr
                cb = d * 2 * hb + hh
                ca = d * 2 * hb + hb + hh
                bcol = sig[:, cb:cb + 1]
                ecol = egc[:, ca:ca + 1]
                decay = jnp.exp(jnp.where(incl, gc[:, ca:ca + 1] - gct[ca:ca + 1, :], -jnp.inf))
                lows.append(jnp.where(strict, bcol * kkt * decay, 0.0))
                heads.append(dict(
                    d=d, hh=hh, o_ref=o_ref,
                    vbm=(v_ref[:, hh * dh:(hh + 1) * dh].astype(F32) * bcol).astype(BF16),
                    kbg=(kf32 * (bcol * ecol)).astype(BF16),
                    a=jnp.where(incl, qkt * decay, 0.0).astype(BF16),
                    qg=(qf32 * ecol).astype(BF16),
                    kgt=(kf32 * ekg[:, ca:ca + 1]).T.astype(BF16),
                    egl=eglast[:, ca:ca + 1]))

    tinvs = [t.astype(BF16) for t in _unit_tri_inverses(lows, eye, masks)]
    us = [jnp.dot(t, h["vbm"], preferred_element_type=F32) for t, h in zip(tinvs, heads)]
    ws = [jnp.dot(t, h["kbg"], preferred_element_type=F32) for t, h in zip(tinvs, heads)]
    ss = [s_scr[h["d"], h["hh"]] for h in heads]
    sbs = [s.astype(BF16) for s in ss]
    vnbs = [(u - jnp.dot(w.astype(BF16), sb, preferred_element_type=F32)).astype(BF16)
            for u, w, sb in zip(us, ws, sbs)]
    for h, sb, vnb in zip(heads, sbs, vnbs):
        hh = h["hh"]
        h["o_ref"][:, hh * dh:(hh + 1) * dh] = (
            jnp.dot(h["qg"], sb, preferred_element_type=F32)
            + jnp.dot(h["a"], vnb, preferred_element_type=F32))
    for h, s, vnb in zip(heads, ss, vnbs):
        s_scr[h["d"], h["hh"]] = s * h["egl"] + jnp.dot(h["kgt"], vnb,
                                                        preferred_element_type=F32)

    @pl.when(last)
    def _():
        sout_ref[0] = s_scr[...]


def dn_scan(qkvc, ba, nega, dtb, s0, seq_lens, *, qk_dim, n_vheads, hb=DN_HEADS_PER_STEP):
    m = qkvc.shape[0]
    chunk = DN_CHUNK
    dh = DN_HEAD_DIM
    n_hg = n_vheads // hb
    rowf, rowb, flags, seqs = [], [], [], []
    row, sid, n_zero_seq = 0, 0, 0
    for n_seq, length, uses_s0 in seq_lens:
        n_chunks = length // chunk
        for _ in range(n_seq):
            for n in range(n_chunks):
                rowf.append(row + n)
                rowb.append(row + n_chunks - 1 - n)
                flags.append((n == 0) * 1 + (n == n_chunks - 1) * 2 + (4 if uses_s0 else 0))
                seqs.append(sid)
            row += n_chunks
            sid += 1
        if not uses_s0:
            n_zero_seq += n_seq
    n_steps = len(rowf)
    n_seq_total = sid
    tabs = [jnp.asarray(np.array(t, np.int32)) for t in (rowf, rowb, flags, seqs)]
    qw, vw = (hb // 2) * dh, hb * dh
    kb0, vb0 = qk_dim // qw, 2 * qk_dim // vw

    def spec(width, col0, rows_idx):
        return pl.BlockSpec((chunk, width),
                            lambda hg, s, rf, rb, fl, sq: ((rf, rb)[rows_idx][s], col0 + hg))

    in_specs = [spec(qw, 0, 0), spec(qw, kb0, 0), spec(vw, vb0, 0),
                spec(qw, 0, 1), spec(qw, kb0, 1), spec(vw, vb0, 1),
                pl.BlockSpec((chunk, LANES), lambda hg, s, rf, rb, fl, sq: (rf[s], hg)),
                pl.BlockSpec((chunk, LANES), lambda hg, s, rf, rb, fl, sq: (rb[s], hg)),
                pl.BlockSpec((1, 1, LANES), lambda hg, s, rf, rb, fl, sq: (hg, 0, 0)),
                pl.BlockSpec((1, 1, LANES), lambda hg, s, rf, rb, fl, sq: (hg, 0, 0)),
                pl.BlockSpec((1, 2, hb, dh, dh),
                             lambda hg, s, rf, rb, fl, sq:
                             (jnp.maximum(sq[s] - n_zero_seq, 0), 0, hg, 0, 0))]
    out_specs = [pl.BlockSpec((chunk, vw), lambda hg, s, rf, rb, fl, sq: (rf[s], hg)),
                 pl.BlockSpec((chunk, vw), lambda hg, s, rf, rb, fl, sq: (rb[s], hg)),
                 pl.BlockSpec((1, 2, hb, dh, dh),
                              lambda hg, s, rf, rb, fl, sq: (sq[s], 0, hg, 0, 0))]
    return pl.pallas_call(
        functools.partial(_dn_scan_kernel, hb=hb, chunk=chunk),
        grid_spec=pltpu.PrefetchScalarGridSpec(
            num_scalar_prefetch=4,
            grid=(n_hg, n_steps),
            in_specs=in_specs,
            out_specs=out_specs,
            scratch_shapes=[pltpu.VMEM((2, hb, dh, dh), F32)],
        ),
        out_shape=[jax.ShapeDtypeStruct((m, n_vheads * dh), F32),
                   jax.ShapeDtypeStruct((m, n_vheads * dh), F32),
                   jax.ShapeDtypeStruct((n_seq_total, 2, n_vheads, dh, dh), F32)],
        compiler_params=_cp("parallel", "arbitrary"),
        name="dn_scan",
    )(*tabs, qkvc, qkvc, qkvc, qkvc, qkvc, qkvc, ba, ba, nega, dtb, s0)


def _dn_gate_kernel(of_ref, ob_ref, z_ref, w_ref, o_ref, *, n_heads):
    w = w_ref[...]
    for h in range(n_heads):
        sl = slice(h * DN_HEAD_DIM, (h + 1) * DN_HEAD_DIM)
        o = of_ref[:, sl] + ob_ref[:, sl]
        y = o * lax.rsqrt(jnp.mean(o * o, axis=-1, keepdims=True) + EPS) * w
        o_ref[:, sl] = (y * _silu(z_ref[:, sl])).astype(o_ref.dtype)


def dn_gate(o_f, o_b, proj, out_norm, z_col0, tm=256, tn=1024):
    m, n = o_f.shape
    zb0 = z_col0 // tn
    return pl.pallas_call(
        functools.partial(_dn_gate_kernel, n_heads=tn // DN_HEAD_DIM),
        grid=(m // tm, n // tn),
        in_specs=[pl.BlockSpec((tm, tn), lambda i, j: (i, j)),
                  pl.BlockSpec((tm, tn), lambda i, j: (i, j)),
                  pl.BlockSpec((tm, tn), lambda i, j: (i, zb0 + j)),
                  pl.BlockSpec((1, DN_HEAD_DIM), lambda i, j: (0, 0))],
        out_specs=pl.BlockSpec((tm, tn), lambda i, j: (i, j)),
        out_shape=jax.ShapeDtypeStruct((m, n), BF16),
        compiler_params=_cp("parallel", "parallel"),
        name="dn_gate",
    )(o_f, o_b, proj, out_norm.reshape(1, DN_HEAD_DIM))


def _router_kernel(x_ref, whi_ref, wlo_ref, g_ref, s_ref, *, n_experts):
    x = x_ref[...]
    xhi = x.astype(BF16)
    xlo = (x - xhi.astype(F32)).astype(BF16)
    lg = (jnp.dot(xhi, whi_ref[...], preferred_element_type=F32)
          + jnp.dot(xlo, whi_ref[...], preferred_element_type=F32)
          + jnp.dot(xhi, wlo_ref[...], preferred_element_type=F32))
    lane = lax.broadcasted_iota(jnp.int32, lg.shape, 1).astype(F32)
    neg = -jnp.inf
    lg = jnp.where(lane < n_experts, lg, neg)
    m1 = jnp.max(lg, axis=-1, keepdims=True)
    i1 = jnp.min(jnp.where(lg == m1, lane, float(LANES)), axis=-1, keepdims=True)
    mk1 = lane == i1
    lg2 = jnp.where(mk1, neg, lg)
    m2 = jnp.max(lg2, axis=-1, keepdims=True)
    i2 = jnp.min(jnp.where(lg2 == m2, lane, float(LANES)), axis=-1, keepdims=True)
    mk2 = lane == i2
    e = jnp.exp(m2 - m1)
    w1 = 1.0 / (1.0 + e)
    g_ref[...] = jnp.where(mk1, w1, 0.0) + jnp.where(mk2, e * w1, 0.0)
    s_ref[...] = jnp.where(mk1, 1.0, 0.0) + jnp.where(mk2, 1.0, 0.0)


def router(x, w_router, tm=512):
    m, d = x.shape
    e = w_router.shape[1]
    wp = jnp.zeros((d, LANES), F32).at[:, :e].set(w_router)
    whi = wp.astype(BF16)
    wlo = (wp - whi.astype(F32)).astype(BF16)
    return pl.pallas_call(
        functools.partial(_router_kernel, n_experts=e),
        grid=(m // tm,),
        in_specs=[pl.BlockSpec((tm, d), lambda i: (i, 0)),
                  pl.BlockSpec((d, LANES), lambda i: (0, 0)),
                  pl.BlockSpec((d, LANES), lambda i: (0, 0))],
        out_specs=[pl.BlockSpec((tm, LANES), lambda i: (i, 0))] * 2,
        out_shape=[jax.ShapeDtypeStruct((m, LANES), F32)] * 2,
        compiler_params=_cp("parallel"),
        name="router",
    )(x, whi, wlo)


def _row_copy(src_hbm, row, dst_ref, r, sem):
    return pltpu.make_async_copy(src_hbm.at[pl.ds(row, 1), :], dst_ref.at[pl.ds(r, 1), :], sem)


def _gather_kernel(idx_ref, x_hbm, o_ref, sem, *, tg):
    def start(r, carry):
        _row_copy(x_hbm, idx_ref[0, 0, r], o_ref, r, sem).start()
        return carry

    def wait(r, carry):
        _row_copy(x_hbm, idx_ref[0, 0, r], o_ref, r, sem).wait()
        return carry

    lax.fori_loop(0, tg, start, 0)
    lax.fori_loop(0, tg, wait, 0)


def gather_rows(x, idx, tg=256):
    ms = idx.shape[0]
    d = x.shape[1]
    return pl.pallas_call(
        functools.partial(_gather_kernel, tg=tg),
        grid=(ms // tg,),
        in_specs=[pl.BlockSpec((1, 1, tg), lambda i: (i, 0, 0), memory_space=pltpu.SMEM),
                  pl.BlockSpec(memory_space=pl.ANY)],
        out_specs=pl.BlockSpec((tg, d), lambda i: (i, 0)),
        out_shape=jax.ShapeDtypeStruct((ms, d), x.dtype),
        scratch_shapes=[pltpu.SemaphoreType.DMA(())],
        compiler_params=_cp("arbitrary"),
        name="gather_rows",
    )(idx.reshape(ms // tg, 1, tg), x)


def _combine_kernel(pos_ref, ys_hbm, g_ref, res_ref, gm_ref, o_ref, buf, sem, *, tc, n_sel):
    def start(r, carry):
        for j in range(n_sel):
            _row_copy(ys_hbm, pos_ref[0, 0, r * n_sel + j], buf.at[j], r, sem).start()
        return carry

    def wait(r, carry):
        for j in range(n_sel):
            _row_copy(ys_hbm, pos_ref[0, 0, r * n_sel + j], buf.at[j], r, sem).wait()
        return carry

    lax.fori_loop(0, tc, start, 0)
    lax.fori_loop(0, tc, wait, 0)
    acc = g_ref[:, 0:1] * buf[0]
    for j in range(1, n_sel):
        acc = acc + g_ref[:, j:j + 1] * buf[j]
    o_ref[...] = res_ref[...] + gm_ref[0] * acc


def moe_combine(ys, pos, gates, res, modtab, gate_row, n_ctx, t_lat, tc=128):
    m, d = res.shape
    n_sel = pos.shape[1]
    return pl.pallas_call(
        functools.partial(_combine_kernel, tc=tc, n_sel=n_sel),
        grid=(m // tc,),
        in_specs=[pl.BlockSpec((1, 1, tc * n_sel), lambda i: (i, 0, 0), memory_space=pltpu.SMEM),
                  pl.BlockSpec(memory_space=pl.ANY),
                  pl.BlockSpec((tc, n_sel), lambda i: (i, 0)),
                  pl.BlockSpec((tc, d), lambda i: (i, 0)),
                  pl.BlockSpec((1, 1, d),
                               lambda i: (_group_of_tile(i, tc, n_ctx, t_lat) * 6 + gate_row, 0, 0))],
        out_specs=pl.BlockSpec((tc, d), lambda i: (i, 0)),
        out_shape=jax.ShapeDtypeStruct((m, d), F32),
        scratch_shapes=[pltpu.VMEM((n_sel, tc, d), F32), pltpu.SemaphoreType.DMA(())],
        compiler_params=_cp("arbitrary"),
        name="moe_combine",
    )(pos.reshape(m // tc, 1, tc * n_sel), ys, gates, res, modtab)


def routing_tables(gates, sel, n_experts, tm):
    m = gates.shape[0]
    g = gates[:, :n_experts]
    pairs = sel[:, :n_experts] > 0.5
    pi = pairs.astype(jnp.int32)
    cnt = jnp.sum(pi, axis=0)
    padded = ((cnt + tm - 1) // tm) * tm
    ends = jnp.cumsum(padded)
    starts = ends - padded
    rank = jnp.cumsum(pi, axis=0) - pi
    pos = starts[None, :] + rank
    ms = TOP_K * m + n_experts * tm
    first = jnp.argmax(pairs, axis=1)
    second = n_experts - 1 - jnp.argmax(pairs[:, ::-1], axis=1)
    pick = jnp.stack([first, second], axis=1)
    pos_sel = jnp.take_along_axis(pos, pick, axis=1)
    g_sel = jnp.take_along_axis(g, pick, axis=1)
    tok = jnp.broadcast_to(jnp.arange(m, dtype=jnp.int32)[:, None], pos.shape)
    tok_sorted = jnp.zeros((ms,), jnp.int32).at[jnp.where(pairs, pos, ms).reshape(-1)].set(
        tok.reshape(-1), mode="drop")
    tile_start = jnp.arange(ms // tm, dtype=jnp.int32) * tm
    tile_expert = jnp.minimum(
        jnp.sum(tile_start[:, None] >= ends[None, :], axis=1), n_experts - 1).astype(jnp.int32)
    n_valid = (ends[-1] // tm).astype(jnp.int32).reshape(1)
    return tok_sorted, tile_expert, n_valid, pos_sel.astype(jnp.int32), g_sel


def moe_layer(h, x_res, w_router, wg, wu, wd, modtab, gate_row, n_ctx, t_lat, tm=512):
    n_experts = wg.shape[0]
    f = wg.shape[2]
    gates, sel = router(h, w_router)
    tok_sorted, te, nv, pos_sel, g_sel = routing_tables(gates, sel, n_experts, tm)
    xs = gather_rows(h, tok_sorted)
    act = mm_swiglu(xs, wg, wu, te, nv, tm=tm, tn=1024)
    ys = mm_k(act, wd, te, nv, tm=tm, tn=min(1024, wd.shape[2]), tk=f // 2)
    return moe_combine(ys, pos_sel, g_sel, x_res, modtab, gate_row, n_ctx, t_lat)


def _dn_ba_layout(hb):
    n_hg = DN_V_HEADS // hb
    idx = np.full((n_hg, LANES), -1, np.int64)
    for hg in range(n_hg):
        for d in range(2):
            for ab in range(2):
                for hh in range(hb):
                    idx[hg, d * 2 * hb + ab * hb + hh] = d * 2 * DN_V_HEADS + ab * DN_V_HEADS + hg * hb + hh
    return idx.reshape(-1)


def _permute_cols(a, idx):
    valid = jnp.asarray(idx >= 0)
    return jnp.where(valid, jnp.take(a, jnp.asarray(np.maximum(idx, 0)), axis=-1), 0.0)


def kernel(x_prompt, x_sample, state_dn, cache_k, cache_v, c, c_ctx, w_mod, b_mod, norm_mix, norm_ffn, norm_final, dn_w_in, dn_conv, dn_A_log, dn_dt_bias, dn_out_norm, dn_w_out, att_w_in, att_q_norm, att_k_norm, att_w_out, ffn_w_gate, ffn_w_up, ffn_w_down, moe_router, moe_w_gate, moe_w_up, moe_w_down):
    bc, tc_, d = x_prompt.shape
    bx, tx, _ = x_sample.shape
    depth = w_mod.shape[0]
    n_ctx = bc * tc_
    m = n_ctx + bx * tx
    past = cache_k.shape[2]
    qk_dim = DN_K_HEADS * DN_HEAD_DIM
    v_dim = DN_V_HEADS * DN_HEAD_DIM
    conv_dim = 2 * qk_dim + v_dim
    kv_dim = ATT_KV_HEADS * ATT_HEAD_DIM
    q_dim = ATT_Q_HEADS * ATT_HEAD_DIM

    x = jnp.concatenate([x_prompt.reshape(n_ctx, d), x_sample.reshape(bx * tx, d)], axis=0)
    n_groups = 16
    cvec = jnp.zeros((n_groups, d), F32).at[0].set(c_ctx).at[1:1 + bx].set(c)
    mods = adaln_all(cvec, w_mod, b_mod)
    cos, sin = rope_tables(tx // GRID_W, ATT_HEAD_DIM)
    ba_idx = _dn_ba_layout(DN_HEADS_PER_STEP)
    n_hg = DN_V_HEADS // DN_HEADS_PER_STEP

    new_dn, new_k, new_v = [], [], []
    for i in range(depth):
        j = i // 2
        modtab = mods[i].reshape(n_groups * 6, 1, d)
        h = norm_mod(x, norm_mix[i], modtab, (0, 1), n_ctx, tx, BF16)
        if i % 2 == 0:
            w_in = dn_w_in[j]
            proj = mm(h, w_in[:, :conv_dim + v_dim].astype(BF16), F32)
            ba = mm(h, _permute_cols(w_in[:, conv_dim + v_dim:], ba_idx).astype(BF16), F32)
            old = jnp.zeros((2, 2, DN_V_HEADS), F32)
            nega_old = old.at[:, 1].set(-jnp.exp(dn_A_log[j].astype(F32))).reshape(-1)
            dtb_old = old.at[:, 1].set(dn_dt_bias[j].astype(F32)).reshape(-1)
            nega = _permute_cols(nega_old, ba_idx).reshape(n_hg, 1, LANES)
            dtb = _permute_cols(dtb_old, ba_idx).reshape(n_hg, 1, LANES)
            qkvc = dn_short_conv(proj, dn_conv[j], n_ctx=n_ctx, l_ctx=tc_, l_lat=tx,
                                 qk_dim=qk_dim, conv_dim=conv_dim, dk=DN_HEAD_DIM)
            o_f, o_b, s_out = dn_scan(qkvc, ba, nega, dtb, state_dn[:, j].astype(F32),
                                      [(bc, tc_, False), (bx, tx, True)],
                                      qk_dim=qk_dim, n_vheads=DN_V_HEADS)
            new_dn.append(s_out[:bc])
            og = dn_gate(o_f, o_b, proj, dn_out_norm[j], conv_dim)
            te, nv = _one_expert_tables(m, 512)
            x = mm_k(og, dn_w_out[j].astype(BF16)[None], te, nv, 512, 1024, v_dim,
                     res=x, modtab=modtab, gate_row=2, n_ctx=n_ctx, t_lat=tx)
        else:
            qkv = mm(h, att_w_in[j].astype(BF16), F32)
            q, k, v, kf = attn_prep(qkv, cos, sin, att_q_norm[j], att_k_norm[j], n_ctx, tx,
                                    ATT_Q_HEADS, ATT_KV_HEADS, ATT_HEAD_DIM)
            o_ctx = attention(q, k, v, None, batch=bc, seq=tc_, row0=0,
                              hq=ATT_Q_HEADS, hkv=ATT_KV_HEADS, dh=ATT_HEAD_DIM, tq=tc_)
            ck = cache_k[:, j].reshape(bx * past, kv_dim).astype(BF16)
            cv = cache_v[:, j].reshape(bx * past, kv_dim).astype(BF16)
            o_lat = attention(q, k, v, (ck, cv), batch=bx, seq=tx, row0=n_ctx,
                              hq=ATT_Q_HEADS, hkv=ATT_KV_HEADS, dh=ATT_HEAD_DIM, tq=256)
            o = jnp.concatenate([o_ctx, o_lat], axis=0)
            new_k.append(kf[:n_ctx].reshape(bc, tc_, ATT_KV_HEADS, ATT_HEAD_DIM))
            new_v.append(qkv[:n_ctx, q_dim + kv_dim:].reshape(bc, tc_, ATT_KV_HEADS, ATT_HEAD_DIM))
            te, nv = _one_expert_tables(m, 512)
            x = mm_k(o, att_w_out[j].astype(BF16)[None], te, nv, 512, 1024, q_dim,
                     res=x, modtab=modtab, gate_row=2, n_ctx=n_ctx, t_lat=tx)
        if i % 2 == 0:
            h = norm_mod(x, norm_ffn[i], modtab, (3, 4), n_ctx, tx, BF16)
            f = ffn_w_gate.shape[2]
            te, nv = _one_expert_tables(m, 1024)
            act = mm_swiglu(h, ffn_w_gate[j].astype(BF16)[None], ffn_w_up[j].astype(BF16)[None],
                            te, nv, tm=1024, tn=512)
            te, nv = _one_expert_tables(m, 512)
            x = mm_k(act, ffn_w_down[j].astype(BF16)[None], te, nv, 512, 1024, f // 2,
                     res=x, modtab=modtab, gate_row=5, n_ctx=n_ctx, t_lat=tx)
        else:
            h = norm_mod(x, norm_ffn[i], modtab, (3, 4), n_ctx, tx, F32)
            x = moe_layer(h, x, moe_router[j], moe_w_gate[j].astype(BF16),
                          moe_w_up[j].astype(BF16), moe_w_down[j].astype(BF16),
                          modtab, 5, n_ctx, tx)
    y = norm_mod(x, norm_final, None, (), n_ctx, tx, F32)
    y_prompt = y[:n_ctx].reshape(bc, tc_, d)
    y_sample = y[n_ctx:].reshape(bx, tx, d)
    return (y_prompt, y_sample, jnp.stack(new_dn, axis=1),
            jnp.stack(new_k, axis=1), jnp.stack(new_v, axis=1))
```

```python
import functools
import math

import numpy as np
import jax
import jax.numpy as jnp
from jax import lax
from jax.experimental import pallas as pl
from jax.experimental.pallas import tpu as pltpu

F32 = jnp.float32
BF16 = jnp.bfloat16
EPS = 1e-6

GRID_W = 64
ROPE_THETA = 10000.0
DN_K_HEADS = 16
DN_V_HEADS = 32
DN_HEAD_DIM = 128
DN_CHUNK = 64
ATT_Q_HEADS = 16
ATT_KV_HEADS = 2
ATT_HEAD_DIM = 256
N_EXPERTS = 8
TOP_K = 2

LANES = 128
VMEM_LIMIT_BYTES = 56 * 1024 * 1024

DN_HEADS_PER_STEP = 16


def _cp(*sem):
    return pltpu.CompilerParams(dimension_semantics=sem, vmem_limit_bytes=VMEM_LIMIT_BYTES)


def _silu(x):
    return x * jax.nn.sigmoid(x)


def _group_of_tile(i, tm, n_ctx, t_lat):
    row = i * tm
    return jnp.where(row < n_ctx, 0, 1 + (row - n_ctx) // t_lat)


def _adaln_kernel(c_ref, w_ref, b_ref, o_ref):
    s = _silu(c_ref[...]).astype(BF16)
    o_ref[0] = jnp.dot(s, w_ref[0].astype(BF16), preferred_element_type=F32) + b_ref[0]


def adaln_all(cvec, w_mod, b_mod, tn=1024):
    n_layers, d, n = w_mod.shape
    g = cvec.shape[0]
    return pl.pallas_call(
        _adaln_kernel,
        grid=(n_layers, n // tn),
        in_specs=[
            pl.BlockSpec((g, d), lambda l, j: (0, 0)),
            pl.BlockSpec((1, d, tn), lambda l, j: (l, 0, j)),
            pl.BlockSpec((1, 1, tn), lambda l, j: (l, 0, j)),
        ],
        out_specs=pl.BlockSpec((1, g, tn), lambda l, j: (l, 0, j)),
        out_shape=jax.ShapeDtypeStruct((n_layers, g, n), F32),
        compiler_params=_cp("parallel", "parallel"),
        name="adaln",
    )(cvec, w_mod, b_mod.reshape(n_layers, 1, n))


def _norm_mod_kernel(x_ref, w_ref, *rest, modulated):
    x = x_ref[...]
    y = x * lax.rsqrt(jnp.mean(x * x, axis=-1, keepdims=True) + EPS) * w_ref[...]
    if modulated:
        sh_ref, sc_ref, o_ref = rest
        y = y * (1.0 + sc_ref[0]) + sh_ref[0]
    else:
        (o_ref,) = rest
    o_ref[...] = y.astype(o_ref.dtype)


def norm_mod(x, w, modtab, rows, n_ctx, t_lat, out_dtype, tm=512):
    m, d = x.shape
    in_specs = [pl.BlockSpec((tm, d), lambda i: (i, 0)), pl.BlockSpec((1, d), lambda i: (0, 0))]
    args = [x, w.reshape(1, d)]
    if modtab is not None:
        for r in rows:
            in_specs.append(pl.BlockSpec(
                (1, 1, d), lambda i, r=r: (_group_of_tile(i, tm, n_ctx, t_lat) * 6 + r, 0, 0)))
            args.append(modtab)
    return pl.pallas_call(
        functools.partial(_norm_mod_kernel, modulated=modtab is not None),
        grid=(m // tm,),
        in_specs=in_specs,
        out_specs=pl.BlockSpec((tm, d), lambda i: (i, 0)),
        out_shape=jax.ShapeDtypeStruct((m, d), out_dtype),
        compiler_params=_cp("parallel"),
        name="norm_mod",
    )(*args)


def _mm_kernel(x_ref, w_ref, o_ref):
    o_ref[...] = jnp.dot(x_ref[...], w_ref[...], preferred_element_type=F32).astype(o_ref.dtype)


def mm(x, w, out_dtype, tm=1024, tn=1024):
    m, k = x.shape
    n = w.shape[1]
    tm, tn = min(tm, m), min(tn, n)
    return pl.pallas_call(
        _mm_kernel,
        grid=(m // tm, n // tn),
        in_specs=[pl.BlockSpec((tm, k), lambda i, j: (i, 0)),
                  pl.BlockSpec((k, tn), lambda i, j: (0, j))],
        out_specs=pl.BlockSpec((tm, tn), lambda i, j: (i, j)),
        out_shape=jax.ShapeDtypeStruct((m, n), out_dtype),
        compiler_params=_cp("parallel", "parallel"),
        name="mm",
    )(x, w)


def _swiglu_kernel(te_ref, nv_ref, x_ref, wg_ref, wu_ref, o_ref):
    i = pl.program_id(0)

    @pl.when(i < nv_ref[0])
    def _():
        x = x_ref[...].astype(BF16)
        g = jnp.dot(x, wg_ref[0], preferred_element_type=F32)
        u = jnp.dot(x, wu_ref[0], preferred_element_type=F32)
        o_ref[...] = (_silu(g) * u).astype(o_ref.dtype)

    @pl.when(i >= nv_ref[0])
    def _():
        o_ref[...] = jnp.zeros_like(o_ref)


def mm_swiglu(x, wg, wu, tile_expert, n_valid, tm, tn):
    m, k = x.shape
    f = wg.shape[2]

    def xmap(i, j, te, nv):
        return (jnp.minimum(i, nv[0] - 1), 0)

    def wmap(i, j, te, nv):
        return (te[jnp.minimum(i, nv[0] - 1)], 0, j)

    return pl.pallas_call(
        _swiglu_kernel,
        grid_spec=pltpu.PrefetchScalarGridSpec(
            num_scalar_prefetch=2,
            grid=(m // tm, f // tn),
            in_specs=[pl.BlockSpec((tm, k), xmap),
                      pl.BlockSpec((1, k, tn), wmap),
                      pl.BlockSpec((1, k, tn), wmap)],
            out_specs=pl.BlockSpec((tm, tn), lambda i, j, te, nv: (i, j)),
        ),
        out_shape=jax.ShapeDtypeStruct((m, f), BF16),
        compiler_params=_cp("parallel", "arbitrary"),
        name="mm_swiglu",
    )(tile_expert, n_valid, x, wg, wu)


def _moe_swiglu_kernel(te_ref, nv_ref, x_ref, wg_ref, wu_ref, o_ref, wgb_ref, wub_ref):
    i = pl.program_id(1)
    valid = i < nv_ref[0]
    changed = (i == 0) | (te_ref[i] != te_ref[jnp.maximum(i, 1) - 1])

    @pl.when(valid & changed)
    def _():
        wgb_ref[...] = wg_ref[0, 0].astype(BF16)
        wub_ref[...] = wu_ref[0, 0].astype(BF16)

    @pl.when(valid)
    def _():
        x = x_ref[...]
        g = jnp.dot(x, wgb_ref[...], preferred_element_type=F32)
        u = jnp.dot(x, wub_ref[...], preferred_element_type=F32)
        o_ref[...] = (_silu(g) * u).astype(o_ref.dtype)

    @pl.when(jnp.logical_not(valid))
    def _():
        o_ref[...] = jnp.zeros_like(o_ref)


def moe_swiglu(x, wg, wu, layer, tile_expert, n_valid, tm, tn):
    m, k = x.shape
    f = wg.shape[3]

    def xmap(j, i, te, nv):
        return (jnp.minimum(i, nv[0] - 1), 0)

    def wmap(j, i, te, nv):
        return (layer, te[jnp.minimum(i, nv[0] - 1)], 0, j)

    return pl.pallas_call(
        _moe_swiglu_kernel,
        grid_spec=pltpu.PrefetchScalarGridSpec(
            num_scalar_prefetch=2,
            grid=(f // tn, m // tm),
            in_specs=[pl.BlockSpec((tm, k), xmap),
                      pl.BlockSpec((1, 1, k, tn), wmap),
                      pl.BlockSpec((1, 1, k, tn), wmap)],
            out_specs=pl.BlockSpec((tm, tn), lambda j, i, te, nv: (i, j)),
            scratch_shapes=[pltpu.VMEM((k, tn), BF16), pltpu.VMEM((k, tn), BF16)],
        ),
        out_shape=jax.ShapeDtypeStruct((m, f), BF16),
        compiler_params=_cp("arbitrary", "arbitrary"),
        name="moe_swiglu",
    )(tile_expert, n_valid, x, wg, wu)


def _mmk_kernel(te_ref, nv_ref, x_ref, w_ref, *rest, nk, has_res):
    if has_res:
        res_ref, g_ref, o_ref = rest
    else:
        (o_ref,) = rest
    i = pl.program_id(0)
    k = pl.program_id(2)

    def finish(acc):
        if has_res:
            return res_ref[...] + g_ref[0] * acc
        return acc

    @pl.when(i < nv_ref[0])
    def _():
        part = jnp.dot(x_ref[...].astype(BF16), w_ref[0], preferred_element_type=F32)
        if nk == 1:
            o_ref[...] = finish(part)
        else:
            @pl.when(k == 0)
            def _():
                o_ref[...] = part

            if nk > 2:
                @pl.when((k > 0) & (k < nk - 1))
                def _():
                    o_ref[...] += part

            @pl.when(k == nk - 1)
            def _():
                o_ref[...] = finish(o_ref[...] + part)

    @pl.when(i >= nv_ref[0])
    def _():
        o_ref[...] = jnp.zeros_like(o_ref)


def mm_k(x, w, tile_expert, n_valid, tm, tn, tk, res=None, modtab=None, gate_row=None,
         n_ctx=0, t_lat=1):
    m, k = x.shape
    n = w.shape[2]
    nk = k // tk

    def xmap(i, j, kk, te, nv):
        return (jnp.minimum(i, nv[0] - 1), kk)

    def wmap(i, j, kk, te, nv):
        return (te[jnp.minimum(i, nv[0] - 1)], kk, j)

    in_specs = [pl.BlockSpec((tm, tk), xmap), pl.BlockSpec((1, tk, tn), wmap)]
    args = [x, w]
    if res is not None:
        in_specs.append(pl.BlockSpec((tm, tn), lambda i, j, kk, te, nv: (i, j)))
        in_specs.append(pl.BlockSpec(
            (1, 1, tn),
            lambda i, j, kk, te, nv: (_group_of_tile(i, tm, n_ctx, t_lat) * 6 + gate_row, 0, j)))
        args += [res, modtab]
    return pl.pallas_call(
        functools.partial(_mmk_kernel, nk=nk, has_res=res is not None),
        grid_spec=pltpu.PrefetchScalarGridSpec(
            num_scalar_prefetch=2,
            grid=(m // tm, n // tn, nk),
            in_specs=in_specs,
            out_specs=pl.BlockSpec((tm, tn), lambda i, j, kk, te, nv: (i, j)),
        ),
        out_shape=jax.ShapeDtypeStruct((m, n), F32),
        compiler_params=_cp("parallel", "parallel", "arbitrary"),
        name="mm_k",
    )(tile_expert, n_valid, *args)


def _one_expert_tables(m, tm):
    return jnp.zeros((m // tm,), jnp.int32), jnp.full((1,), m // tm, jnp.int32)


def rope_tables(rows, head_dim):
    n_freq = head_dim // 4
    inv = ROPE_THETA ** (-jnp.arange(n_freq, dtype=F32) / n_freq)
    r = jnp.repeat(jnp.arange(rows, dtype=F32), GRID_W)
    cl = jnp.tile(jnp.arange(GRID_W, dtype=F32), rows)
    ang = jnp.concatenate([r[:, None] * inv, cl[:, None] * inv], axis=-1)
    return jnp.cos(ang), jnp.sin(ang)


def _attn_prep_kernel(x_ref, cos_ref, sin_ref, qn_ref, kn_ref, q_ref, k_ref, v_ref, kf_ref,
                      *, hq, hkv, dh, n_ctx, tm):
    is_lat = pl.program_id(0) * tm >= n_ctx
    c = jnp.where(is_lat, cos_ref[...], 1.0)
    s = jnp.where(is_lat, sin_ref[...], 0.0)
    half = dh // 2
    for h in range(hq + hkv):
        xh = x_ref[:, h * dh:(h + 1) * dh]
        w = qn_ref[...] if h < hq else kn_ref[...]
        y = xh * lax.rsqrt(jnp.mean(xh * xh, axis=-1, keepdims=True) + EPS) * w
        y1, y2 = y[:, :half], y[:, half:]
        o1 = y1 * c - y2 * s
        o2 = y2 * c + y1 * s
        if h < hq:
            scale = dh ** -0.5
            q_ref[:, h * dh:h * dh + half] = (o1 * scale).astype(BF16)
            q_ref[:, h * dh + half:(h + 1) * dh] = (o2 * scale).astype(BF16)
        else:
            b = (h - hq) * dh
            k_ref[:, b:b + half] = o1.astype(BF16)
            k_ref[:, b + half:b + dh] = o2.astype(BF16)
            kf_ref[:, b:b + half] = o1
            kf_ref[:, b + half:b + dh] = o2
    v_ref[...] = x_ref[:, (hq + hkv) * dh:].astype(BF16)


def attn_prep(qkv, cos, sin, q_norm, k_norm, n_ctx, t_lat, hq, hkv, dh, tm=256):
    m = qkv.shape[0]
    half = dh // 2

    def posmap(i):
        row = i * tm
        return (jnp.where(row >= n_ctx, ((row - n_ctx) % t_lat) // tm, 0), 0)

    return pl.pallas_call(
        functools.partial(_attn_prep_kernel, hq=hq, hkv=hkv, dh=dh, n_ctx=n_ctx, tm=tm),
        grid=(m // tm,),
        in_specs=[pl.BlockSpec((tm, (hq + 2 * hkv) * dh), lambda i: (i, 0)),
                  pl.BlockSpec((tm, half), posmap),
                  pl.BlockSpec((tm, half), posmap),
                  pl.BlockSpec((1, dh), lambda i: (0, 0)),
                  pl.BlockSpec((1, dh), lambda i: (0, 0))],
        out_specs=[pl.BlockSpec((tm, hq * dh), lambda i: (i, 0)),
                   pl.BlockSpec((tm, hkv * dh), lambda i: (i, 0)),
                   pl.BlockSpec((tm, hkv * dh), lambda i: (i, 0)),
                   pl.BlockSpec((tm, hkv * dh), lambda i: (i, 0))],
        out_shape=[jax.ShapeDtypeStruct((m, hq * dh), BF16),
                   jax.ShapeDtypeStruct((m, hkv * dh), BF16),
                   jax.ShapeDtypeStruct((m, hkv * dh), BF16),
                   jax.ShapeDtypeStruct((m, hkv * dh), F32)],
        compiler_params=_cp("parallel"),
        name="attn_prep",
    )(qkv, cos, sin, q_norm.reshape(1, dh), k_norm.reshape(1, dh))


def _attn_kernel(*refs, n_src, n_group, dh):
    q_ref = refs[0]
    kv = refs[1:1 + 2 * n_src]
    o_ref = refs[-1]
    nt = (((1,), (1,)), ((), ()))
    for g in range(n_group):
        qg = q_ref[:, g * dh:(g + 1) * dh]
        ss = [lax.dot_general(qg, kv[2 * i][...], nt, preferred_element_type=F32)
              for i in range(n_src)]
        mx = functools.reduce(jnp.maximum, [jnp.max(s, axis=-1, keepdims=True) for s in ss])
        ps = [jnp.exp(s - mx) for s in ss]
        den = functools.reduce(jnp.add, [jnp.sum(p, axis=-1, keepdims=True) for p in ps])
        o = functools.reduce(jnp.add, [
            jnp.dot(p.astype(BF16), kv[2 * i + 1][...], preferred_element_type=F32)
            for i, p in enumerate(ps)])
        o_ref[:, g * dh:(g + 1) * dh] = (o / den).astype(o_ref.dtype)


def attention(q, k, v, cache, *, batch, seq, row0, hq, hkv, dh, tq):
    n_group = hq // hkv
    nq = seq // tq
    qb0, kb0 = row0 // tq, row0 // seq
    in_specs = [pl.BlockSpec((tq, n_group * dh), lambda b, h, t: (qb0 + b * nq + t, h))]
    args = [q]
    if cache is not None:
        p_len = cache[0].shape[0] // batch
        in_specs += [pl.BlockSpec((p_len, dh), lambda b, h, t: (b, h))] * 2
        args += list(cache)
    in_specs += [pl.BlockSpec((seq, dh), lambda b, h, t: (kb0 + b, h))] * 2
    args += [k, v]
    n_src = (len(args) - 1) // 2
    return pl.pallas_call(
        functools.partial(_attn_kernel, n_src=n_src, n_group=n_group, dh=dh),
        grid=(batch, hkv, nq),
        in_specs=in_specs,
        out_specs=pl.BlockSpec((tq, n_group * dh), lambda b, h, t: (b * nq + t, h)),
        out_shape=jax.ShapeDtypeStruct((batch * seq, hq * dh), BF16),
        compiler_params=_cp("parallel", "parallel", "arbitrary"),
        name="attention",
    )(*args)


def _dn_conv_kernel(x_ref, w_ref, o_ref, pad_ref, *, t, tc, n_ctx_blocks, l_ctx, l_lat,
                    n_q_tiles, n_qk_tiles, dk):
    j = pl.program_id(1)
    zeros8 = jnp.zeros((8, tc), F32)
    pad_ref[0:8, :] = zeros8
    pad_ref[t + 8:t + 16, :] = zeros8
    pad_ref[8:t + 8, :] = x_ref[...]
    w = w_ref[...]
    seq_len = jnp.where(pl.program_id(0) < n_ctx_blocks, l_ctx, l_lat)
    pos = lax.broadcasted_iota(jnp.int32, (t, tc), 0) & (seq_len - 1)
    y = (w[0:1] * jnp.where(pos >= 1, pad_ref[7:t + 7, :], 0.0)
         + w[1:2] * pad_ref[8:t + 8, :]
         + w[2:3] * jnp.where(pos <= seq_len - 2, pad_ref[9:t + 9, :], 0.0)
         + w[3:4] * jnp.where(pos <= seq_len - 3, pad_ref[10:t + 10, :], 0.0))
    y = _silu(y)
    is_qk = j < n_qk_tiles
    q_scale = jnp.where(j < n_q_tiles, dk ** -0.5, 1.0)
    for h in range(tc // LANES):
        yh = y[:, h * LANES:(h + 1) * LANES]
        ss = jnp.sum(yh * yh, axis=-1, keepdims=True)
        inv = jnp.where(is_qk, lax.rsqrt(ss + EPS) * q_scale, 1.0)
        o_ref[:, h * LANES:(h + 1) * LANES] = (yh * inv).astype(o_ref.dtype)


def dn_short_conv(proj, conv_w, *, n_ctx, l_ctx, l_lat, qk_dim, conv_dim, dk, tc=512):
    m = proj.shape[0]
    assert l_lat % l_ctx == 0 and n_ctx % l_lat == 0 and m % l_lat == 0
    assert l_ctx & (l_ctx - 1) == 0 and l_lat & (l_lat - 1) == 0 and l_ctx >= 4
    return pl.pallas_call(
        functools.partial(_dn_conv_kernel, t=l_lat, tc=tc, n_ctx_blocks=n_ctx // l_lat,
                          l_ctx=l_ctx, l_lat=l_lat, n_q_tiles=qk_dim // tc,
                          n_qk_tiles=2 * qk_dim // tc, dk=dk),
        grid=(m // l_lat, conv_dim // tc),
        in_specs=[pl.BlockSpec((l_lat, tc), lambda b, j: (b, j)),
                  pl.BlockSpec((conv_w.shape[0], tc), lambda b, j: (0, j))],
        out_specs=pl.BlockSpec((l_lat, tc), lambda b, j: (b, j)),
        out_shape=jax.ShapeDtypeStruct((m, conv_dim), BF16),
        scratch_shapes=[pltpu.VMEM((l_lat + 16, tc), F32)],
        compiler_params=_cp("parallel", "parallel"),
        name="dn_conv",
    )(proj, conv_w)


def _softplus(x):
    return jnp.maximum(x, 0.0) + jnp.log1p(jnp.exp(-jnp.abs(x)))


def _split_bf16(a):
    hi = a.astype(BF16)
    return hi, (a - hi.astype(F32)).astype(BF16)


def _mm_3pass(a, b):
    ah, al = a
    bh, bl = b
    n = ah.shape[0]
    both = jnp.dot(jnp.concatenate([ah, al], axis=0), bh, preferred_element_type=F32)
    return both[:n] + both[n:] + jnp.dot(ah, bl, preferred_element_type=F32)


def _pair_masks(n):
    r = lax.broadcasted_iota(jnp.int32, (n, n), 0)
    c = lax.broadcasted_iota(jnp.int32, (n, n), 1)
    return [((r >> (k + 1)) == (c >> (k + 1))) & ((r >> k) != (c >> k))
            for k in range(int(math.log2(n)))]


def _unit_tri_inverses(lows, eye, masks):
    ds = [eye - jnp.where(masks[0], low, 0.0) for low in lows]
    for mask in masks[1:]:
        dsp = [_split_bf16(d) for d in ds]
        ts = [_mm_3pass(d, _split_bf16(jnp.where(mask, low, 0.0))) for d, low in zip(dsp, lows)]
        ds = [d - _mm_3pass(_split_bf16(t), dp) for d, t, dp in zip(ds, ts, dsp)]
    return ds


def _dn_scan_kernel(rowf_ref, rowb_ref, flag_ref, seq_ref,
                    qf_ref, kf_ref, vf_ref, qb_ref, kb_ref, vb_ref, baf_ref, bab_ref,
                    nega_ref, dtb_ref, s0_ref, of_ref, ob_ref, sout_ref, s_scr, *, hb, chunk):
    step = pl.program_id(1)
    flag = flag_ref[step]
    first = (flag & 1) == 1
    last = (flag & 2) == 2
    is_lat = (flag & 4) == 4
    dh = DN_HEAD_DIM
    nt = (((1,), (1,)), ((), ()))

    @pl.when(first)
    def _():
        s_scr[...] = jnp.where(is_lat, s0_ref[0], 0.0)

    r = lax.broadcasted_iota(jnp.int32, (chunk, chunk), 0)
    c = lax.broadcasted_iota(jnp.int32, (chunk, chunk), 1)
    eye = (r == c).astype(F32)
    masks = _pair_masks(chunk)

    heads = []
    lows = []
    for d, q_ref, k_ref, v_ref, ba_ref, o_ref in ((0, qf_ref, kf_ref, vf_ref, baf_ref, of_ref),
                                                   (1, qb_ref, kb_ref, vb_ref, bab_ref, ob_ref)):
        incl = (r >= c) if d == 0 else (r <= c)
        strict = (r > c) if d == 0 else (r < c)
        last_row = chunk - 1 if d == 0 else 0
        ba = ba_ref[...]
        sig = jax.nn.sigmoid(ba)
        g = nega_ref[0] * _softplus(ba + dtb_ref[0])
        gc = jnp.dot(incl.astype(F32), g, preferred_element_type=F32,
                     precision=lax.Precision.HIGHEST)
        gct = gc.T
        egc = jnp.exp(gc)
        glast = gc[last_row:last_row + 1, :]
        eglast = jnp.exp(glast)
        ekg = jnp.exp(glast - gc)
        for kh in range(hb // 2):
            qh = q_ref[:, kh * dh:(kh + 1) * dh]
            kk_ = k_ref[:, kh * dh:(kh + 1) * dh]
            kf32 = kk_.astype(F32)
            qf32 = qh.astype(F32)
            kkt = lax.dot_general(kk_, kk_, nt, preferred_element_type=F32)
            qkt = lax.dot_general(qh, kk_, nt, preferred_element_type=F32)
            for rr in range(2):
                hh = kh * 2 + rr
                cb = d * 2 * hb + hh
                ca = d * 2 * hb + hb + hh
                bcol = sig[:, cb:cb + 1]
                ecol = egc[:, ca:ca + 1]
                decay = jnp.exp(jnp.where(incl, gc[:, ca:ca + 1] - gct[ca:ca + 1, :], -jnp.inf))
                lows.append(jnp.where(strict, bcol * kkt * decay, 0.0))
                heads.append(dict(
                    d=d, hh=hh, o_ref=o_ref,
                    vbm=(v_ref[:, hh * dh:(hh + 1) * dh].astype(F32) * bcol).astype(BF16),
                    kbg=(kf32 * (bcol * ecol)).astype(BF16),
                    a=jnp.where(incl, qkt * decay, 0.0).astype(BF16),
                    qg=(qf32 * ecol).astype(BF16),
                    kgt=(kf32 * ekg[:, ca:ca + 1]).T.astype(BF16),
                    egl=eglast[:, ca:ca + 1]))

    tinvs = [t.astype(BF16) for t in _unit_tri_inverses(lows, eye, masks)]
    us = [jnp.dot(t, h["vbm"], preferred_element_type=F32) for t, h in zip(tinvs, heads)]
    ws = [jnp.dot(t, h["kbg"], preferred_element_type=F32) for t, h in zip(tinvs, heads)]
    ss = [s_scr[h["d"], h["hh"]] for h in heads]
    sbs = [s.astype(BF16) for s in ss]
    vnbs = [(u - jnp.dot(w.astype(BF16), sb, preferred_element_type=F32)).astype(BF16)
            for u, w, sb in zip(us, ws, sbs)]
    for h, sb, vnb in zip(heads, sbs, vnbs):
        hh = h["hh"]
        h["o_ref"][:, hh * dh:(hh + 1) * dh] = (
            jnp.dot(h["qg"], sb, preferred_element_type=F32)
            + jnp.dot(h["a"], vnb, preferred_element_type=F32))
    for h, s, vnb in zip(heads, ss, vnbs):
        s_scr[h["d"], h["hh"]] = s * h["egl"] + jnp.dot(h["kgt"], vnb,
                                                        preferred_element_type=F32)

    @pl.when(last)
    def _():
        sout_ref[0] = s_scr[...]


def dn_scan(qkvc, ba, nega, dtb, s0, seq_lens, *, qk_dim, n_vheads, hb=DN_HEADS_PER_STEP):
    m = qkvc.shape[0]
    chunk = DN_CHUNK
    dh = DN_HEAD_DIM
    n_hg = n_vheads // hb
    rowf, rowb, flags, seqs = [], [], [], []
    row, sid, n_zero_seq = 0, 0, 0
    for n_seq, length, uses_s0 in seq_lens:
        n_chunks = length // chunk
        for _ in range(n_seq):
            for n in range(n_chunks):
                rowf.append(row + n)
                rowb.append(row + n_chunks - 1 - n)
                flags.append((n == 0) * 1 + (n == n_chunks - 1) * 2 + (4 if uses_s0 else 0))
                seqs.append(sid)
            row += n_chunks
            sid += 1
        if not uses_s0:
            n_zero_seq += n_seq
    n_steps = len(rowf)
    n_seq_total = sid
    tabs = [jnp.asarray(np.array(t, np.int32)) for t in (rowf, rowb, flags, seqs)]
    qw, vw = (hb // 2) * dh, hb * dh
    kb0, vb0 = qk_dim // qw, 2 * qk_dim // vw

    def spec(width, col0, rows_idx):
        return pl.BlockSpec((chunk, width),
                            lambda hg, s, rf, rb, fl, sq: ((rf, rb)[rows_idx][s], col0 + hg))

    in_specs = [spec(qw, 0, 0), spec(qw, kb0, 0), spec(vw, vb0, 0),
                spec(qw, 0, 1), spec(qw, kb0, 1), spec(vw, vb0, 1),
                pl.BlockSpec((chunk, LANES), lambda hg, s, rf, rb, fl, sq: (rf[s], hg)),
                pl.BlockSpec((chunk, LANES), lambda hg, s, rf, rb, fl, sq: (rb[s], hg)),
                pl.BlockSpec((1, 1, LANES), lambda hg, s, rf, rb, fl, sq: (hg, 0, 0)),
                pl.BlockSpec((1, 1, LANES), lambda hg, s, rf, rb, fl, sq: (hg, 0, 0)),
                pl.BlockSpec((1, 2, hb, dh, dh),
                             lambda hg, s, rf, rb, fl, sq:
                             (jnp.maximum(sq[s] - n_zero_seq, 0), 0, hg, 0, 0))]
    out_specs = [pl.BlockSpec((chunk, vw), lambda hg, s, rf, rb, fl, sq: (rf[s], hg)),
                 pl.BlockSpec((chunk, vw), lambda hg, s, rf, rb, fl, sq: (rb[s], hg)),
                 pl.BlockSpec((1, 2, hb, dh, dh),
                              lambda hg, s, rf, rb, fl, sq: (sq[s], 0, hg, 0, 0))]
    return pl.pallas_call(
        functools.partial(_dn_scan_kernel, hb=hb, chunk=chunk),
        grid_spec=pltpu.PrefetchScalarGridSpec(
            num_scalar_prefetch=4,
            grid=(n_hg, n_steps),
            in_specs=in_specs,
            out_specs=out_specs,
            scratch_shapes=[pltpu.VMEM((2, hb, dh, dh), F32)],
        ),
        out_shape=[jax.ShapeDtypeStruct((m, n_vheads * dh), F32),
                   jax.ShapeDtypeStruct((m, n_vheads * dh), F32),
                   jax.ShapeDtypeStruct((n_seq_total, 2, n_vheads, dh, dh), F32)],
        compiler_params=_cp("parallel", "arbitrary"),
        name="dn_scan",
    )(*tabs, qkvc, qkvc, qkvc, qkvc, qkvc, qkvc, ba, ba, nega, dtb, s0)


def _dn_gate_kernel(of_ref, ob_ref, z_ref, w_ref, o_ref, *, n_heads):
    w = w_ref[...]
    for h in range(n_heads):
        sl = slice(h * DN_HEAD_DIM, (h + 1) * DN_HEAD_DIM)
        o = of_ref[:, sl] + ob_ref[:, sl]
        y = o * lax.rsqrt(jnp.mean(o * o, axis=-1, keepdims=True) + EPS) * w
        o_ref[:, sl] = (y * _silu(z_ref[:, sl])).astype(o_ref.dtype)


def dn_gate(o_f, o_b, proj, out_norm, z_col0, tm=256, tn=1024):
    m, n = o_f.shape
    zb0 = z_col0 // tn
    return pl.pallas_call(
        functools.partial(_dn_gate_kernel, n_heads=tn // DN_HEAD_DIM),
        grid=(m // tm, n // tn),
        in_specs=[pl.BlockSpec((tm, tn), lambda i, j: (i, j)),
                  pl.BlockSpec((tm, tn), lambda i, j: (i, j)),
                  pl.BlockSpec((tm, tn), lambda i, j: (i, zb0 + j)),
                  pl.BlockSpec((1, DN_HEAD_DIM), lambda i, j: (0, 0))],
        out_specs=pl.BlockSpec((tm, tn), lambda i, j: (i, j)),
        out_shape=jax.ShapeDtypeStruct((m, n), BF16),
        compiler_params=_cp("parallel", "parallel"),
        name="dn_gate",
    )(o_f, o_b, proj, out_norm.reshape(1, DN_HEAD_DIM))


def _router_kernel(x_ref, whi_ref, wlo_ref, g_ref, s_ref, *, n_experts):
    x = x_ref[...]
    xhi = x.astype(BF16)
    xlo = (x - xhi.astype(F32)).astype(BF16)
    lg = (jnp.dot(xhi, whi_ref[...], preferred_element_type=F32)
          + jnp.dot(xlo, whi_ref[...], preferred_element_type=F32)
          + jnp.dot(xhi, wlo_ref[...], preferred_element_type=F32))
    lane = lax.broadcasted_iota(jnp.int32, lg.shape, 1).astype(F32)
    neg = -jnp.inf
    lg = jnp.where(lane < n_experts, lg, neg)
    m1 = jnp.max(lg, axis=-1, keepdims=True)
    i1 = jnp.min(jnp.where(lg == m1, lane, float(LANES)), axis=-1, keepdims=True)
    mk1 = lane == i1
    lg2 = jnp.where(mk1, neg, lg)
    m2 = jnp.max(lg2, axis=-1, keepdims=True)
    i2 = jnp.min(jnp.where(lg2 == m2, lane, float(LANES)), axis=-1, keepdims=True)
    mk2 = lane == i2
    e = jnp.exp(m2 - m1)
    w1 = 1.0 / (1.0 + e)
    g_ref[...] = jnp.where(mk1, w1, 0.0) + jnp.where(mk2, e * w1, 0.0)
    s_ref[...] = jnp.where(mk1, 1.0, 0.0) + jnp.where(mk2, 1.0, 0.0)


def router(x, w_router, tm=512):
    m, d = x.shape
    e = w_router.shape[1]
    wp = jnp.zeros((d, LANES), F32).at[:, :e].set(w_router)
    whi = wp.astype(BF16)
    wlo = (wp - whi.astype(F32)).astype(BF16)
    return pl.pallas_call(
        functools.partial(_router_kernel, n_experts=e),
        grid=(m // tm,),
        in_specs=[pl.BlockSpec((tm, d), lambda i: (i, 0)),
                  pl.BlockSpec((d, LANES), lambda i: (0, 0)),
                  pl.BlockSpec((d, LANES), lambda i: (0, 0))],
        out_specs=[pl.BlockSpec((tm, LANES), lambda i: (i, 0))] * 2,
        out_shape=[jax.ShapeDtypeStruct((m, LANES), F32)] * 2,
        compiler_params=_cp("parallel"),
        name="router",
    )(x, whi, wlo)


def _row_copy(src_hbm, row, dst_ref, r, sem):
    return pltpu.make_async_copy(src_hbm.at[pl.ds(row, 1), :], dst_ref.at[pl.ds(r, 1), :], sem)


def _gather_kernel(idx_ref, x_hbm, o_ref, buf, sem, *, tg):
    def start(r, carry):
        _row_copy(x_hbm, idx_ref[0, 0, r], buf, r, sem).start()
        return carry

    def wait(r, carry):
        _row_copy(x_hbm, idx_ref[0, 0, r], buf, r, sem).wait()
        return carry

    lax.fori_loop(0, tg, start, 0, unroll=8)
    lax.fori_loop(0, tg, wait, 0, unroll=8)
    o_ref[...] = buf[...].astype(o_ref.dtype)


def gather_rows(x, idx, out_dtype, tg=256):
    ms = idx.shape[0]
    d = x.shape[1]
    return pl.pallas_call(
        functools.partial(_gather_kernel, tg=tg),
        grid=(ms // tg,),
        in_specs=[pl.BlockSpec((1, 1, tg), lambda i: (i, 0, 0), memory_space=pltpu.SMEM),
                  pl.BlockSpec(memory_space=pl.ANY)],
        out_specs=pl.BlockSpec((tg, d), lambda i: (i, 0)),
        out_shape=jax.ShapeDtypeStruct((ms, d), out_dtype),
        scratch_shapes=[pltpu.VMEM((tg, d), x.dtype), pltpu.SemaphoreType.DMA(())],
        compiler_params=_cp("arbitrary"),
        name="gather_rows",
    )(idx.reshape(ms // tg, 1, tg), x)


def _combine_kernel(pos_ref, ys_hbm, g_ref, res_ref, gm_ref, o_ref, buf, sem, *, tc, n_sel):
    def start(r, carry):
        for j in range(n_sel):
            _row_copy(ys_hbm, pos_ref[0, 0, r * n_sel + j], buf.at[j], r, sem).start()
        return carry

    def wait(r, carry):
        for j in range(n_sel):
            _row_copy(ys_hbm, pos_ref[0, 0, r * n_sel + j], buf.at[j], r, sem).wait()
        return carry

    lax.fori_loop(0, tc, start, 0)
    lax.fori_loop(0, tc, wait, 0)
    acc = g_ref[:, 0:1] * buf[0]
    for j in range(1, n_sel):
        acc = acc + g_ref[:, j:j + 1] * buf[j]
    o_ref[...] = res_ref[...] + gm_ref[0] * acc


def moe_combine(ys, pos, gates, res, modtab, gate_row, n_ctx, t_lat, tc=128):
    m, d = res.shape
    n_sel = pos.shape[1]
    return pl.pallas_call(
        functools.partial(_combine_kernel, tc=tc, n_sel=n_sel),
        grid=(m // tc,),
        in_specs=[pl.BlockSpec((1, 1, tc * n_sel), lambda i: (i, 0, 0), memory_space=pltpu.SMEM),
                  pl.BlockSpec(memory_space=pl.ANY),
                  pl.BlockSpec((tc, n_sel), lambda i: (i, 0)),
                  pl.BlockSpec((tc, d), lambda i: (i, 0)),
                  pl.BlockSpec((1, 1, d),
                               lambda i: (_group_of_tile(i, tc, n_ctx, t_lat) * 6 + gate_row, 0, 0))],
        out_specs=pl.BlockSpec((tc, d), lambda i: (i, 0)),
        out_shape=jax.ShapeDtypeStruct((m, d), F32),
        scratch_shapes=[pltpu.VMEM((n_sel, tc, d), F32), pltpu.SemaphoreType.DMA(())],
        compiler_params=_cp("arbitrary"),
        name="moe_combine",
    )(pos.reshape(m // tc, 1, tc * n_sel), ys, gates, res, modtab)


def routing_tables(gates, sel, n_experts, tm):
    m = gates.shape[0]
    g = gates[:, :n_experts]
    pairs = sel[:, :n_experts] > 0.5
    pi = pairs.astype(jnp.int32)
    cnt = jnp.sum(pi, axis=0)
    padded = ((cnt + tm - 1) // tm) * tm
    ends = jnp.cumsum(padded)
    starts = ends - padded
    rank = jnp.cumsum(pi, axis=0) - pi
    pos = starts[None, :] + rank
    ms = TOP_K * m + n_experts * tm
    first = jnp.argmax(pairs, axis=1)
    second = n_experts - 1 - jnp.argmax(pairs[:, ::-1], axis=1)
    pick = jnp.stack([first, second], axis=1)
    pos_sel = jnp.take_along_axis(pos, pick, axis=1)
    g_sel = jnp.take_along_axis(g, pick, axis=1)
    tok = jnp.broadcast_to(jnp.arange(m, dtype=jnp.int32)[:, None], pos.shape)
    tok_sorted = jnp.zeros((ms,), jnp.int32).at[jnp.where(pairs, pos, ms).reshape(-1)].set(
        tok.reshape(-1), mode="drop")
    tile_start = jnp.arange(ms // tm, dtype=jnp.int32) * tm
    tile_expert = jnp.minimum(
        jnp.sum(tile_start[:, None] >= ends[None, :], axis=1), n_experts - 1).astype(jnp.int32)
    n_valid = (ends[-1] // tm).astype(jnp.int32).reshape(1)
    return tok_sorted, tile_expert, n_valid, pos_sel.astype(jnp.int32), g_sel


def moe_layer(h, x_res, w_router, wg, wu, layer, wd, modtab, gate_row, n_ctx, t_lat, tm=512):
    n_experts = wg.shape[1]
    f = wg.shape[3]
    gates, sel = router(h, w_router)
    tok_sorted, te, nv, pos_sel, g_sel = routing_tables(gates, sel, n_experts, tm)
    xs = gather_rows(h, tok_sorted, BF16)
    act = moe_swiglu(xs, wg, wu, layer, te, nv, tm=tm, tn=1024)
    ys = mm_k(act, wd, te, nv, tm=tm, tn=min(1024, wd.shape[2]), tk=f // 2)
    return moe_combine(ys, pos_sel, g_sel, x_res, modtab, gate_row, n_ctx, t_lat)


def _dn_ba_layout(hb):
    n_hg = DN_V_HEADS // hb
    idx = np.full((n_hg, LANES), -1, np.int64)
    for hg in range(n_hg):
        for d in range(2):
            for ab in range(2):
                for hh in range(hb):
                    idx[hg, d * 2 * hb + ab * hb + hh] = d * 2 * DN_V_HEADS + ab * DN_V_HEADS + hg * hb + hh
    return idx.reshape(-1)


def _permute_cols(a, idx):
    valid = jnp.asarray(idx >= 0)
    return jnp.where(valid, jnp.take(a, jnp.asarray(np.maximum(idx, 0)), axis=-1), 0.0)


def kernel(x_prompt, x_sample, state_dn, cache_k, cache_v, c, c_ctx, w_mod, b_mod, norm_mix, norm_ffn, norm_final, dn_w_in, dn_conv, dn_A_log, dn_dt_bias, dn_out_norm, dn_w_out, att_w_in, att_q_norm, att_k_norm, att_w_out, ffn_w_gate, ffn_w_up, ffn_w_down, moe_router, moe_w_gate, moe_w_up, moe_w_down):
    bc, tc_, d = x_prompt.shape
    bx, tx, _ = x_sample.shape
    depth = w_mod.shape[0]
    n_ctx = bc * tc_
    m = n_ctx + bx * tx
    past = cache_k.shape[2]
    qk_dim = DN_K_HEADS * DN_HEAD_DIM
    v_dim = DN_V_HEADS * DN_HEAD_DIM
    conv_dim = 2 * qk_dim + v_dim
    kv_dim = ATT_KV_HEADS * ATT_HEAD_DIM
    q_dim = ATT_Q_HEADS * ATT_HEAD_DIM

    x = jnp.concatenate([x_prompt.reshape(n_ctx, d), x_sample.reshape(bx * tx, d)], axis=0)
    n_groups = 16
    cvec = jnp.zeros((n_groups, d), F32).at[0].set(c_ctx).at[1:1 + bx].set(c)
    mods = adaln_all(cvec, w_mod, b_mod)
    cos, sin = rope_tables(tx // GRID_W, ATT_HEAD_DIM)
    ba_idx = _dn_ba_layout(DN_HEADS_PER_STEP)
    n_hg = DN_V_HEADS // DN_HEADS_PER_STEP

    new_dn, new_k, new_v = [], [], []
    for i in range(depth):
        j = i // 2
        modtab = mods[i].reshape(n_groups * 6, 1, d)
        h = norm_mod(x, norm_mix[i], modtab, (0, 1), n_ctx, tx, BF16)
        if i % 2 == 0:
            w_in = dn_w_in[j]
            proj = mm(h, w_in[:, :conv_dim + v_dim].astype(BF16), F32)
            ba = mm(h, _permute_cols(w_in[:, conv_dim + v_dim:], ba_idx).astype(BF16), F32)
            old = jnp.zeros((2, 2, DN_V_HEADS), F32)
            nega_old = old.at[:, 1].set(-jnp.exp(dn_A_log[j].astype(F32))).reshape(-1)
            dtb_old = old.at[:, 1].set(dn_dt_bias[j].astype(F32)).reshape(-1)
            nega = _permute_cols(nega_old, ba_idx).reshape(n_hg, 1, LANES)
            dtb = _permute_cols(dtb_old, ba_idx).reshape(n_hg, 1, LANES)
            qkvc = dn_short_conv(proj, dn_conv[j], n_ctx=n_ctx, l_ctx=tc_, l_lat=tx,
                                 qk_dim=qk_dim, conv_dim=conv_dim, dk=DN_HEAD_DIM)
            o_f, o_b, s_out = dn_scan(qkvc, ba, nega, dtb, state_dn[:, j].astype(F32),
                                      [(bc, tc_, False), (bx, tx, True)],
                                      qk_dim=qk_dim, n_vheads=DN_V_HEADS)
            new_dn.append(s_out[:bc])
            og = dn_gate(o_f, o_b, proj, dn_out_norm[j], conv_dim)
            te, nv = _one_expert_tables(m, 512)
            x = mm_k(og, dn_w_out[j].astype(BF16)[None], te, nv, 512, 1024, v_dim,
                     res=x, modtab=modtab, gate_row=2, n_ctx=n_ctx, t_lat=tx)
        else:
            qkv = mm(h, att_w_in[j].astype(BF16), F32)
            q, k, v, kf = attn_prep(qkv, cos, sin, att_q_norm[j], att_k_norm[j], n_ctx, tx,
                                    ATT_Q_HEADS, ATT_KV_HEADS, ATT_HEAD_DIM)
            o_ctx = attention(q, k, v, None, batch=bc, seq=tc_, row0=0,
                              hq=ATT_Q_HEADS, hkv=ATT_KV_HEADS, dh=ATT_HEAD_DIM, tq=tc_)
            ck = cache_k[:, j].reshape(bx * past, kv_dim).astype(BF16)
            cv = cache_v[:, j].reshape(bx * past, kv_dim).astype(BF16)
            o_lat = attention(q, k, v, (ck, cv), batch=bx, seq=tx, row0=n_ctx,
                              hq=ATT_Q_HEADS, hkv=ATT_KV_HEADS, dh=ATT_HEAD_DIM, tq=256)
            o = jnp.concatenate([o_ctx, o_lat], axis=0)
            new_k.append(kf[:n_ctx].reshape(bc, tc_, ATT_KV_HEADS, ATT_HEAD_DIM))
            new_v.append(qkv[:n_ctx, q_dim + kv_dim:].reshape(bc, tc_, ATT_KV_HEADS, ATT_HEAD_DIM))
            te, nv = _one_expert_tables(m, 512)
            x = mm_k(o, att_w_out[j].astype(BF16)[None], te, nv, 512, 1024, q_dim,
                     res=x, modtab=modtab, gate_row=2, n_ctx=n_ctx, t_lat=tx)
        if i % 2 == 0:
            h = norm_mod(x, norm_ffn[i], modtab, (3, 4), n_ctx, tx, BF16)
            f = ffn_w_gate.shape[2]
            te, nv = _one_expert_tables(m, 1024)
            act = mm_swiglu(h, ffn_w_gate[j].astype(BF16)[None], ffn_w_up[j].astype(BF16)[None],
                            te, nv, tm=1024, tn=512)
            te, nv = _one_expert_tables(m, 512)
            x = mm_k(act, ffn_w_down[j].astype(BF16)[None], te, nv, 512, 1024, f // 2,
                     res=x, modtab=modtab, gate_row=5, n_ctx=n_ctx, t_lat=tx)
        else:
            h = norm_mod(x, norm_ffn[i], modtab, (3, 4), n_ctx, tx, F32)
            x = moe_layer(h, x, moe_router[j], moe_w_gate, moe_w_up, j,
                          moe_w_down[j].astype(BF16),
                          modtab, 5, n_ctx, tx)
    y = norm_mod(x, norm_final, None, (), n_ctx, tx, F32)
    y_prompt = y[:n_ctx].reshape(bc, tc_, d)
    y_sample = y[n_ctx:].reshape(bx, tx, d)
    return (y_prompt, y_sample, jnp.stack(new_dn, axis=1),
            jnp.stack(new_k, axis=1), jnp.stack(new_v, axis=1))
```

```python
import functools
import math

import numpy as np
import jax
import jax.numpy as jnp
from jax import lax
from jax.experimental import pallas as pl
from jax.experimental.pallas import tpu as pltpu

F32 = jnp.float32
BF16 = jnp.bfloat16
EPS = 1e-6

GRID_W = 64
ROPE_THETA = 10000.0
DN_K_HEADS = 16
DN_V_HEADS = 32
DN_HEAD_DIM = 128
DN_CHUNK = 64
ATT_Q_HEADS = 16
ATT_KV_HEADS = 2
ATT_HEAD_DIM = 256
N_EXPERTS = 8
TOP_K = 2

LANES = 128
VMEM_LIMIT_BYTES = 56 * 1024 * 1024

DN_HEADS_PER_STEP = 16


def _cp(*sem):
    return pltpu.CompilerParams(dimension_semantics=sem, vmem_limit_bytes=VMEM_LIMIT_BYTES)


def _silu(x):
    return x * jax.nn.sigmoid(x)


def _group_of_tile(i, tm, n_ctx, t_lat):
    row = i * tm
    return jnp.where(row < n_ctx, 0, 1 + (row - n_ctx) // t_lat)


def _adaln_kernel(c_ref, w_ref, b_ref, o_ref):
    s = _silu(c_ref[...]).astype(BF16)
    o_ref[0] = jnp.dot(s, w_ref[0].astype(BF16), preferred_element_type=F32) + b_ref[0]


def adaln_all(cvec, w_mod, b_mod, tn=1024):
    n_layers, d, n = w_mod.shape
    g = cvec.shape[0]
    return pl.pallas_call(
        _adaln_kernel,
        grid=(n_layers, n // tn),
        in_specs=[
            pl.BlockSpec((g, d), lambda l, j: (0, 0)),
            pl.BlockSpec((1, d, tn), lambda l, j: (l, 0, j)),
            pl.BlockSpec((1, 1, tn), lambda l, j: (l, 0, j)),
        ],
        out_specs=pl.BlockSpec((1, g, tn), lambda l, j: (l, 0, j)),
        out_shape=jax.ShapeDtypeStruct((n_layers, g, n), F32),
        compiler_params=_cp("parallel", "parallel"),
        name="adaln",
    )(cvec, w_mod, b_mod.reshape(n_layers, 1, n))


def _norm_mod_kernel(x_ref, w_ref, *rest, modulated):
    x = x_ref[...]
    y = x * lax.rsqrt(jnp.mean(x * x, axis=-1, keepdims=True) + EPS) * w_ref[...]
    if modulated:
        sh_ref, sc_ref, o_ref = rest
        y = y * (1.0 + sc_ref[0]) + sh_ref[0]
    else:
        (o_ref,) = rest
    o_ref[...] = y.astype(o_ref.dtype)


def norm_mod(x, w, modtab, rows, n_ctx, t_lat, out_dtype, tm=512):
    m, d = x.shape
    in_specs = [pl.BlockSpec((tm, d), lambda i: (i, 0)), pl.BlockSpec((1, d), lambda i: (0, 0))]
    args = [x, w.reshape(1, d)]
    if modtab is not None:
        for r in rows:
            in_specs.append(pl.BlockSpec(
                (1, 1, d), lambda i, r=r: (_group_of_tile(i, tm, n_ctx, t_lat) * 6 + r, 0, 0)))
            args.append(modtab)
    return pl.pallas_call(
        functools.partial(_norm_mod_kernel, modulated=modtab is not None),
        grid=(m // tm,),
        in_specs=in_specs,
        out_specs=pl.BlockSpec((tm, d), lambda i: (i, 0)),
        out_shape=jax.ShapeDtypeStruct((m, d), out_dtype),
        compiler_params=_cp("parallel"),
        name="norm_mod",
    )(*args)


def _mm_kernel(x_ref, w_ref, o_ref):
    o_ref[...] = jnp.dot(x_ref[...], w_ref[...], preferred_element_type=F32).astype(o_ref.dtype)


def mm(x, w, out_dtype, tm=1024, tn=1024):
    m, k = x.shape
    n = w.shape[1]
    tm, tn = min(tm, m), min(tn, n)
    return pl.pallas_call(
        _mm_kernel,
        grid=(m // tm, n // tn),
        in_specs=[pl.BlockSpec((tm, k), lambda i, j: (i, 0)),
                  pl.BlockSpec((k, tn), lambda i, j: (0, j))],
        out_specs=pl.BlockSpec((tm, tn), lambda i, j: (i, j)),
        out_shape=jax.ShapeDtypeStruct((m, n), out_dtype),
        compiler_params=_cp("parallel", "parallel"),
        name="mm",
    )(x, w)


def _swiglu_kernel(te_ref, nv_ref, x_ref, wg_ref, wu_ref, o_ref):
    i = pl.program_id(0)

    @pl.when(i < nv_ref[0])
    def _():
        x = x_ref[...].astype(BF16)
        g = jnp.dot(x, wg_ref[0], preferred_element_type=F32)
        u = jnp.dot(x, wu_ref[0], preferred_element_type=F32)
        o_ref[...] = (_silu(g) * u).astype(o_ref.dtype)

    @pl.when(i >= nv_ref[0])
    def _():
        o_ref[...] = jnp.zeros_like(o_ref)


def mm_swiglu(x, wg, wu, tile_expert, n_valid, tm, tn):
    m, k = x.shape
    f = wg.shape[2]

    def xmap(i, j, te, nv):
        return (jnp.minimum(i, nv[0] - 1), 0)

    def wmap(i, j, te, nv):
        return (te[jnp.minimum(i, nv[0] - 1)], 0, j)

    return pl.pallas_call(
        _swiglu_kernel,
        grid_spec=pltpu.PrefetchScalarGridSpec(
            num_scalar_prefetch=2,
            grid=(m // tm, f // tn),
            in_specs=[pl.BlockSpec((tm, k), xmap),
                      pl.BlockSpec((1, k, tn), wmap),
                      pl.BlockSpec((1, k, tn), wmap)],
            out_specs=pl.BlockSpec((tm, tn), lambda i, j, te, nv: (i, j)),
        ),
        out_shape=jax.ShapeDtypeStruct((m, f), BF16),
        compiler_params=_cp("parallel", "arbitrary"),
        name="mm_swiglu",
    )(tile_expert, n_valid, x, wg, wu)


def _moe_swiglu_kernel(te_ref, nv_ref, x_ref, wg_ref, wu_ref, o_ref, wgb_ref, wub_ref):
    i = pl.program_id(1)
    valid = i < nv_ref[0]
    changed = (i == 0) | (te_ref[i] != te_ref[jnp.maximum(i, 1) - 1])

    @pl.when(valid & changed)
    def _():
        wgb_ref[...] = wg_ref[0, 0].astype(BF16)
        wub_ref[...] = wu_ref[0, 0].astype(BF16)

    @pl.when(valid)
    def _():
        x = x_ref[...]
        g = jnp.dot(x, wgb_ref[...], preferred_element_type=F32)
        u = jnp.dot(x, wub_ref[...], preferred_element_type=F32)
        o_ref[...] = (_silu(g) * u).astype(o_ref.dtype)

    @pl.when(jnp.logical_not(valid))
    def _():
        o_ref[...] = jnp.zeros_like(o_ref)


def moe_swiglu(x, wg, wu, layer, tile_expert, n_valid, tm, tn):
    m, k = x.shape
    f = wg.shape[3]

    def xmap(j, i, te, nv):
        return (jnp.minimum(i, nv[0] - 1), 0)

    def wmap(j, i, te, nv):
        return (layer, te[jnp.minimum(i, nv[0] - 1)], 0, j)

    return pl.pallas_call(
        _moe_swiglu_kernel,
        grid_spec=pltpu.PrefetchScalarGridSpec(
            num_scalar_prefetch=2,
            grid=(f // tn, m // tm),
            in_specs=[pl.BlockSpec((tm, k), xmap),
                      pl.BlockSpec((1, 1, k, tn), wmap),
                      pl.BlockSpec((1, 1, k, tn), wmap)],
            out_specs=pl.BlockSpec((tm, tn), lambda j, i, te, nv: (i, j)),
            scratch_shapes=[pltpu.VMEM((k, tn), BF16), pltpu.VMEM((k, tn), BF16)],
        ),
        out_shape=jax.ShapeDtypeStruct((m, f), BF16),
        compiler_params=_cp("arbitrary", "arbitrary"),
        name="moe_swiglu",
    )(tile_expert, n_valid, x, wg, wu)


def _moe_down_kernel(te_ref, nv_ref, x_ref, w_ref, o_ref):
    valid = pl.program_id(1) < nv_ref[0]

    @pl.when(valid)
    def _():
        o_ref[...] = jnp.dot(x_ref[...], w_ref[0], preferred_element_type=F32)

    @pl.when(jnp.logical_not(valid))
    def _():
        o_ref[...] = jnp.zeros_like(o_ref)


def moe_down(x, w, tile_expert, n_valid, tm, tn):
    m, k = x.shape
    n = w.shape[2]

    def xmap(j, i, te, nv):
        return (jnp.minimum(i, nv[0] - 1), 0)

    def wmap(j, i, te, nv):
        return (te[jnp.minimum(i, nv[0] - 1)], 0, j)

    return pl.pallas_call(
        _moe_down_kernel,
        grid_spec=pltpu.PrefetchScalarGridSpec(
            num_scalar_prefetch=2,
            grid=(n // tn, m // tm),
            in_specs=[pl.BlockSpec((tm, k), xmap), pl.BlockSpec((1, k, tn), wmap)],
            out_specs=pl.BlockSpec((tm, tn), lambda j, i, te, nv: (i, j)),
        ),
        out_shape=jax.ShapeDtypeStruct((m, n), F32),
        compiler_params=_cp("arbitrary", "arbitrary"),
        name="moe_down",
    )(tile_expert, n_valid, x, w)


def _mmk_kernel(te_ref, nv_ref, x_ref, w_ref, *rest, nk, has_res):
    if has_res:
        res_ref, g_ref, o_ref = rest
    else:
        (o_ref,) = rest
    i = pl.program_id(0)
    k = pl.program_id(2)

    def finish(acc):
        if has_res:
            return res_ref[...] + g_ref[0] * acc
        return acc

    @pl.when(i < nv_ref[0])
    def _():
        part = jnp.dot(x_ref[...].astype(BF16), w_ref[0], preferred_element_type=F32)
        if nk == 1:
            o_ref[...] = finish(part)
        else:
            @pl.when(k == 0)
            def _():
                o_ref[...] = part

            if nk > 2:
                @pl.when((k > 0) & (k < nk - 1))
                def _():
                    o_ref[...] += part

            @pl.when(k == nk - 1)
            def _():
                o_ref[...] = finish(o_ref[...] + part)

    @pl.when(i >= nv_ref[0])
    def _():
        o_ref[...] = jnp.zeros_like(o_ref)


def mm_k(x, w, tile_expert, n_valid, tm, tn, tk, res=None, modtab=None, gate_row=None,
         n_ctx=0, t_lat=1):
    m, k = x.shape
    n = w.shape[2]
    nk = k // tk

    def xmap(i, j, kk, te, nv):
        return (jnp.minimum(i, nv[0] - 1), kk)

    def wmap(i, j, kk, te, nv):
        return (te[jnp.minimum(i, nv[0] - 1)], kk, j)

    in_specs = [pl.BlockSpec((tm, tk), xmap), pl.BlockSpec((1, tk, tn), wmap)]
    args = [x, w]
    if res is not None:
        in_specs.append(pl.BlockSpec((tm, tn), lambda i, j, kk, te, nv: (i, j)))
        in_specs.append(pl.BlockSpec(
            (1, 1, tn),
            lambda i, j, kk, te, nv: (_group_of_tile(i, tm, n_ctx, t_lat) * 6 + gate_row, 0, j)))
        args += [res, modtab]
    return pl.pallas_call(
        functools.partial(_mmk_kernel, nk=nk, has_res=res is not None),
        grid_spec=pltpu.PrefetchScalarGridSpec(
            num_scalar_prefetch=2,
            grid=(m // tm, n // tn, nk),
            in_specs=in_specs,
            out_specs=pl.BlockSpec((tm, tn), lambda i, j, kk, te, nv: (i, j)),
        ),
        out_shape=jax.ShapeDtypeStruct((m, n), F32),
        compiler_params=_cp("parallel", "parallel", "arbitrary"),
        name="mm_k",
    )(tile_expert, n_valid, *args)


def _one_expert_tables(m, tm):
    return jnp.zeros((m // tm,), jnp.int32), jnp.full((1,), m // tm, jnp.int32)


def rope_tables(rows, head_dim):
    n_freq = head_dim // 4
    inv = ROPE_THETA ** (-jnp.arange(n_freq, dtype=F32) / n_freq)
    r = jnp.repeat(jnp.arange(rows, dtype=F32), GRID_W)
    cl = jnp.tile(jnp.arange(GRID_W, dtype=F32), rows)
    ang = jnp.concatenate([r[:, None] * inv, cl[:, None] * inv], axis=-1)
    return jnp.cos(ang), jnp.sin(ang)


def _attn_prep_kernel(x_ref, cos_ref, sin_ref, qn_ref, kn_ref, q_ref, k_ref, v_ref, kf_ref,
                      *, hq, hkv, dh, n_ctx, tm):
    is_lat = pl.program_id(0) * tm >= n_ctx
    c = jnp.where(is_lat, cos_ref[...], 1.0)
    s = jnp.where(is_lat, sin_ref[...], 0.0)
    half = dh // 2
    for h in range(hq + hkv):
        xh = x_ref[:, h * dh:(h + 1) * dh]
        w = qn_ref[...] if h < hq else kn_ref[...]
        y = xh * lax.rsqrt(jnp.mean(xh * xh, axis=-1, keepdims=True) + EPS) * w
        y1, y2 = y[:, :half], y[:, half:]
        o1 = y1 * c - y2 * s
        o2 = y2 * c + y1 * s
        if h < hq:
            scale = dh ** -0.5
            q_ref[:, h * dh:h * dh + half] = (o1 * scale).astype(BF16)
            q_ref[:, h * dh + half:(h + 1) * dh] = (o2 * scale).astype(BF16)
        else:
            b = (h - hq) * dh
            k_ref[:, b:b + half] = o1.astype(BF16)
            k_ref[:, b + half:b + dh] = o2.astype(BF16)
            kf_ref[:, b:b + half] = o1
            kf_ref[:, b + half:b + dh] = o2
    v_ref[...] = x_ref[:, (hq + hkv) * dh:].astype(BF16)


def attn_prep(qkv, cos, sin, q_norm, k_norm, n_ctx, t_lat, hq, hkv, dh, tm=256):
    m = qkv.shape[0]
    half = dh // 2

    def posmap(i):
        row = i * tm
        return (jnp.where(row >= n_ctx, ((row - n_ctx) % t_lat) // tm, 0), 0)

    return pl.pallas_call(
        functools.partial(_attn_prep_kernel, hq=hq, hkv=hkv, dh=dh, n_ctx=n_ctx, tm=tm),
        grid=(m // tm,),
        in_specs=[pl.BlockSpec((tm, (hq + 2 * hkv) * dh), lambda i: (i, 0)),
                  pl.BlockSpec((tm, half), posmap),
                  pl.BlockSpec((tm, half), posmap),
                  pl.BlockSpec((1, dh), lambda i: (0, 0)),
                  pl.BlockSpec((1, dh), lambda i: (0, 0))],
        out_specs=[pl.BlockSpec((tm, hq * dh), lambda i: (i, 0)),
                   pl.BlockSpec((tm, hkv * dh), lambda i: (i, 0)),
                   pl.BlockSpec((tm, hkv * dh), lambda i: (i, 0)),
                   pl.BlockSpec((tm, hkv * dh), lambda i: (i, 0))],
        out_shape=[jax.ShapeDtypeStruct((m, hq * dh), BF16),
                   jax.ShapeDtypeStruct((m, hkv * dh), BF16),
                   jax.ShapeDtypeStruct((m, hkv * dh), BF16),
                   jax.ShapeDtypeStruct((m, hkv * dh), F32)],
        compiler_params=_cp("parallel"),
        name="attn_prep",
    )(qkv, cos, sin, q_norm.reshape(1, dh), k_norm.reshape(1, dh))


def _attn_kernel(*refs, n_src, n_group, dh):
    q_ref = refs[0]
    kv = refs[1:1 + 2 * n_src]
    o_ref = refs[-1]
    nt = (((1,), (1,)), ((), ()))
    for g in range(n_group):
        qg = q_ref[:, g * dh:(g + 1) * dh]
        ss = [lax.dot_general(qg, kv[2 * i][...], nt, preferred_element_type=F32)
              for i in range(n_src)]
        mx = functools.reduce(jnp.maximum, [jnp.max(s, axis=-1, keepdims=True) for s in ss])
        ps = [jnp.exp(s - mx) for s in ss]
        den = functools.reduce(jnp.add, [jnp.sum(p, axis=-1, keepdims=True) for p in ps])
        o = functools.reduce(jnp.add, [
            jnp.dot(p.astype(BF16), kv[2 * i + 1][...], preferred_element_type=F32)
            for i, p in enumerate(ps)])
        o_ref[:, g * dh:(g + 1) * dh] = (o / den).astype(o_ref.dtype)


def attention(q, k, v, cache, *, batch, seq, row0, hq, hkv, dh, tq):
    n_group = hq // hkv
    nq = seq // tq
    qb0, kb0 = row0 // tq, row0 // seq
    in_specs = [pl.BlockSpec((tq, n_group * dh), lambda b, h, t: (qb0 + b * nq + t, h))]
    args = [q]
    if cache is not None:
        p_len = cache[0].shape[0] // batch
        in_specs += [pl.BlockSpec((p_len, dh), lambda b, h, t: (b, h))] * 2
        args += list(cache)
    in_specs += [pl.BlockSpec((seq, dh), lambda b, h, t: (kb0 + b, h))] * 2
    args += [k, v]
    n_src = (len(args) - 1) // 2
    return pl.pallas_call(
        functools.partial(_attn_kernel, n_src=n_src, n_group=n_group, dh=dh),
        grid=(batch, hkv, nq),
        in_specs=in_specs,
        out_specs=pl.BlockSpec((tq, n_group * dh), lambda b, h, t: (b * nq + t, h)),
        out_shape=jax.ShapeDtypeStruct((batch * seq, hq * dh), BF16),
        compiler_params=_cp("parallel", "parallel", "arbitrary"),
        name="attention",
    )(*args)


def _dn_conv_kernel(x_ref, w_ref, o_ref, pad_ref, *, t, tc, n_ctx_blocks, l_ctx, l_lat,
                    n_q_tiles, n_qk_tiles, dk):
    j = pl.program_id(1)
    zeros8 = jnp.zeros((8, tc), F32)
    pad_ref[0:8, :] = zeros8
    pad_ref[t + 8:t + 16, :] = zeros8
    pad_ref[8:t + 8, :] = x_ref[...]
    w = w_ref[...]
    seq_len = jnp.where(pl.program_id(0) < n_ctx_blocks, l_ctx, l_lat)
    pos = lax.broadcasted_iota(jnp.int32, (t, tc), 0) & (seq_len - 1)
    y = (w[0:1] * jnp.where(pos >= 1, pad_ref[7:t + 7, :], 0.0)
         + w[1:2] * pad_ref[8:t + 8, :]
         + w[2:3] * jnp.where(pos <= seq_len - 2, pad_ref[9:t + 9, :], 0.0)
         + w[3:4] * jnp.where(pos <= seq_len - 3, pad_ref[10:t + 10, :], 0.0))
    y = _silu(y)
    is_qk = j < n_qk_tiles
    q_scale = jnp.where(j < n_q_tiles, dk ** -0.5, 1.0)
    for h in range(tc // LANES):
        yh = y[:, h * LANES:(h + 1) * LANES]
        ss = jnp.sum(yh * yh, axis=-1, keepdims=True)
        inv = jnp.where(is_qk, lax.rsqrt(ss + EPS) * q_scale, 1.0)
        o_ref[:, h * LANES:(h + 1) * LANES] = (yh * inv).astype(o_ref.dtype)


def dn_short_conv(proj, conv_w, *, n_ctx, l_ctx, l_lat, qk_dim, conv_dim, dk, tc=512):
    m = proj.shape[0]
    assert l_lat % l_ctx == 0 and n_ctx % l_lat == 0 and m % l_lat == 0
    assert l_ctx & (l_ctx - 1) == 0 and l_lat & (l_lat - 1) == 0 and l_ctx >= 4
    return pl.pallas_call(
        functools.partial(_dn_conv_kernel, t=l_lat, tc=tc, n_ctx_blocks=n_ctx // l_lat,
                          l_ctx=l_ctx, l_lat=l_lat, n_q_tiles=qk_dim // tc,
                          n_qk_tiles=2 * qk_dim // tc, dk=dk),
        grid=(m // l_lat, conv_dim // tc),
        in_specs=[pl.BlockSpec((l_lat, tc), lambda b, j: (b, j)),
                  pl.BlockSpec((conv_w.shape[0], tc), lambda b, j: (0, j))],
        out_specs=pl.BlockSpec((l_lat, tc), lambda b, j: (b, j)),
        out_shape=jax.ShapeDtypeStruct((m, conv_dim), BF16),
        scratch_shapes=[pltpu.VMEM((l_lat + 16, tc), F32)],
        compiler_params=_cp("parallel", "parallel"),
        name="dn_conv",
    )(proj, conv_w)


def _softplus(x):
    return jnp.maximum(x, 0.0) + jnp.log1p(jnp.exp(-jnp.abs(x)))


def _pair_masks(n):
    r = lax.broadcasted_iota(jnp.int32, (n, n), 0)
    c = lax.broadcasted_iota(jnp.int32, (n, n), 1)
    return [((r >> (k + 1)) == (c >> (k + 1))) & ((r >> k) != (c >> k))
            for k in range(int(math.log2(n)))]


def _unit_tri_inverses(lows, eye, masks):
    ds = [eye - jnp.where(masks[0], low, 0.0) for low in lows]
    lows_b = [low.astype(BF16) for low in lows]
    zero = jnp.zeros((), BF16)
    for mask in masks[1:]:
        dbs = [d.astype(BF16) for d in ds]
        ts = [jnp.dot(db, jnp.where(mask, lb, zero), preferred_element_type=F32)
              for db, lb in zip(dbs, lows_b)]
        ds = [d - jnp.dot(t.astype(BF16), db, preferred_element_type=F32)
              for d, t, db in zip(ds, ts, dbs)]
    return ds


def _dn_scan_kernel(rowf_ref, rowb_ref, flag_ref, seq_ref,
                    qf_ref, kf_ref, vf_ref, qb_ref, kb_ref, vb_ref, baf_ref, bab_ref,
                    nega_ref, dtb_ref, s0_ref, of_ref, ob_ref, sout_ref, s_scr, *, hb, chunk):
    step = pl.program_id(1)
    flag = flag_ref[step]
    first = (flag & 1) == 1
    last = (flag & 2) == 2
    is_lat = (flag & 4) == 4
    dh = DN_HEAD_DIM
    nt = (((1,), (1,)), ((), ()))

    @pl.when(first)
    def _():
        s_scr[...] = jnp.where(is_lat, s0_ref[0], 0.0)

    r = lax.broadcasted_iota(jnp.int32, (chunk, chunk), 0)
    c = lax.broadcasted_iota(jnp.int32, (chunk, chunk), 1)
    eye = (r == c).astype(F32)
    masks = _pair_masks(chunk)

    heads = []
    lows = []
    for d, q_ref, k_ref, v_ref, ba_ref, o_ref in ((0, qf_ref, kf_ref, vf_ref, baf_ref, of_ref),
                                                   (1, qb_ref, kb_ref, vb_ref, bab_ref, ob_ref)):
        incl = (r >= c) if d == 0 else (r <= c)
        strict = (r > c) if d == 0 else (r < c)
        last_row = chunk - 1 if d == 0 else 0
        ba = ba_ref[...]
        sig = jax.nn.sigmoid(ba)
        g = nega_ref[0] * _softplus(ba + dtb_ref[0])
        gc = jnp.dot(incl.astype(F32), g, preferred_element_type=F32,
                     precision=lax.Precision.HIGHEST)
        gct = gc.T
        egc = jnp.exp(gc)
        glast = gc[last_row:last_row + 1, :]
        eglast = jnp.exp(glast)
        ekg = jnp.exp(glast - gc)
        for kh in range(hb // 2):
            qh = q_ref[:, kh * dh:(kh + 1) * dh]
            kk_ = k_ref[:, kh * dh:(kh + 1) * dh]
            kf32 = kk_.astype(F32)
            qf32 = qh.astype(F32)
            kkt = lax.dot_general(kk_, kk_, nt, preferred_element_type=F32)
            qkt = lax.dot_general(qh, kk_, nt, preferred_element_type=F32)
            for rr in range(2):
                hh = kh * 2 + rr
                cb = d * 2 * hb + hh
                ca = d * 2 * hb + hb + hh
                bcol = sig[:, cb:cb + 1]
                ecol = egc[:, ca:ca + 1]
                decay = jnp.exp(jnp.where(incl, gc[:, ca:ca + 1] - gct[ca:ca + 1, :], -jnp.inf))
                lows.append(jnp.where(strict, bcol * kkt * decay, 0.0))
                heads.append(dict(
                    d=d, hh=hh, o_ref=o_ref,
                    vbm=(v_ref[:, hh * dh:(hh + 1) * dh].astype(F32) * bcol).astype(BF16),
                    kbg=(kf32 * (bcol * ecol)).astype(BF16),
                    a=jnp.where(incl, qkt * decay, 0.0).astype(BF16),
                    qg=(qf32 * ecol).astype(BF16),
                    kgt=(kf32 * ekg[:, ca:ca + 1]).T.astype(BF16),
                    egl=eglast[:, ca:ca + 1]))

    tinvs = [t.astype(BF16) for t in _unit_tri_inverses(lows, eye, masks)]
    uws = [jnp.dot(t, jnp.concatenate([h["vbm"], h["kbg"]], axis=1), preferred_element_type=F32)
           for t, h in zip(tinvs, heads)]
    ss = [s_scr[h["d"], h["hh"]] for h in heads]
    wqs = [jnp.dot(jnp.concatenate([uw[:, dh:].astype(BF16), h["qg"]], axis=0), s.astype(BF16),
                   preferred_element_type=F32)
           for uw, h, s in zip(uws, heads, ss)]
    vnbs = [(uw[:, :dh] - wq[:chunk]).astype(BF16) for uw, wq in zip(uws, wqs)]
    avs = [jnp.dot(jnp.concatenate([h["a"], h["kgt"]], axis=0), vnb, preferred_element_type=F32)
           for h, vnb in zip(heads, vnbs)]
    for h, wq, av in zip(heads, wqs, avs):
        hh = h["hh"]
        h["o_ref"][:, hh * dh:(hh + 1) * dh] = wq[chunk:] + av[:chunk]
    for h, s, av in zip(heads, ss, avs):
        s_scr[h["d"], h["hh"]] = s * h["egl"] + av[chunk:]

    @pl.when(last)
    def _():
        sout_ref[0] = s_scr[...]


def dn_scan(qkvc, ba, nega, dtb, s0, seq_lens, *, qk_dim, n_vheads, hb=DN_HEADS_PER_STEP):
    m = qkvc.shape[0]
    chunk = DN_CHUNK
    dh = DN_HEAD_DIM
    n_hg = n_vheads // hb
    rowf, rowb, flags, seqs = [], [], [], []
    row, sid, n_zero_seq = 0, 0, 0
    for n_seq, length, uses_s0 in seq_lens:
        n_chunks = length // chunk
        for _ in range(n_seq):
            for n in range(n_chunks):
                rowf.append(row + n)
                rowb.append(row + n_chunks - 1 - n)
                flags.append((n == 0) * 1 + (n == n_chunks - 1) * 2 + (4 if uses_s0 else 0))
                seqs.append(sid)
            row += n_chunks
            sid += 1
        if not uses_s0:
            n_zero_seq += n_seq
    n_steps = len(rowf)
    n_seq_total = sid
    tabs = [jnp.asarray(np.array(t, np.int32)) for t in (rowf, rowb, flags, seqs)]
    qw, vw = (hb // 2) * dh, hb * dh
    kb0, vb0 = qk_dim // qw, 2 * qk_dim // vw

    def spec(width, col0, rows_idx):
        return pl.BlockSpec((chunk, width),
                            lambda hg, s, rf, rb, fl, sq: ((rf, rb)[rows_idx][s], col0 + hg))

    in_specs = [spec(qw, 0, 0), spec(qw, kb0, 0), spec(vw, vb0, 0),
                spec(qw, 0, 1), spec(qw, kb0, 1), spec(vw, vb0, 1),
                pl.BlockSpec((chunk, LANES), lambda hg, s, rf, rb, fl, sq: (rf[s], hg)),
                pl.BlockSpec((chunk, LANES), lambda hg, s, rf, rb, fl, sq: (rb[s], hg)),
                pl.BlockSpec((1, 1, LANES), lambda hg, s, rf, rb, fl, sq: (hg, 0, 0)),
                pl.BlockSpec((1, 1, LANES), lambda hg, s, rf, rb, fl, sq: (hg, 0, 0)),
                pl.BlockSpec((1, 2, hb, dh, dh),
                             lambda hg, s, rf, rb, fl, sq:
                             (jnp.maximum(sq[s] - n_zero_seq, 0), 0, hg, 0, 0))]
    out_specs = [pl.BlockSpec((chunk, vw), lambda hg, s, rf, rb, fl, sq: (rf[s], hg)),
                 pl.BlockSpec((chunk, vw), lambda hg, s, rf, rb, fl, sq: (rb[s], hg)),
                 pl.BlockSpec((1, 2, hb, dh, dh),
                              lambda hg, s, rf, rb, fl, sq: (sq[s], 0, hg, 0, 0))]
    return pl.pallas_call(
        functools.partial(_dn_scan_kernel, hb=hb, chunk=chunk),
        grid_spec=pltpu.PrefetchScalarGridSpec(
            num_scalar_prefetch=4,
            grid=(n_hg, n_steps),
            in_specs=in_specs,
            out_specs=out_specs,
            scratch_shapes=[pltpu.VMEM((2, hb, dh, dh), F32)],
        ),
        out_shape=[jax.ShapeDtypeStruct((m, n_vheads * dh), F32),
                   jax.ShapeDtypeStruct((m, n_vheads * dh), F32),
                   jax.ShapeDtypeStruct((n_seq_total, 2, n_vheads, dh, dh), F32)],
        compiler_params=_cp("parallel", "arbitrary"),
        name="dn_scan",
    )(*tabs, qkvc, qkvc, qkvc, qkvc, qkvc, qkvc, ba, ba, nega, dtb, s0)


def _dn_gate_kernel(of_ref, ob_ref, z_ref, w_ref, o_ref, *, n_heads):
    w = w_ref[...]
    for h in range(n_heads):
        sl = slice(h * DN_HEAD_DIM, (h + 1) * DN_HEAD_DIM)
        o = of_ref[:, sl] + ob_ref[:, sl]
        y = o * lax.rsqrt(jnp.mean(o * o, axis=-1, keepdims=True) + EPS) * w
        o_ref[:, sl] = (y * _silu(z_ref[:, sl])).astype(o_ref.dtype)


def dn_gate(o_f, o_b, proj, out_norm, z_col0, tm=256, tn=1024):
    m, n = o_f.shape
    zb0 = z_col0 // tn
    return pl.pallas_call(
        functools.partial(_dn_gate_kernel, n_heads=tn // DN_HEAD_DIM),
        grid=(m // tm, n // tn),
        in_specs=[pl.BlockSpec((tm, tn), lambda i, j: (i, j)),
                  pl.BlockSpec((tm, tn), lambda i, j: (i, j)),
                  pl.BlockSpec((tm, tn), lambda i, j: (i, zb0 + j)),
                  pl.BlockSpec((1, DN_HEAD_DIM), lambda i, j: (0, 0))],
        out_specs=pl.BlockSpec((tm, tn), lambda i, j: (i, j)),
        out_shape=jax.ShapeDtypeStruct((m, n), BF16),
        compiler_params=_cp("parallel", "parallel"),
        name="dn_gate",
    )(o_f, o_b, proj, out_norm.reshape(1, DN_HEAD_DIM))


def _router_kernel(x_ref, whi_ref, wlo_ref, g_ref, s_ref, *, n_experts):
    x = x_ref[...]
    xhi = x.astype(BF16)
    xlo = (x - xhi.astype(F32)).astype(BF16)
    lg = (jnp.dot(xhi, whi_ref[...], preferred_element_type=F32)
          + jnp.dot(xlo, whi_ref[...], preferred_element_type=F32)
          + jnp.dot(xhi, wlo_ref[...], preferred_element_type=F32))
    lane = lax.broadcasted_iota(jnp.int32, lg.shape, 1).astype(F32)
    neg = -jnp.inf
    lg = jnp.where(lane < n_experts, lg, neg)
    m1 = jnp.max(lg, axis=-1, keepdims=True)
    i1 = jnp.min(jnp.where(lg == m1, lane, float(LANES)), axis=-1, keepdims=True)
    mk1 = lane == i1
    lg2 = jnp.where(mk1, neg, lg)
    m2 = jnp.max(lg2, axis=-1, keepdims=True)
    i2 = jnp.min(jnp.where(lg2 == m2, lane, float(LANES)), axis=-1, keepdims=True)
    mk2 = lane == i2
    e = jnp.exp(m2 - m1)
    w1 = 1.0 / (1.0 + e)
    g_ref[...] = jnp.where(mk1, w1, 0.0) + jnp.where(mk2, e * w1, 0.0)
    s_ref[...] = jnp.where(mk1, 1.0, 0.0) + jnp.where(mk2, 1.0, 0.0)


def router(x, w_router, tm=512):
    m, d = x.shape
    e = w_router.shape[1]
    wp = jnp.zeros((d, LANES), F32).at[:, :e].set(w_router)
    whi = wp.astype(BF16)
    wlo = (wp - whi.astype(F32)).astype(BF16)
    return pl.pallas_call(
        functools.partial(_router_kernel, n_experts=e),
        grid=(m // tm,),
        in_specs=[pl.BlockSpec((tm, d), lambda i: (i, 0)),
                  pl.BlockSpec((d, LANES), lambda i: (0, 0)),
                  pl.BlockSpec((d, LANES), lambda i: (0, 0))],
        out_specs=[pl.BlockSpec((tm, LANES), lambda i: (i, 0))] * 2,
        out_shape=[jax.ShapeDtypeStruct((m, LANES), F32)] * 2,
        compiler_params=_cp("parallel"),
        name="router",
    )(x, whi, wlo)


def _row_copy(src_hbm, row, dst_ref, r, sem):
    return pltpu.make_async_copy(src_hbm.at[pl.ds(row, 1), :], dst_ref.at[pl.ds(r, 1), :], sem)


def _gather_kernel(idx_ref, x_hbm, o_ref, buf, sem, *, tg):
    def start(r, carry):
        _row_copy(x_hbm, idx_ref[0, 0, r], buf, r, sem).start()
        return carry

    def wait(r, carry):
        _row_copy(x_hbm, idx_ref[0, 0, r], buf, r, sem).wait()
        return carry

    lax.fori_loop(0, tg, start, 0, unroll=8)
    lax.fori_loop(0, tg, wait, 0, unroll=8)
    o_ref[...] = buf[...].astype(o_ref.dtype)


def gather_rows(x, idx, out_dtype, tg=256):
    ms = idx.shape[0]
    d = x.shape[1]
    return pl.pallas_call(
        functools.partial(_gather_kernel, tg=tg),
        grid=(ms // tg,),
        in_specs=[pl.BlockSpec((1, 1, tg), lambda i: (i, 0, 0), memory_space=pltpu.SMEM),
                  pl.BlockSpec(memory_space=pl.ANY)],
        out_specs=pl.BlockSpec((tg, d), lambda i: (i, 0)),
        out_shape=jax.ShapeDtypeStruct((ms, d), out_dtype),
        scratch_shapes=[pltpu.VMEM((tg, d), x.dtype), pltpu.SemaphoreType.DMA(())],
        compiler_params=_cp("arbitrary"),
        name="gather_rows",
    )(idx.reshape(ms // tg, 1, tg), x)


def _combine_kernel(pos_ref, ys_hbm, g_ref, res_ref, gm_ref, o_ref, buf, sem, *, tc, n_sel):
    def start(r, carry):
        for j in range(n_sel):
            _row_copy(ys_hbm, pos_ref[0, 0, r * n_sel + j], buf.at[j], r, sem).start()
        return carry

    def wait(r, carry):
        for j in range(n_sel):
            _row_copy(ys_hbm, pos_ref[0, 0, r * n_sel + j], buf.at[j], r, sem).wait()
        return carry

    lax.fori_loop(0, tc, start, 0)
    lax.fori_loop(0, tc, wait, 0)
    acc = g_ref[:, 0:1] * buf[0]
    for j in range(1, n_sel):
        acc = acc + g_ref[:, j:j + 1] * buf[j]
    o_ref[...] = res_ref[...] + gm_ref[0] * acc


def moe_combine(ys, pos, gates, res, modtab, gate_row, n_ctx, t_lat, tc=128):
    m, d = res.shape
    n_sel = pos.shape[1]
    return pl.pallas_call(
        functools.partial(_combine_kernel, tc=tc, n_sel=n_sel),
        grid=(m // tc,),
        in_specs=[pl.BlockSpec((1, 1, tc * n_sel), lambda i: (i, 0, 0), memory_space=pltpu.SMEM),
                  pl.BlockSpec(memory_space=pl.ANY),
                  pl.BlockSpec((tc, n_sel), lambda i: (i, 0)),
                  pl.BlockSpec((tc, d), lambda i: (i, 0)),
                  pl.BlockSpec((1, 1, d),
                               lambda i: (_group_of_tile(i, tc, n_ctx, t_lat) * 6 + gate_row, 0, 0))],
        out_specs=pl.BlockSpec((tc, d), lambda i: (i, 0)),
        out_shape=jax.ShapeDtypeStruct((m, d), F32),
        scratch_shapes=[pltpu.VMEM((n_sel, tc, d), F32), pltpu.SemaphoreType.DMA(())],
        compiler_params=_cp("arbitrary"),
        name="moe_combine",
    )(pos.reshape(m // tc, 1, tc * n_sel), ys, gates, res, modtab)


def routing_tables(gates, sel, n_experts, tm):
    m = gates.shape[0]
    g = gates[:, :n_experts]
    pairs = sel[:, :n_experts] > 0.5
    pi = pairs.astype(jnp.int32)
    cnt = jnp.sum(pi, axis=0)
    padded = ((cnt + tm - 1) // tm) * tm
    ends = jnp.cumsum(padded)
    starts = ends - padded
    rank = jnp.cumsum(pi, axis=0) - pi
    pos = starts[None, :] + rank
    ms = TOP_K * m + n_experts * tm
    first = jnp.argmax(pairs, axis=1)
    second = n_experts - 1 - jnp.argmax(pairs[:, ::-1], axis=1)
    pick = jnp.stack([first, second], axis=1)
    pos_sel = jnp.take_along_axis(pos, pick, axis=1)
    g_sel = jnp.take_along_axis(g, pick, axis=1)
    tok = jnp.broadcast_to(jnp.arange(m, dtype=jnp.int32)[:, None], pos.shape)
    tok_sorted = jnp.zeros((ms,), jnp.int32).at[jnp.where(pairs, pos, ms).reshape(-1)].set(
        tok.reshape(-1), mode="drop")
    tile_start = jnp.arange(ms // tm, dtype=jnp.int32) * tm
    tile_expert = jnp.minimum(
        jnp.sum(tile_start[:, None] >= ends[None, :], axis=1), n_experts - 1).astype(jnp.int32)
    n_valid = (ends[-1] // tm).astype(jnp.int32).reshape(1)
    return tok_sorted, tile_expert, n_valid, pos_sel.astype(jnp.int32), g_sel


def moe_layer(h, x_res, w_router, wg, wu, layer, wd, modtab, gate_row, n_ctx, t_lat, tm=512):
    n_experts = wg.shape[1]
    f = wg.shape[3]
    gates, sel = router(h, w_router)
    tok_sorted, te, nv, pos_sel, g_sel = routing_tables(gates, sel, n_experts, tm)
    xs = gather_rows(h, tok_sorted, BF16)
    act = moe_swiglu(xs, wg, wu, layer, te, nv, tm=tm, tn=1024)
    ys = moe_down(act, wd, te, nv, tm=tm, tn=min(1024, wd.shape[2]))
    return moe_combine(ys, pos_sel, g_sel, x_res, modtab, gate_row, n_ctx, t_lat)


def _dn_ba_layout(hb):
    n_hg = DN_V_HEADS // hb
    idx = np.full((n_hg, LANES), -1, np.int64)
    for hg in range(n_hg):
        for d in range(2):
            for ab in range(2):
                for hh in range(hb):
                    idx[hg, d * 2 * hb + ab * hb + hh] = d * 2 * DN_V_HEADS + ab * DN_V_HEADS + hg * hb + hh
    return idx.reshape(-1)


def _permute_cols(a, idx):
    valid = jnp.asarray(idx >= 0)
    return jnp.where(valid, jnp.take(a, jnp.asarray(np.maximum(idx, 0)), axis=-1), 0.0)


def kernel(x_prompt, x_sample, state_dn, cache_k, cache_v, c, c_ctx, w_mod, b_mod, norm_mix, norm_ffn, norm_final, dn_w_in, dn_conv, dn_A_log, dn_dt_bias, dn_out_norm, dn_w_out, att_w_in, att_q_norm, att_k_norm, att_w_out, ffn_w_gate, ffn_w_up, ffn_w_down, moe_router, moe_w_gate, moe_w_up, moe_w_down):
    bc, tc_, d = x_prompt.shape
    bx, tx, _ = x_sample.shape
    depth = w_mod.shape[0]
    n_ctx = bc * tc_
    m = n_ctx + bx * tx
    past = cache_k.shape[2]
    qk_dim = DN_K_HEADS * DN_HEAD_DIM
    v_dim = DN_V_HEADS * DN_HEAD_DIM
    conv_dim = 2 * qk_dim + v_dim
    kv_dim = ATT_KV_HEADS * ATT_HEAD_DIM
    q_dim = ATT_Q_HEADS * ATT_HEAD_DIM

    x = jnp.concatenate([x_prompt.reshape(n_ctx, d), x_sample.reshape(bx * tx, d)], axis=0)
    n_groups = 16
    cvec = jnp.zeros((n_groups, d), F32).at[0].set(c_ctx).at[1:1 + bx].set(c)
    mods = adaln_all(cvec, w_mod, b_mod)
    cos, sin = rope_tables(tx // GRID_W, ATT_HEAD_DIM)
    ba_idx = _dn_ba_layout(DN_HEADS_PER_STEP)
    n_hg = DN_V_HEADS // DN_HEADS_PER_STEP

    new_dn, new_k, new_v = [], [], []
    for i in range(depth):
        j = i // 2
        modtab = mods[i].reshape(n_groups * 6, 1, d)
        h = norm_mod(x, norm_mix[i], modtab, (0, 1), n_ctx, tx, BF16)
        if i % 2 == 0:
            w_in = dn_w_in[j]
            proj = mm(h, w_in[:, :conv_dim + v_dim].astype(BF16), F32)
            ba = mm(h, _permute_cols(w_in[:, conv_dim + v_dim:], ba_idx).astype(BF16), F32)
            old = jnp.zeros((2, 2, DN_V_HEADS), F32)
            nega_old = old.at[:, 1].set(-jnp.exp(dn_A_log[j].astype(F32))).reshape(-1)
            dtb_old = old.at[:, 1].set(dn_dt_bias[j].astype(F32)).reshape(-1)
            nega = _permute_cols(nega_old, ba_idx).reshape(n_hg, 1, LANES)
            dtb = _permute_cols(dtb_old, ba_idx).reshape(n_hg, 1, LANES)
            qkvc = dn_short_conv(proj, dn_conv[j], n_ctx=n_ctx, l_ctx=tc_, l_lat=tx,
                                 qk_dim=qk_dim, conv_dim=conv_dim, dk=DN_HEAD_DIM)
            o_f, o_b, s_out = dn_scan(qkvc, ba, nega, dtb, state_dn[:, j].astype(F32),
                                      [(bc, tc_, False), (bx, tx, True)],
                                      qk_dim=qk_dim, n_vheads=DN_V_HEADS)
            new_dn.append(s_out[:bc])
            og = dn_gate(o_f, o_b, proj, dn_out_norm[j], conv_dim)
            te, nv = _one_expert_tables(m, 512)
            x = mm_k(og, dn_w_out[j].astype(BF16)[None], te, nv, 512, 1024, v_dim,
                     res=x, modtab=modtab, gate_row=2, n_ctx=n_ctx, t_lat=tx)
        else:
            qkv = mm(h, att_w_in[j].astype(BF16), F32)
            q, k, v, kf = attn_prep(qkv, cos, sin, att_q_norm[j], att_k_norm[j], n_ctx, tx,
                                    ATT_Q_HEADS, ATT_KV_HEADS, ATT_HEAD_DIM)
            o_ctx = attention(q, k, v, None, batch=bc, seq=tc_, row0=0,
                              hq=ATT_Q_HEADS, hkv=ATT_KV_HEADS, dh=ATT_HEAD_DIM, tq=tc_)
            ck = cache_k[:, j].reshape(bx * past, kv_dim).astype(BF16)
            cv = cache_v[:, j].reshape(bx * past, kv_dim).astype(BF16)
            o_lat = attention(q, k, v, (ck, cv), batch=bx, seq=tx, row0=n_ctx,
                              hq=ATT_Q_HEADS, hkv=ATT_KV_HEADS, dh=ATT_HEAD_DIM, tq=256)
            o = jnp.concatenate([o_ctx, o_lat], axis=0)
            new_k.append(kf[:n_ctx].reshape(bc, tc_, ATT_KV_HEADS, ATT_HEAD_DIM))
            new_v.append(qkv[:n_ctx, q_dim + kv_dim:].reshape(bc, tc_, ATT_KV_HEADS, ATT_HEAD_DIM))
            te, nv = _one_expert_tables(m, 512)
            x = mm_k(o, att_w_out[j].astype(BF16)[None], te, nv, 512, 1024, q_dim,
                     res=x, modtab=modtab, gate_row=2, n_ctx=n_ctx, t_lat=tx)
        if i % 2 == 0:
            h = norm_mod(x, norm_ffn[i], modtab, (3, 4), n_ctx, tx, BF16)
            f = ffn_w_gate.shape[2]
            te, nv = _one_expert_tables(m, 1024)
            act = mm_swiglu(h, ffn_w_gate[j].astype(BF16)[None], ffn_w_up[j].astype(BF16)[None],
                            te, nv, tm=1024, tn=512)
            te, nv = _one_expert_tables(m, 512)
            x = mm_k(act, ffn_w_down[j].astype(BF16)[None], te, nv, 512, 1024, f // 2,
                     res=x, modtab=modtab, gate_row=5, n_ctx=n_ctx, t_lat=tx)
        else:
            h = norm_mod(x, norm_ffn[i], modtab, (3, 4), n_ctx, tx, F32)
            x = moe_layer(h, x, moe_router[j], moe_w_gate, moe_w_up, j,
                          moe_w_down[j].astype(BF16),
                          modtab, 5, n_ctx, tx)
    y = norm_mod(x, norm_final, None, (), n_ctx, tx, F32)
    y_prompt = y[:n_ctx].reshape(bc, tc_, d)
    y_sample = y[n_ctx:].reshape(bx, tx, d)
    return (y_prompt, y_sample, jnp.stack(new_dn, axis=1),
            jnp.stack(new_k, axis=1), jnp.stack(new_v, axis=1))
```

```python
import functools
import math

import numpy as np
import jax
import jax.numpy as jnp
from jax import lax
from jax.experimental import pallas as pl
from jax.experimental.pallas import tpu as pltpu

F32 = jnp.float32
BF16 = jnp.bfloat16
EPS = 1e-6

GRID_W = 64
ROPE_THETA = 10000.0
DN_K_HEADS = 16
DN_V_HEADS = 32
DN_HEAD_DIM = 128
DN_CHUNK = 64
ATT_Q_HEADS = 16
ATT_KV_HEADS = 2
ATT_HEAD_DIM = 256
N_EXPERTS = 8
TOP_K = 2

LANES = 128
VMEM_LIMIT_BYTES = 56 * 1024 * 1024

DN_HEADS_PER_STEP = 16


def _cp(*sem):
    return pltpu.CompilerParams(dimension_semantics=sem, vmem_limit_bytes=VMEM_LIMIT_BYTES)


def _silu(x):
    return x * jax.nn.sigmoid(x)


def _group_of_tile(i, tm, n_ctx, t_lat):
    row = i * tm
    return jnp.where(row < n_ctx, 0, 1 + (row - n_ctx) // t_lat)


def _adaln_kernel(c_ref, w_ref, b_ref, o_ref):
    s = _silu(c_ref[...]).astype(BF16)
    o_ref[0] = jnp.dot(s, w_ref[0].astype(BF16), preferred_element_type=F32) + b_ref[0]


def adaln_all(cvec, w_mod, b_mod, tn=1024):
    n_layers, d, n = w_mod.shape
    g = cvec.shape[0]
    return pl.pallas_call(
        _adaln_kernel,
        grid=(n_layers, n // tn),
        in_specs=[
            pl.BlockSpec((g, d), lambda l, j: (0, 0)),
            pl.BlockSpec((1, d, tn), lambda l, j: (l, 0, j)),
            pl.BlockSpec((1, 1, tn), lambda l, j: (l, 0, j)),
        ],
        out_specs=pl.BlockSpec((1, g, tn), lambda l, j: (l, 0, j)),
        out_shape=jax.ShapeDtypeStruct((n_layers, g, n), F32),
        compiler_params=_cp("parallel", "parallel"),
        name="adaln",
    )(cvec, w_mod, b_mod.reshape(n_layers, 1, n))


def _norm_mod_kernel(x_ref, w_ref, *rest, modulated):
    x = x_ref[...]
    y = x * lax.rsqrt(jnp.mean(x * x, axis=-1, keepdims=True) + EPS) * w_ref[...]
    if modulated:
        sh_ref, sc_ref, o_ref = rest
        y = y * (1.0 + sc_ref[0]) + sh_ref[0]
    else:
        (o_ref,) = rest
    o_ref[...] = y.astype(o_ref.dtype)


def norm_mod(x, w, modtab, rows, n_ctx, t_lat, out_dtype, tm=512):
    m, d = x.shape
    in_specs = [pl.BlockSpec((tm, d), lambda i: (i, 0)), pl.BlockSpec((1, d), lambda i: (0, 0))]
    args = [x, w.reshape(1, d)]
    if modtab is not None:
        for r in rows:
            in_specs.append(pl.BlockSpec(
                (1, 1, d), lambda i, r=r: (_group_of_tile(i, tm, n_ctx, t_lat) * 6 + r, 0, 0)))
            args.append(modtab)
    return pl.pallas_call(
        functools.partial(_norm_mod_kernel, modulated=modtab is not None),
        grid=(m // tm,),
        in_specs=in_specs,
        out_specs=pl.BlockSpec((tm, d), lambda i: (i, 0)),
        out_shape=jax.ShapeDtypeStruct((m, d), out_dtype),
        compiler_params=_cp("parallel"),
        name="norm_mod",
    )(*args)


def _mm_kernel(x_ref, w_ref, o_ref):
    o_ref[...] = jnp.dot(x_ref[...], w_ref[...], preferred_element_type=F32).astype(o_ref.dtype)


def mm(x, w, out_dtype, tm=1024, tn=1024):
    m, k = x.shape
    n = w.shape[1]
    tm, tn = min(tm, m), min(tn, n)
    return pl.pallas_call(
        _mm_kernel,
        grid=(m // tm, n // tn),
        in_specs=[pl.BlockSpec((tm, k), lambda i, j: (i, 0)),
                  pl.BlockSpec((k, tn), lambda i, j: (0, j))],
        out_specs=pl.BlockSpec((tm, tn), lambda i, j: (i, j)),
        out_shape=jax.ShapeDtypeStruct((m, n), out_dtype),
        compiler_params=_cp("parallel", "parallel"),
        name="mm",
    )(x, w)


def _swiglu_kernel(te_ref, nv_ref, x_ref, wg_ref, wu_ref, o_ref):
    i = pl.program_id(0)

    @pl.when(i < nv_ref[0])
    def _():
        x = x_ref[...].astype(BF16)
        g = jnp.dot(x, wg_ref[0], preferred_element_type=F32)
        u = jnp.dot(x, wu_ref[0], preferred_element_type=F32)
        o_ref[...] = (_silu(g) * u).astype(o_ref.dtype)

    @pl.when(i >= nv_ref[0])
    def _():
        o_ref[...] = jnp.zeros_like(o_ref)


def mm_swiglu(x, wg, wu, tile_expert, n_valid, tm, tn):
    m, k = x.shape
    f = wg.shape[2]

    def xmap(i, j, te, nv):
        return (jnp.minimum(i, nv[0] - 1), 0)

    def wmap(i, j, te, nv):
        return (te[jnp.minimum(i, nv[0] - 1)], 0, j)

    return pl.pallas_call(
        _swiglu_kernel,
        grid_spec=pltpu.PrefetchScalarGridSpec(
            num_scalar_prefetch=2,
            grid=(m // tm, f // tn),
            in_specs=[pl.BlockSpec((tm, k), xmap),
                      pl.BlockSpec((1, k, tn), wmap),
                      pl.BlockSpec((1, k, tn), wmap)],
            out_specs=pl.BlockSpec((tm, tn), lambda i, j, te, nv: (i, j)),
        ),
        out_shape=jax.ShapeDtypeStruct((m, f), BF16),
        compiler_params=_cp("parallel", "arbitrary"),
        name="mm_swiglu",
    )(tile_expert, n_valid, x, wg, wu)


def _moe_swiglu_kernel(te_ref, nv_ref, x_ref, wg_ref, wu_ref, o_ref, wgb_ref, wub_ref):
    i = pl.program_id(1)
    valid = i < nv_ref[0]
    changed = (i == 0) | (te_ref[i] != te_ref[jnp.maximum(i, 1) - 1])

    @pl.when(valid & changed)
    def _():
        wgb_ref[...] = wg_ref[0, 0].astype(BF16)
        wub_ref[...] = wu_ref[0, 0].astype(BF16)

    @pl.when(valid)
    def _():
        x = x_ref[...]
        g = jnp.dot(x, wgb_ref[...], preferred_element_type=F32)
        u = jnp.dot(x, wub_ref[...], preferred_element_type=F32)
        o_ref[...] = (_silu(g) * u).astype(o_ref.dtype)

    @pl.when(jnp.logical_not(valid))
    def _():
        o_ref[...] = jnp.zeros_like(o_ref)


def moe_swiglu(x, wg, wu, layer, tile_expert, n_valid, tm, tn):
    m, k = x.shape
    f = wg.shape[3]

    def xmap(j, i, te, nv):
        return (jnp.minimum(i, nv[0] - 1), 0)

    def wmap(j, i, te, nv):
        return (layer, te[jnp.minimum(i, nv[0] - 1)], 0, j)

    return pl.pallas_call(
        _moe_swiglu_kernel,
        grid_spec=pltpu.PrefetchScalarGridSpec(
            num_scalar_prefetch=2,
            grid=(f // tn, m // tm),
            in_specs=[pl.BlockSpec((tm, k), xmap),
                      pl.BlockSpec((1, 1, k, tn), wmap),
                      pl.BlockSpec((1, 1, k, tn), wmap)],
            out_specs=pl.BlockSpec((tm, tn), lambda j, i, te, nv: (i, j)),
            scratch_shapes=[pltpu.VMEM((k, tn), BF16), pltpu.VMEM((k, tn), BF16)],
        ),
        out_shape=jax.ShapeDtypeStruct((m, f), BF16),
        compiler_params=_cp("arbitrary", "arbitrary"),
        name="moe_swiglu",
    )(tile_expert, n_valid, x, wg, wu)


def _moe_down_kernel(te_ref, nv_ref, x_ref, w_ref, o_ref):
    valid = pl.program_id(1) < nv_ref[0]

    @pl.when(valid)
    def _():
        o_ref[...] = jnp.dot(x_ref[...], w_ref[0], preferred_element_type=F32)

    @pl.when(jnp.logical_not(valid))
    def _():
        o_ref[...] = jnp.zeros_like(o_ref)


def moe_down(x, w, tile_expert, n_valid, tm, tn):
    m, k = x.shape
    n = w.shape[2]

    def xmap(j, i, te, nv):
        return (jnp.minimum(i, nv[0] - 1), 0)

    def wmap(j, i, te, nv):
        return (te[jnp.minimum(i, nv[0] - 1)], 0, j)

    return pl.pallas_call(
        _moe_down_kernel,
        grid_spec=pltpu.PrefetchScalarGridSpec(
            num_scalar_prefetch=2,
            grid=(n // tn, m // tm),
            in_specs=[pl.BlockSpec((tm, k), xmap), pl.BlockSpec((1, k, tn), wmap)],
            out_specs=pl.BlockSpec((tm, tn), lambda j, i, te, nv: (i, j)),
        ),
        out_shape=jax.ShapeDtypeStruct((m, n), F32),
        compiler_params=_cp("arbitrary", "arbitrary"),
        name="moe_down",
    )(tile_expert, n_valid, x, w)


def _mmk_kernel(te_ref, nv_ref, x_ref, w_ref, *rest, nk, has_res):
    if has_res:
        res_ref, g_ref, o_ref = rest
    else:
        (o_ref,) = rest
    i = pl.program_id(0)
    k = pl.program_id(2)

    def finish(acc):
        if has_res:
            return res_ref[...] + g_ref[0] * acc
        return acc

    @pl.when(i < nv_ref[0])
    def _():
        part = jnp.dot(x_ref[...].astype(BF16), w_ref[0], preferred_element_type=F32)
        if nk == 1:
            o_ref[...] = finish(part)
        else:
            @pl.when(k == 0)
            def _():
                o_ref[...] = part

            if nk > 2:
                @pl.when((k > 0) & (k < nk - 1))
                def _():
                    o_ref[...] += part

            @pl.when(k == nk - 1)
            def _():
                o_ref[...] = finish(o_ref[...] + part)

    @pl.when(i >= nv_ref[0])
    def _():
        o_ref[...] = jnp.zeros_like(o_ref)


def mm_k(x, w, tile_expert, n_valid, tm, tn, tk, res=None, modtab=None, gate_row=None,
         n_ctx=0, t_lat=1):
    m, k = x.shape
    n = w.shape[2]
    nk = k // tk

    def xmap(i, j, kk, te, nv):
        return (jnp.minimum(i, nv[0] - 1), kk)

    def wmap(i, j, kk, te, nv):
        return (te[jnp.minimum(i, nv[0] - 1)], kk, j)

    in_specs = [pl.BlockSpec((tm, tk), xmap), pl.BlockSpec((1, tk, tn), wmap)]
    args = [x, w]
    if res is not None:
        in_specs.append(pl.BlockSpec((tm, tn), lambda i, j, kk, te, nv: (i, j)))
        in_specs.append(pl.BlockSpec(
            (1, 1, tn),
            lambda i, j, kk, te, nv: (_group_of_tile(i, tm, n_ctx, t_lat) * 6 + gate_row, 0, j)))
        args += [res, modtab]
    return pl.pallas_call(
        functools.partial(_mmk_kernel, nk=nk, has_res=res is not None),
        grid_spec=pltpu.PrefetchScalarGridSpec(
            num_scalar_prefetch=2,
            grid=(m // tm, n // tn, nk),
            in_specs=in_specs,
            out_specs=pl.BlockSpec((tm, tn), lambda i, j, kk, te, nv: (i, j)),
        ),
        out_shape=jax.ShapeDtypeStruct((m, n), F32),
        compiler_params=_cp("parallel", "parallel", "arbitrary"),
        name="mm_k",
    )(tile_expert, n_valid, *args)


def _one_expert_tables(m, tm):
    return jnp.zeros((m // tm,), jnp.int32), jnp.full((1,), m // tm, jnp.int32)


def rope_tables(rows, head_dim):
    n_freq = head_dim // 4
    inv = ROPE_THETA ** (-jnp.arange(n_freq, dtype=F32) / n_freq)
    r = jnp.repeat(jnp.arange(rows, dtype=F32), GRID_W)
    cl = jnp.tile(jnp.arange(GRID_W, dtype=F32), rows)
    ang = jnp.concatenate([r[:, None] * inv, cl[:, None] * inv], axis=-1)
    return jnp.cos(ang), jnp.sin(ang)


def _attn_prep_kernel(x_ref, cos_ref, sin_ref, qn_ref, kn_ref, q_ref, k_ref, v_ref, kf_ref,
                      *, hq, hkv, dh, n_ctx, tm):
    is_lat = pl.program_id(0) * tm >= n_ctx
    c = jnp.where(is_lat, cos_ref[...], 1.0)
    s = jnp.where(is_lat, sin_ref[...], 0.0)
    half = dh // 2
    for h in range(hq + hkv):
        xh = x_ref[:, h * dh:(h + 1) * dh]
        w = qn_ref[...] if h < hq else kn_ref[...]
        y = xh * lax.rsqrt(jnp.mean(xh * xh, axis=-1, keepdims=True) + EPS) * w
        y1, y2 = y[:, :half], y[:, half:]
        o1 = y1 * c - y2 * s
        o2 = y2 * c + y1 * s
        if h < hq:
            scale = dh ** -0.5
            q_ref[:, h * dh:h * dh + half] = (o1 * scale).astype(BF16)
            q_ref[:, h * dh + half:(h + 1) * dh] = (o2 * scale).astype(BF16)
        else:
            b = (h - hq) * dh
            k_ref[:, b:b + half] = o1.astype(BF16)
            k_ref[:, b + half:b + dh] = o2.astype(BF16)
            kf_ref[:, b:b + half] = o1
            kf_ref[:, b + half:b + dh] = o2
    v_ref[...] = x_ref[:, (hq + hkv) * dh:].astype(BF16)


def attn_prep(qkv, cos, sin, q_norm, k_norm, n_ctx, t_lat, hq, hkv, dh, tm=256):
    m = qkv.shape[0]
    half = dh // 2

    def posmap(i):
        row = i * tm
        return (jnp.where(row >= n_ctx, ((row - n_ctx) % t_lat) // tm, 0), 0)

    return pl.pallas_call(
        functools.partial(_attn_prep_kernel, hq=hq, hkv=hkv, dh=dh, n_ctx=n_ctx, tm=tm),
        grid=(m // tm,),
        in_specs=[pl.BlockSpec((tm, (hq + 2 * hkv) * dh), lambda i: (i, 0)),
                  pl.BlockSpec((tm, half), posmap),
                  pl.BlockSpec((tm, half), posmap),
                  pl.BlockSpec((1, dh), lambda i: (0, 0)),
                  pl.BlockSpec((1, dh), lambda i: (0, 0))],
        out_specs=[pl.BlockSpec((tm, hq * dh), lambda i: (i, 0)),
                   pl.BlockSpec((tm, hkv * dh), lambda i: (i, 0)),
                   pl.BlockSpec((tm, hkv * dh), lambda i: (i, 0)),
                   pl.BlockSpec((tm, hkv * dh), lambda i: (i, 0))],
        out_shape=[jax.ShapeDtypeStruct((m, hq * dh), BF16),
                   jax.ShapeDtypeStruct((m, hkv * dh), BF16),
                   jax.ShapeDtypeStruct((m, hkv * dh), BF16),
                   jax.ShapeDtypeStruct((m, hkv * dh), F32)],
        compiler_params=_cp("parallel"),
        name="attn_prep",
    )(qkv, cos, sin, q_norm.reshape(1, dh), k_norm.reshape(1, dh))


def _attn_kernel(*refs, n_src, n_group, dh):
    q_ref = refs[0]
    kv = refs[1:1 + 2 * n_src]
    o_ref = refs[-1]
    nt = (((1,), (1,)), ((), ()))
    for g in range(n_group):
        qg = q_ref[:, g * dh:(g + 1) * dh]
        ss = [lax.dot_general(qg, kv[2 * i][...], nt, preferred_element_type=F32)
              for i in range(n_src)]
        mx = functools.reduce(jnp.maximum, [jnp.max(s, axis=-1, keepdims=True) for s in ss])
        ps = [jnp.exp(s - mx) for s in ss]
        den = functools.reduce(jnp.add, [jnp.sum(p, axis=-1, keepdims=True) for p in ps])
        o = functools.reduce(jnp.add, [
            jnp.dot(p.astype(BF16), kv[2 * i + 1][...], preferred_element_type=F32)
            for i, p in enumerate(ps)])
        o_ref[:, g * dh:(g + 1) * dh] = (o / den).astype(o_ref.dtype)


def attention(q, k, v, cache, *, batch, seq, row0, hq, hkv, dh, tq):
    n_group = hq // hkv
    nq = seq // tq
    qb0, kb0 = row0 // tq, row0 // seq
    in_specs = [pl.BlockSpec((tq, n_group * dh), lambda b, h, t: (qb0 + b * nq + t, h))]
    args = [q]
    if cache is not None:
        p_len = cache[0].shape[0] // batch
        in_specs += [pl.BlockSpec((p_len, dh), lambda b, h, t: (b, h))] * 2
        args += list(cache)
    in_specs += [pl.BlockSpec((seq, dh), lambda b, h, t: (kb0 + b, h))] * 2
    args += [k, v]
    n_src = (len(args) - 1) // 2
    return pl.pallas_call(
        functools.partial(_attn_kernel, n_src=n_src, n_group=n_group, dh=dh),
        grid=(batch, hkv, nq),
        in_specs=in_specs,
        out_specs=pl.BlockSpec((tq, n_group * dh), lambda b, h, t: (b * nq + t, h)),
        out_shape=jax.ShapeDtypeStruct((batch * seq, hq * dh), BF16),
        compiler_params=_cp("parallel", "parallel", "arbitrary"),
        name="attention",
    )(*args)


def _dn_conv_kernel(x_ref, w_ref, o_ref, pad_ref, *, t, tc, n_ctx_blocks, l_ctx, l_lat,
                    n_q_tiles, n_qk_tiles, dk):
    j = pl.program_id(1)
    zeros8 = jnp.zeros((8, tc), F32)
    pad_ref[0:8, :] = zeros8
    pad_ref[t + 8:t + 16, :] = zeros8
    pad_ref[8:t + 8, :] = x_ref[...]
    w = w_ref[...]
    seq_len = jnp.where(pl.program_id(0) < n_ctx_blocks, l_ctx, l_lat)
    pos = lax.broadcasted_iota(jnp.int32, (t, tc), 0) & (seq_len - 1)
    y = (w[0:1] * jnp.where(pos >= 1, pad_ref[7:t + 7, :], 0.0)
         + w[1:2] * pad_ref[8:t + 8, :]
         + w[2:3] * jnp.where(pos <= seq_len - 2, pad_ref[9:t + 9, :], 0.0)
         + w[3:4] * jnp.where(pos <= seq_len - 3, pad_ref[10:t + 10, :], 0.0))
    y = _silu(y)
    is_qk = j < n_qk_tiles
    q_scale = jnp.where(j < n_q_tiles, dk ** -0.5, 1.0)
    for h in range(tc // LANES):
        yh = y[:, h * LANES:(h + 1) * LANES]
        ss = jnp.sum(yh * yh, axis=-1, keepdims=True)
        inv = jnp.where(is_qk, lax.rsqrt(ss + EPS) * q_scale, 1.0)
        o_ref[:, h * LANES:(h + 1) * LANES] = (yh * inv).astype(o_ref.dtype)


def dn_short_conv(proj, conv_w, *, n_ctx, l_ctx, l_lat, qk_dim, conv_dim, dk, tc=512):
    m = proj.shape[0]
    assert l_lat % l_ctx == 0 and n_ctx % l_lat == 0 and m % l_lat == 0
    assert l_ctx & (l_ctx - 1) == 0 and l_lat & (l_lat - 1) == 0 and l_ctx >= 4
    return pl.pallas_call(
        functools.partial(_dn_conv_kernel, t=l_lat, tc=tc, n_ctx_blocks=n_ctx // l_lat,
                          l_ctx=l_ctx, l_lat=l_lat, n_q_tiles=qk_dim // tc,
                          n_qk_tiles=2 * qk_dim // tc, dk=dk),
        grid=(m // l_lat, conv_dim // tc),
        in_specs=[pl.BlockSpec((l_lat, tc), lambda b, j: (b, j)),
                  pl.BlockSpec((conv_w.shape[0], tc), lambda b, j: (0, j))],
        out_specs=pl.BlockSpec((l_lat, tc), lambda b, j: (b, j)),
        out_shape=jax.ShapeDtypeStruct((m, conv_dim), BF16),
        scratch_shapes=[pltpu.VMEM((l_lat + 16, tc), F32)],
        compiler_params=_cp("parallel", "parallel"),
        name="dn_conv",
    )(proj, conv_w)


def _softplus(x):
    return jnp.maximum(x, 0.0) + jnp.log1p(jnp.exp(-jnp.abs(x)))


def _pair_masks(n):
    r = lax.broadcasted_iota(jnp.int32, (n, n), 0)
    c = lax.broadcasted_iota(jnp.int32, (n, n), 1)
    return [((r >> (k + 1)) == (c >> (k + 1))) & ((r >> k) != (c >> k))
            for k in range(int(math.log2(n)))]


def _unit_tri_inverses(lows, eye, masks):
    ds = [eye - jnp.where(masks[0], low, 0.0) for low in lows]
    lows_b = [low.astype(BF16) for low in lows]
    zero = jnp.zeros((), BF16)
    for mask in masks[1:]:
        dbs = [d.astype(BF16) for d in ds]
        ts = [jnp.dot(db, jnp.where(mask, lb, zero), preferred_element_type=F32)
              for db, lb in zip(dbs, lows_b)]
        ds = [d - jnp.dot(t.astype(BF16), db, preferred_element_type=F32)
              for d, t, db in zip(ds, ts, dbs)]
    return ds


def _dn_scan_kernel(rowf_ref, rowb_ref, flag_ref, seq_ref,
                    qf_ref, kf_ref, vf_ref, qb_ref, kb_ref, vb_ref, baf_ref, bab_ref,
                    nega_ref, dtb_ref, s0_ref, of_ref, ob_ref, sout_ref, s_scr, *, hb, chunk):
    step = pl.program_id(1)
    flag = flag_ref[step]
    first = (flag & 1) == 1
    last = (flag & 2) == 2
    is_lat = (flag & 4) == 4
    dh = DN_HEAD_DIM
    nt = (((1,), (1,)), ((), ()))

    @pl.when(first)
    def _():
        s_scr[...] = jnp.where(is_lat, s0_ref[0], 0.0)

    r = lax.broadcasted_iota(jnp.int32, (chunk, chunk), 0)
    c = lax.broadcasted_iota(jnp.int32, (chunk, chunk), 1)
    eye = (r == c).astype(F32)
    masks = _pair_masks(chunk)

    heads = []
    lows = []
    for d, q_ref, k_ref, v_ref, ba_ref, o_ref in ((0, qf_ref, kf_ref, vf_ref, baf_ref, of_ref),
                                                   (1, qb_ref, kb_ref, vb_ref, bab_ref, ob_ref)):
        incl = (r >= c) if d == 0 else (r <= c)
        strict = (r > c) if d == 0 else (r < c)
        last_row = chunk - 1 if d == 0 else 0
        ba = ba_ref[...]
        sig = jax.nn.sigmoid(ba)
        g = nega_ref[0] * _softplus(ba + dtb_ref[0])
        gc = jnp.dot(incl.astype(F32), g, preferred_element_type=F32,
                     precision=lax.Precision.HIGHEST)
        gct = gc.T
        egc = jnp.exp(gc)
        glast = gc[last_row:last_row + 1, :]
        eglast = jnp.exp(glast)
        ekg = jnp.exp(glast - gc)
        for kh in range(hb // 2):
            qh = q_ref[:, kh * dh:(kh + 1) * dh]
            kk_ = k_ref[:, kh * dh:(kh + 1) * dh]
            kf32 = kk_.astype(F32)
            qf32 = qh.astype(F32)
            kkt = lax.dot_general(kk_, kk_, nt, preferred_element_type=F32)
            qkt = lax.dot_general(qh, kk_, nt, preferred_element_type=F32)
            for rr in range(2):
                hh = kh * 2 + rr
                cb = d * 2 * hb + hh
                ca = d * 2 * hb + hb + hh
                bcol = sig[:, cb:cb + 1]
                ecol = egc[:, ca:ca + 1]
                decay = jnp.exp(jnp.where(incl, gc[:, ca:ca + 1] - gct[ca:ca + 1, :], -jnp.inf))
                lows.append(jnp.where(strict, bcol * kkt * decay, 0.0))
                heads.append(dict(
                    d=d, hh=hh, o_ref=o_ref,
                    vbm=(v_ref[:, hh * dh:(hh + 1) * dh].astype(F32) * bcol).astype(BF16),
                    kbg=(kf32 * (bcol * ecol)).astype(BF16),
                    a=jnp.where(incl, qkt * decay, 0.0).astype(BF16),
                    qg=(qf32 * ecol).astype(BF16),
                    kgt=(kf32 * ekg[:, ca:ca + 1]).T.astype(BF16),
                    egl=eglast[:, ca:ca + 1]))

    tinvs = [t.astype(BF16) for t in _unit_tri_inverses(lows, eye, masks)]
    uws = [jnp.dot(t, jnp.concatenate([h["vbm"], h["kbg"]], axis=1), preferred_element_type=F32)
           for t, h in zip(tinvs, heads)]
    ss = [s_scr[h["d"], h["hh"]] for h in heads]
    wqs = [jnp.dot(jnp.concatenate([uw[:, dh:].astype(BF16), h["qg"]], axis=0), s.astype(BF16),
                   preferred_element_type=F32)
           for uw, h, s in zip(uws, heads, ss)]
    vnbs = [(uw[:, :dh] - wq[:chunk]).astype(BF16) for uw, wq in zip(uws, wqs)]
    avs = [jnp.dot(jnp.concatenate([h["a"], h["kgt"]], axis=0), vnb, preferred_element_type=F32)
           for h, vnb in zip(heads, vnbs)]
    for h, wq, av in zip(heads, wqs, avs):
        hh = h["hh"]
        h["o_ref"][:, hh * dh:(hh + 1) * dh] = wq[chunk:] + av[:chunk]
    for h, s, av in zip(heads, ss, avs):
        s_scr[h["d"], h["hh"]] = s * h["egl"] + av[chunk:]

    @pl.when(last)
    def _():
        sout_ref[0] = s_scr[...]


def dn_scan(qkvc, ba, nega, dtb, s0, seq_lens, *, qk_dim, n_vheads, hb=DN_HEADS_PER_STEP):
    m = qkvc.shape[0]
    chunk = DN_CHUNK
    dh = DN_HEAD_DIM
    n_hg = n_vheads // hb
    rowf, rowb, flags, seqs = [], [], [], []
    row, sid, n_zero_seq = 0, 0, 0
    for n_seq, length, uses_s0 in seq_lens:
        n_chunks = length // chunk
        for _ in range(n_seq):
            for n in range(n_chunks):
                rowf.append(row + n)
                rowb.append(row + n_chunks - 1 - n)
                flags.append((n == 0) * 1 + (n == n_chunks - 1) * 2 + (4 if uses_s0 else 0))
                seqs.append(sid)
            row += n_chunks
            sid += 1
        if not uses_s0:
            n_zero_seq += n_seq
    n_steps = len(rowf)
    n_seq_total = sid
    tabs = [jnp.asarray(np.array(t, np.int32)) for t in (rowf, rowb, flags, seqs)]
    qw, vw = (hb // 2) * dh, hb * dh
    kb0, vb0 = qk_dim // qw, 2 * qk_dim // vw

    def spec(width, col0, rows_idx):
        return pl.BlockSpec((chunk, width),
                            lambda hg, s, rf, rb, fl, sq: ((rf, rb)[rows_idx][s], col0 + hg))

    in_specs = [spec(qw, 0, 0), spec(qw, kb0, 0), spec(vw, vb0, 0),
                spec(qw, 0, 1), spec(qw, kb0, 1), spec(vw, vb0, 1),
                pl.BlockSpec((chunk, LANES), lambda hg, s, rf, rb, fl, sq: (rf[s], hg)),
                pl.BlockSpec((chunk, LANES), lambda hg, s, rf, rb, fl, sq: (rb[s], hg)),
                pl.BlockSpec((1, 1, LANES), lambda hg, s, rf, rb, fl, sq: (hg, 0, 0)),
                pl.BlockSpec((1, 1, LANES), lambda hg, s, rf, rb, fl, sq: (hg, 0, 0)),
                pl.BlockSpec((1, 2, hb, dh, dh),
                             lambda hg, s, rf, rb, fl, sq:
                             (jnp.maximum(sq[s] - n_zero_seq, 0), 0, hg, 0, 0))]
    out_specs = [pl.BlockSpec((chunk, vw), lambda hg, s, rf, rb, fl, sq: (rf[s], hg)),
                 pl.BlockSpec((chunk, vw), lambda hg, s, rf, rb, fl, sq: (rb[s], hg)),
                 pl.BlockSpec((1, 2, hb, dh, dh),
                              lambda hg, s, rf, rb, fl, sq: (sq[s], 0, hg, 0, 0))]
    return pl.pallas_call(
        functools.partial(_dn_scan_kernel, hb=hb, chunk=chunk),
        grid_spec=pltpu.PrefetchScalarGridSpec(
            num_scalar_prefetch=4,
            grid=(n_hg, n_steps),
            in_specs=in_specs,
            out_specs=out_specs,
            scratch_shapes=[pltpu.VMEM((2, hb, dh, dh), F32)],
        ),
        out_shape=[jax.ShapeDtypeStruct((m, n_vheads * dh), F32),
                   jax.ShapeDtypeStruct((m, n_vheads * dh), F32),
                   jax.ShapeDtypeStruct((n_seq_total, 2, n_vheads, dh, dh), F32)],
        compiler_params=_cp("parallel", "arbitrary"),
        name="dn_scan",
    )(*tabs, qkvc, qkvc, qkvc, qkvc, qkvc, qkvc, ba, ba, nega, dtb, s0)


def _dn_gate_kernel(of_ref, ob_ref, z_ref, w_ref, o_ref, *, n_heads):
    w = w_ref[...]
    for h in range(n_heads):
        sl = slice(h * DN_HEAD_DIM, (h + 1) * DN_HEAD_DIM)
        o = of_ref[:, sl] + ob_ref[:, sl]
        y = o * lax.rsqrt(jnp.mean(o * o, axis=-1, keepdims=True) + EPS) * w
        o_ref[:, sl] = (y * _silu(z_ref[:, sl])).astype(o_ref.dtype)


def dn_gate(o_f, o_b, proj, out_norm, z_col0, tm=256, tn=1024):
    m, n = o_f.shape
    zb0 = z_col0 // tn
    return pl.pallas_call(
        functools.partial(_dn_gate_kernel, n_heads=tn // DN_HEAD_DIM),
        grid=(m // tm, n // tn),
        in_specs=[pl.BlockSpec((tm, tn), lambda i, j: (i, j)),
                  pl.BlockSpec((tm, tn), lambda i, j: (i, j)),
                  pl.BlockSpec((tm, tn), lambda i, j: (i, zb0 + j)),
                  pl.BlockSpec((1, DN_HEAD_DIM), lambda i, j: (0, 0))],
        out_specs=pl.BlockSpec((tm, tn), lambda i, j: (i, j)),
        out_shape=jax.ShapeDtypeStruct((m, n), BF16),
        compiler_params=_cp("parallel", "parallel"),
        name="dn_gate",
    )(o_f, o_b, proj, out_norm.reshape(1, DN_HEAD_DIM))


def _router_kernel(x_ref, whi_ref, wlo_ref, info_ref, cnt_ref, *, n_experts, tm):
    @pl.when(pl.program_id(0) == 0)
    def _():
        cnt_ref[...] = jnp.zeros_like(cnt_ref)

    x = x_ref[...]
    xhi = x.astype(BF16)
    xlo = (x - xhi.astype(F32)).astype(BF16)
    lg = (jnp.dot(xhi, whi_ref[...], preferred_element_type=F32)
          + jnp.dot(xlo, whi_ref[...], preferred_element_type=F32)
          + jnp.dot(xhi, wlo_ref[...], preferred_element_type=F32))
    lane = lax.broadcasted_iota(jnp.int32, lg.shape, 1).astype(F32)
    neg = -jnp.inf
    lg = jnp.where(lane < n_experts, lg, neg)
    m1 = jnp.max(lg, axis=-1, keepdims=True)
    i1 = jnp.min(jnp.where(lg == m1, lane, float(LANES)), axis=-1, keepdims=True)
    mk1 = lane == i1
    lg2 = jnp.where(mk1, neg, lg)
    m2 = jnp.max(lg2, axis=-1, keepdims=True)
    i2 = jnp.min(jnp.where(lg2 == m2, lane, float(LANES)), axis=-1, keepdims=True)
    mk2 = lane == i2
    e = jnp.exp(m2 - m1)
    w1 = 1.0 / (1.0 + e)
    sel = jnp.where(mk1 | mk2, 1.0, 0.0)
    r = lax.broadcasted_iota(jnp.int32, (tm, tm), 0)
    c = lax.broadcasted_iota(jnp.int32, (tm, tm), 1)
    before = jnp.dot(jnp.where(r > c, 1.0, 0.0).astype(BF16), sel.astype(BF16),
                     preferred_element_type=F32)
    rank = cnt_ref[...] + before
    r1 = jnp.sum(jnp.where(mk1, rank, 0.0), axis=-1, keepdims=True)
    r2 = jnp.sum(jnp.where(mk2, rank, 0.0), axis=-1, keepdims=True)
    cnt_ref[...] += jnp.sum(sel, axis=0, keepdims=True)
    info = jnp.where(lane == 0, i1, 0.0)
    for k, val in enumerate((i2, r1, r2, w1, e * w1), start=1):
        info = jnp.where(lane == k, val, info)
    info_ref[...] = info


def router(x, w_router, tm=512):
    m, d = x.shape
    e = w_router.shape[1]
    wp = jnp.zeros((d, LANES), F32).at[:, :e].set(w_router)
    whi = wp.astype(BF16)
    wlo = (wp - whi.astype(F32)).astype(BF16)
    return pl.pallas_call(
        functools.partial(_router_kernel, n_experts=e, tm=tm),
        grid=(m // tm,),
        in_specs=[pl.BlockSpec((tm, d), lambda i: (i, 0)),
                  pl.BlockSpec((d, LANES), lambda i: (0, 0)),
                  pl.BlockSpec((d, LANES), lambda i: (0, 0))],
        out_specs=[pl.BlockSpec((tm, LANES), lambda i: (i, 0)),
                   pl.BlockSpec((1, LANES), lambda i: (0, 0))],
        out_shape=[jax.ShapeDtypeStruct((m, LANES), F32),
                   jax.ShapeDtypeStruct((1, LANES), F32)],
        compiler_params=_cp("arbitrary"),
        name="router",
    )(x, whi, wlo)


def _row_copy(src_hbm, row, dst_ref, r, sem):
    return pltpu.make_async_copy(src_hbm.at[pl.ds(row, 1), :], dst_ref.at[pl.ds(r, 1), :], sem)


def _gather_kernel(idx_ref, x_hbm, o_ref, buf, sem, *, tg):
    def start(r, carry):
        _row_copy(x_hbm, idx_ref[0, 0, r], buf, r, sem).start()
        return carry

    def wait(r, carry):
        _row_copy(x_hbm, idx_ref[0, 0, r], buf, r, sem).wait()
        return carry

    lax.fori_loop(0, tg, start, 0, unroll=8)
    lax.fori_loop(0, tg, wait, 0, unroll=8)
    o_ref[...] = buf[...].astype(o_ref.dtype)


def gather_rows(x, idx, out_dtype, tg=256):
    ms = idx.shape[0]
    d = x.shape[1]
    return pl.pallas_call(
        functools.partial(_gather_kernel, tg=tg),
        grid=(ms // tg,),
        in_specs=[pl.BlockSpec((1, 1, tg), lambda i: (i, 0, 0), memory_space=pltpu.SMEM),
                  pl.BlockSpec(memory_space=pl.ANY)],
        out_specs=pl.BlockSpec((tg, d), lambda i: (i, 0)),
        out_shape=jax.ShapeDtypeStruct((ms, d), out_dtype),
        scratch_shapes=[pltpu.VMEM((tg, d), x.dtype), pltpu.SemaphoreType.DMA(())],
        compiler_params=_cp("arbitrary"),
        name="gather_rows",
    )(idx.reshape(ms // tg, 1, tg), x)


def _combine_kernel(pos_ref, ys_hbm, g_ref, res_ref, gm_ref, o_ref, buf, sem, *, tc, n_sel):
    def start(r, carry):
        for j in range(n_sel):
            _row_copy(ys_hbm, pos_ref[0, 0, r * n_sel + j], buf.at[j], r, sem).start()
        return carry

    def wait(r, carry):
        for j in range(n_sel):
            _row_copy(ys_hbm, pos_ref[0, 0, r * n_sel + j], buf.at[j], r, sem).wait()
        return carry

    lax.fori_loop(0, tc, start, 0)
    lax.fori_loop(0, tc, wait, 0)
    acc = g_ref[:, 0:1] * buf[0]
    for j in range(1, n_sel):
        acc = acc + g_ref[:, j:j + 1] * buf[j]
    o_ref[...] = res_ref[...] + gm_ref[0] * acc


def moe_combine(ys, pos, gates, res, modtab, gate_row, n_ctx, t_lat, tc=128):
    m, d = res.shape
    n_sel = pos.shape[1]
    return pl.pallas_call(
        functools.partial(_combine_kernel, tc=tc, n_sel=n_sel),
        grid=(m // tc,),
        in_specs=[pl.BlockSpec((1, 1, tc * n_sel), lambda i: (i, 0, 0), memory_space=pltpu.SMEM),
                  pl.BlockSpec(memory_space=pl.ANY),
                  pl.BlockSpec((tc, n_sel), lambda i: (i, 0)),
                  pl.BlockSpec((tc, d), lambda i: (i, 0)),
                  pl.BlockSpec((1, 1, d),
                               lambda i: (_group_of_tile(i, tc, n_ctx, t_lat) * 6 + gate_row, 0, 0))],
        out_specs=pl.BlockSpec((tc, d), lambda i: (i, 0)),
        out_shape=jax.ShapeDtypeStruct((m, d), F32),
        scratch_shapes=[pltpu.VMEM((n_sel, tc, d), F32), pltpu.SemaphoreType.DMA(())],
        compiler_params=_cp("arbitrary"),
        name="moe_combine",
    )(pos.reshape(m // tc, 1, tc * n_sel), ys, gates, res, modtab)


def _invert_kernel(pos_ref, tok_ref, *, n_pairs, n_sel, n_rows):
    def zero(r, carry):
        tok_ref[r] = 0
        return carry

    def put(p, carry):
        tok_ref[pos_ref[p]] = p // n_sel
        return carry

    lax.fori_loop(0, n_rows, zero, 0, unroll=8)
    lax.fori_loop(0, n_pairs, put, 0, unroll=8)


def invert_positions(pos_flat, n_sel, n_rows):
    n_pairs = pos_flat.shape[0]
    return pl.pallas_call(
        functools.partial(_invert_kernel, n_pairs=n_pairs, n_sel=n_sel, n_rows=n_rows),
        in_specs=[pl.BlockSpec(memory_space=pltpu.SMEM)],
        out_specs=pl.BlockSpec(memory_space=pltpu.SMEM),
        out_shape=jax.ShapeDtypeStruct((n_rows,), jnp.int32),
        name="invert_positions",
    )(pos_flat)


def routing_tables(info, counts, n_experts, tm):
    m = info.shape[0]
    cnt = counts[0, :n_experts].astype(jnp.int32)
    padded = ((cnt + tm - 1) // tm) * tm
    ends = jnp.cumsum(padded)
    starts = ends - padded
    ms = TOP_K * m + n_experts * tm
    ids = info[:, 0:TOP_K].astype(jnp.int32)
    ranks = info[:, TOP_K:2 * TOP_K].astype(jnp.int32)
    g_sel = info[:, 2 * TOP_K:3 * TOP_K]
    onehot = ids[:, :, None] == jnp.arange(n_experts, dtype=jnp.int32)[None, None, :]
    pos_sel = ranks + jnp.sum(jnp.where(onehot, starts[None, None, :], 0), axis=-1)
    tok_sorted = invert_positions(pos_sel.reshape(-1), TOP_K, ms)
    tile_start = jnp.arange(ms // tm, dtype=jnp.int32) * tm
    tile_expert = jnp.minimum(
        jnp.sum(tile_start[:, None] >= ends[None, :], axis=1), n_experts - 1).astype(jnp.int32)
    n_valid = (ends[-1] // tm).astype(jnp.int32).reshape(1)
    return tok_sorted, tile_expert, n_valid, pos_sel.astype(jnp.int32), g_sel


def moe_layer(h, x_res, w_router, wg, wu, layer, wd, modtab, gate_row, n_ctx, t_lat, tm=512):
    n_experts = wg.shape[1]
    info, counts = router(h, w_router)
    tok_sorted, te, nv, pos_sel, g_sel = routing_tables(info, counts, n_experts, tm)
    xs = gather_rows(h, tok_sorted, BF16)
    act = moe_swiglu(xs, wg, wu, layer, te, nv, tm=tm, tn=1024)
    ys = moe_down(act, wd, te, nv, tm=tm, tn=min(1024, wd.shape[2]))
    return moe_combine(ys, pos_sel, g_sel, x_res, modtab, gate_row, n_ctx, t_lat)


def _dn_ba_layout(hb):
    n_hg = DN_V_HEADS // hb
    idx = np.full((n_hg, LANES), -1, np.int64)
    for hg in range(n_hg):
        for d in range(2):
            for ab in range(2):
                for hh in range(hb):
                    idx[hg, d * 2 * hb + ab * hb + hh] = d * 2 * DN_V_HEADS + ab * DN_V_HEADS + hg * hb + hh
    return idx.reshape(-1)


def _permute_cols(a, idx):
    valid = jnp.asarray(idx >= 0)
    return jnp.where(valid, jnp.take(a, jnp.asarray(np.maximum(idx, 0)), axis=-1), 0.0)


def kernel(x_prompt, x_sample, state_dn, cache_k, cache_v, c, c_ctx, w_mod, b_mod, norm_mix, norm_ffn, norm_final, dn_w_in, dn_conv, dn_A_log, dn_dt_bias, dn_out_norm, dn_w_out, att_w_in, att_q_norm, att_k_norm, att_w_out, ffn_w_gate, ffn_w_up, ffn_w_down, moe_router, moe_w_gate, moe_w_up, moe_w_down):
    bc, tc_, d = x_prompt.shape
    bx, tx, _ = x_sample.shape
    depth = w_mod.shape[0]
    n_ctx = bc * tc_
    m = n_ctx + bx * tx
    past = cache_k.shape[2]
    qk_dim = DN_K_HEADS * DN_HEAD_DIM
    v_dim = DN_V_HEADS * DN_HEAD_DIM
    conv_dim = 2 * qk_dim + v_dim
    kv_dim = ATT_KV_HEADS * ATT_HEAD_DIM
    q_dim = ATT_Q_HEADS * ATT_HEAD_DIM

    x = jnp.concatenate([x_prompt.reshape(n_ctx, d), x_sample.reshape(bx * tx, d)], axis=0)
    n_groups = 16
    cvec = jnp.zeros((n_groups, d), F32).at[0].set(c_ctx).at[1:1 + bx].set(c)
    mods = adaln_all(cvec, w_mod, b_mod)
    cos, sin = rope_tables(tx // GRID_W, ATT_HEAD_DIM)
    ba_idx = _dn_ba_layout(DN_HEADS_PER_STEP)
    n_hg = DN_V_HEADS // DN_HEADS_PER_STEP

    new_dn, new_k, new_v = [], [], []
    for i in range(depth):
        j = i // 2
        modtab = mods[i].reshape(n_groups * 6, 1, d)
        h = norm_mod(x, norm_mix[i], modtab, (0, 1), n_ctx, tx, BF16)
        if i % 2 == 0:
            w_in = dn_w_in[j]
            proj = mm(h, w_in[:, :conv_dim + v_dim].astype(BF16), F32)
            ba = mm(h, _permute_cols(w_in[:, conv_dim + v_dim:], ba_idx).astype(BF16), F32)
            old = jnp.zeros((2, 2, DN_V_HEADS), F32)
            nega_old = old.at[:, 1].set(-jnp.exp(dn_A_log[j].astype(F32))).reshape(-1)
            dtb_old = old.at[:, 1].set(dn_dt_bias[j].astype(F32)).reshape(-1)
            nega = _permute_cols(nega_old, ba_idx).reshape(n_hg, 1, LANES)
            dtb = _permute_cols(dtb_old, ba_idx).reshape(n_hg, 1, LANES)
            qkvc = dn_short_conv(proj, dn_conv[j], n_ctx=n_ctx, l_ctx=tc_, l_lat=tx,
                                 qk_dim=qk_dim, conv_dim=conv_dim, dk=DN_HEAD_DIM)
            o_f, o_b, s_out = dn_scan(qkvc, ba, nega, dtb, state_dn[:, j].astype(F32),
                                      [(bc, tc_, False), (bx, tx, True)],
                                      qk_dim=qk_dim, n_vheads=DN_V_HEADS)
            new_dn.append(s_out[:bc])
            og = dn_gate(o_f, o_b, proj, dn_out_norm[j], conv_dim)
            te, nv = _one_expert_tables(m, 512)
            x = mm_k(og, dn_w_out[j].astype(BF16)[None], te, nv, 512, 1024, v_dim,
                     res=x, modtab=modtab, gate_row=2, n_ctx=n_ctx, t_lat=tx)
        else:
            qkv = mm(h, att_w_in[j].astype(BF16), F32)
            q, k, v, kf = attn_prep(qkv, cos, sin, att_q_norm[j], att_k_norm[j], n_ctx, tx,
                                    ATT_Q_HEADS, ATT_KV_HEADS, ATT_HEAD_DIM)
            o_ctx = attention(q, k, v, None, batch=bc, seq=tc_, row0=0,
                              hq=ATT_Q_HEADS, hkv=ATT_KV_HEADS, dh=ATT_HEAD_DIM, tq=tc_)
            ck = cache_k[:, j].reshape(bx * past, kv_dim).astype(BF16)
            cv = cache_v[:, j].reshape(bx * past, kv_dim).astype(BF16)
            o_lat = attention(q, k, v, (ck, cv), batch=bx, seq=tx, row0=n_ctx,
                              hq=ATT_Q_HEADS, hkv=ATT_KV_HEADS, dh=ATT_HEAD_DIM, tq=256)
            o = jnp.concatenate([o_ctx, o_lat], axis=0)
            new_k.append(kf[:n_ctx].reshape(bc, tc_, ATT_KV_HEADS, ATT_HEAD_DIM))
            new_v.append(qkv[:n_ctx, q_dim + kv_dim:].reshape(bc, tc_, ATT_KV_HEADS, ATT_HEAD_DIM))
            te, nv = _one_expert_tables(m, 512)
            x = mm_k(o, att_w_out[j].astype(BF16)[None], te, nv, 512, 1024, q_dim,
                     res=x, modtab=modtab, gate_row=2, n_ctx=n_ctx, t_lat=tx)
        if i % 2 == 0:
            h = norm_mod(x, norm_ffn[i], modtab, (3, 4), n_ctx, tx, BF16)
            f = ffn_w_gate.shape[2]
            te, nv = _one_expert_tables(m, 1024)
            act = mm_swiglu(h, ffn_w_gate[j].astype(BF16)[None], ffn_w_up[j].astype(BF16)[None],
                            te, nv, tm=1024, tn=512)
            te, nv = _one_expert_tables(m, 512)
            x = mm_k(act, ffn_w_down[j].astype(BF16)[None], te, nv, 512, 1024, f // 2,
                     res=x, modtab=modtab, gate_row=5, n_ctx=n_ctx, t_lat=tx)
        else:
            h = norm_mod(x, norm_ffn[i], modtab, (3, 4), n_ctx, tx, F32)
            x = moe_layer(h, x, moe_router[j], moe_w_gate, moe_w_up, j,
                          moe_w_down[j].astype(BF16),
                          modtab, 5, n_ctx, tx)
    y = norm_mod(x, norm_final, None, (), n_ctx, tx, F32)
    y_prompt = y[:n_ctx].reshape(bc, tc_, d)
    y_sample = y[n_ctx:].reshape(bx, tx, d)
    return (y_prompt, y_sample, jnp.stack(new_dn, axis=1),
            jnp.stack(new_k, axis=1), jnp.stack(new_v, axis=1))
```

```python
import functools
import math

import numpy as np
import jax
import jax.numpy as jnp
from jax import lax
from jax.experimental import pallas as pl
from jax.experimental.pallas import tpu as pltpu

F32 = jnp.float32
BF16 = jnp.bfloat16
EPS = 1e-6

GRID_W = 64
ROPE_THETA = 10000.0
DN_K_HEADS = 16
DN_V_HEADS = 32
DN_HEAD_DIM = 128
DN_CHUNK = 64
ATT_Q_HEADS = 16
ATT_KV_HEADS = 2
ATT_HEAD_DIM = 256
N_EXPERTS = 8
TOP_K = 2

LANES = 128
VMEM_LIMIT_BYTES = 56 * 1024 * 1024

DN_HEADS_PER_STEP = 32


def _cp(*sem):
    return pltpu.CompilerParams(dimension_semantics=sem, vmem_limit_bytes=VMEM_LIMIT_BYTES)


def _silu(x):
    return x * jax.nn.sigmoid(x)


def _group_of_tile(i, tm, n_ctx, t_lat):
    row = i * tm
    return jnp.where(row < n_ctx, 0, 1 + (row - n_ctx) // t_lat)


def _adaln_kernel(c_ref, w_ref, b_ref, o_ref):
    s = _silu(c_ref[...]).astype(BF16)
    o_ref[0] = jnp.dot(s, w_ref[0].astype(BF16), preferred_element_type=F32) + b_ref[0]


def adaln_all(cvec, w_mod, b_mod, tn=1024):
    n_layers, d, n = w_mod.shape
    g = cvec.shape[0]
    return pl.pallas_call(
        _adaln_kernel,
        grid=(n_layers, n // tn),
        in_specs=[
            pl.BlockSpec((g, d), lambda l, j: (0, 0)),
            pl.BlockSpec((1, d, tn), lambda l, j: (l, 0, j)),
            pl.BlockSpec((1, 1, tn), lambda l, j: (l, 0, j)),
        ],
        out_specs=pl.BlockSpec((1, g, tn), lambda l, j: (l, 0, j)),
        out_shape=jax.ShapeDtypeStruct((n_layers, g, n), F32),
        compiler_params=_cp("parallel", "parallel"),
        name="adaln",
    )(cvec, w_mod, b_mod.reshape(n_layers, 1, n))


def _norm_mod_kernel(x_ref, w_ref, *rest, modulated):
    x = x_ref[...]
    y = x * lax.rsqrt(jnp.mean(x * x, axis=-1, keepdims=True) + EPS) * w_ref[...]
    if modulated:
        sh_ref, sc_ref, o_ref = rest
        y = y * (1.0 + sc_ref[0]) + sh_ref[0]
    else:
        (o_ref,) = rest
    o_ref[...] = y.astype(o_ref.dtype)


def norm_mod(x, w, modtab, rows, n_ctx, t_lat, out_dtype, tm=512):
    m, d = x.shape
    in_specs = [pl.BlockSpec((tm, d), lambda i: (i, 0)), pl.BlockSpec((1, d), lambda i: (0, 0))]
    args = [x, w.reshape(1, d)]
    if modtab is not None:
        for r in rows:
            in_specs.append(pl.BlockSpec(
                (1, 1, d), lambda i, r=r: (_group_of_tile(i, tm, n_ctx, t_lat) * 6 + r, 0, 0)))
            args.append(modtab)
    return pl.pallas_call(
        functools.partial(_norm_mod_kernel, modulated=modtab is not None),
        grid=(m // tm,),
        in_specs=in_specs,
        out_specs=pl.BlockSpec((tm, d), lambda i: (i, 0)),
        out_shape=jax.ShapeDtypeStruct((m, d), out_dtype),
        compiler_params=_cp("parallel"),
        name="norm_mod",
    )(*args)


def _mm_kernel(x_ref, w_ref, o_ref):
    o_ref[...] = jnp.dot(x_ref[...], w_ref[...], preferred_element_type=F32).astype(o_ref.dtype)


def mm(x, w, out_dtype, tm=1024, tn=1024):
    m, k = x.shape
    n = w.shape[1]
    tm, tn = min(tm, m), min(tn, n)
    return pl.pallas_call(
        _mm_kernel,
        grid=(m // tm, n // tn),
        in_specs=[pl.BlockSpec((tm, k), lambda i, j: (i, 0)),
                  pl.BlockSpec((k, tn), lambda i, j: (0, j))],
        out_specs=pl.BlockSpec((tm, tn), lambda i, j: (i, j)),
        out_shape=jax.ShapeDtypeStruct((m, n), out_dtype),
        compiler_params=_cp("parallel", "parallel"),
        name="mm",
    )(x, w)


def _swiglu_kernel(te_ref, nv_ref, x_ref, wg_ref, wu_ref, o_ref):
    i = pl.program_id(0)

    @pl.when(i < nv_ref[0])
    def _():
        x = x_ref[...].astype(BF16)
        g = jnp.dot(x, wg_ref[0], preferred_element_type=F32)
        u = jnp.dot(x, wu_ref[0], preferred_element_type=F32)
        o_ref[...] = (_silu(g) * u).astype(o_ref.dtype)

    @pl.when(i >= nv_ref[0])
    def _():
        o_ref[...] = jnp.zeros_like(o_ref)


def mm_swiglu(x, wg, wu, tile_expert, n_valid, tm, tn):
    m, k = x.shape
    f = wg.shape[2]

    def xmap(i, j, te, nv):
        return (jnp.minimum(i, nv[0] - 1), 0)

    def wmap(i, j, te, nv):
        return (te[jnp.minimum(i, nv[0] - 1)], 0, j)

    return pl.pallas_call(
        _swiglu_kernel,
        grid_spec=pltpu.PrefetchScalarGridSpec(
            num_scalar_prefetch=2,
            grid=(m // tm, f // tn),
            in_specs=[pl.BlockSpec((tm, k), xmap),
                      pl.BlockSpec((1, k, tn), wmap),
                      pl.BlockSpec((1, k, tn), wmap)],
            out_specs=pl.BlockSpec((tm, tn), lambda i, j, te, nv: (i, j)),
        ),
        out_shape=jax.ShapeDtypeStruct((m, f), BF16),
        compiler_params=_cp("parallel", "arbitrary"),
        name="mm_swiglu",
    )(tile_expert, n_valid, x, wg, wu)


def _moe_swiglu_kernel(te_ref, nv_ref, x_ref, wg_ref, wu_ref, o_ref, wgb_ref, wub_ref):
    i = pl.program_id(1)
    valid = i < nv_ref[0]
    changed = (i == 0) | (te_ref[i] != te_ref[jnp.maximum(i, 1) - 1])

    @pl.when(valid & changed)
    def _():
        wgb_ref[...] = wg_ref[0, 0].astype(BF16)
        wub_ref[...] = wu_ref[0, 0].astype(BF16)

    @pl.when(valid)
    def _():
        x = x_ref[...]
        g = jnp.dot(x, wgb_ref[...], preferred_element_type=F32)
        u = jnp.dot(x, wub_ref[...], preferred_element_type=F32)
        o_ref[...] = (_silu(g) * u).astype(o_ref.dtype)

    @pl.when(jnp.logical_not(valid))
    def _():
        o_ref[...] = jnp.zeros_like(o_ref)


def moe_swiglu(x, wg, wu, layer, tile_expert, n_valid, tm, tn):
    m, k = x.shape
    f = wg.shape[3]

    def xmap(j, i, te, nv):
        return (jnp.minimum(i, nv[0] - 1), 0)

    def wmap(j, i, te, nv):
        return (layer, te[jnp.minimum(i, nv[0] - 1)], 0, j)

    return pl.pallas_call(
        _moe_swiglu_kernel,
        grid_spec=pltpu.PrefetchScalarGridSpec(
            num_scalar_prefetch=2,
            grid=(f // tn, m // tm),
            in_specs=[pl.BlockSpec((tm, k), xmap),
                      pl.BlockSpec((1, 1, k, tn), wmap),
                      pl.BlockSpec((1, 1, k, tn), wmap)],
            out_specs=pl.BlockSpec((tm, tn), lambda j, i, te, nv: (i, j)),
            scratch_shapes=[pltpu.VMEM((k, tn), BF16), pltpu.VMEM((k, tn), BF16)],
        ),
        out_shape=jax.ShapeDtypeStruct((m, f), BF16),
        compiler_params=_cp("arbitrary", "arbitrary"),
        name="moe_swiglu",
    )(tile_expert, n_valid, x, wg, wu)


def _moe_down_kernel(te_ref, nv_ref, x_ref, w_ref, o_ref):
    valid = pl.program_id(1) < nv_ref[0]

    @pl.when(valid)
    def _():
        o_ref[...] = jnp.dot(x_ref[...], w_ref[0], preferred_element_type=F32)

    @pl.when(jnp.logical_not(valid))
    def _():
        o_ref[...] = jnp.zeros_like(o_ref)


def moe_down(x, w, tile_expert, n_valid, tm, tn):
    m, k = x.shape
    n = w.shape[2]

    def xmap(j, i, te, nv):
        return (jnp.minimum(i, nv[0] - 1), 0)

    def wmap(j, i, te, nv):
        return (te[jnp.minimum(i, nv[0] - 1)], 0, j)

    return pl.pallas_call(
        _moe_down_kernel,
        grid_spec=pltpu.PrefetchScalarGridSpec(
            num_scalar_prefetch=2,
            grid=(n // tn, m // tm),
            in_specs=[pl.BlockSpec((tm, k), xmap), pl.BlockSpec((1, k, tn), wmap)],
            out_specs=pl.BlockSpec((tm, tn), lambda j, i, te, nv: (i, j)),
        ),
        out_shape=jax.ShapeDtypeStruct((m, n), F32),
        compiler_params=_cp("arbitrary", "arbitrary"),
        name="moe_down",
    )(tile_expert, n_valid, x, w)


def _mmk_kernel(te_ref, nv_ref, x_ref, w_ref, *rest, nk, has_res):
    if has_res:
        res_ref, g_ref, o_ref = rest
    else:
        (o_ref,) = rest
    i = pl.program_id(0)
    k = pl.program_id(2)

    def finish(acc):
        if has_res:
            return res_ref[...] + g_ref[0] * acc
        return acc

    @pl.when(i < nv_ref[0])
    def _():
        part = jnp.dot(x_ref[...].astype(BF16), w_ref[0], preferred_element_type=F32)
        if nk == 1:
            o_ref[...] = finish(part)
        else:
            @pl.when(k == 0)
            def _():
                o_ref[...] = part

            if nk > 2:
                @pl.when((k > 0) & (k < nk - 1))
                def _():
                    o_ref[...] += part

            @pl.when(k == nk - 1)
            def _():
                o_ref[...] = finish(o_ref[...] + part)

    @pl.when(i >= nv_ref[0])
    def _():
        o_ref[...] = jnp.zeros_like(o_ref)


def mm_k(x, w, tile_expert, n_valid, tm, tn, tk, res=None, modtab=None, gate_row=None,
         n_ctx=0, t_lat=1):
    m, k = x.shape
    n = w.shape[2]
    nk = k // tk

    def xmap(i, j, kk, te, nv):
        return (jnp.minimum(i, nv[0] - 1), kk)

    def wmap(i, j, kk, te, nv):
        return (te[jnp.minimum(i, nv[0] - 1)], kk, j)

    in_specs = [pl.BlockSpec((tm, tk), xmap), pl.BlockSpec((1, tk, tn), wmap)]
    args = [x, w]
    if res is not None:
        in_specs.append(pl.BlockSpec((tm, tn), lambda i, j, kk, te, nv: (i, j)))
        in_specs.append(pl.BlockSpec(
            (1, 1, tn),
            lambda i, j, kk, te, nv: (_group_of_tile(i, tm, n_ctx, t_lat) * 6 + gate_row, 0, j)))
        args += [res, modtab]
    return pl.pallas_call(
        functools.partial(_mmk_kernel, nk=nk, has_res=res is not None),
        grid_spec=pltpu.PrefetchScalarGridSpec(
            num_scalar_prefetch=2,
            grid=(m // tm, n // tn, nk),
            in_specs=in_specs,
            out_specs=pl.BlockSpec((tm, tn), lambda i, j, kk, te, nv: (i, j)),
        ),
        out_shape=jax.ShapeDtypeStruct((m, n), F32),
        compiler_params=_cp("parallel", "parallel", "arbitrary"),
        name="mm_k",
    )(tile_expert, n_valid, *args)


def _one_expert_tables(m, tm):
    return jnp.zeros((m // tm,), jnp.int32), jnp.full((1,), m // tm, jnp.int32)


def rope_tables(rows, head_dim):
    n_freq = head_dim // 4
    inv = ROPE_THETA ** (-jnp.arange(n_freq, dtype=F32) / n_freq)
    r = jnp.repeat(jnp.arange(rows, dtype=F32), GRID_W)
    cl = jnp.tile(jnp.arange(GRID_W, dtype=F32), rows)
    ang = jnp.concatenate([r[:, None] * inv, cl[:, None] * inv], axis=-1)
    return jnp.cos(ang), jnp.sin(ang)


def _attn_prep_kernel(x_ref, cos_ref, sin_ref, qn_ref, kn_ref, q_ref, k_ref, v_ref, kf_ref,
                      *, hq, hkv, dh, n_ctx, tm):
    is_lat = pl.program_id(0) * tm >= n_ctx
    c = jnp.where(is_lat, cos_ref[...], 1.0)
    s = jnp.where(is_lat, sin_ref[...], 0.0)
    half = dh // 2
    for h in range(hq + hkv):
        xh = x_ref[:, h * dh:(h + 1) * dh]
        w = qn_ref[...] if h < hq else kn_ref[...]
        y = xh * lax.rsqrt(jnp.mean(xh * xh, axis=-1, keepdims=True) + EPS) * w
        y1, y2 = y[:, :half], y[:, half:]
        o1 = y1 * c - y2 * s
        o2 = y2 * c + y1 * s
        if h < hq:
            scale = dh ** -0.5
            q_ref[:, h * dh:h * dh + half] = (o1 * scale).astype(BF16)
            q_ref[:, h * dh + half:(h + 1) * dh] = (o2 * scale).astype(BF16)
        else:
            b = (h - hq) * dh
            k_ref[:, b:b + half] = o1.astype(BF16)
            k_ref[:, b + half:b + dh] = o2.astype(BF16)
            kf_ref[:, b:b + half] = o1
            kf_ref[:, b + half:b + dh] = o2
    v_ref[...] = x_ref[:, (hq + hkv) * dh:].astype(BF16)


def attn_prep(qkv, cos, sin, q_norm, k_norm, n_ctx, t_lat, hq, hkv, dh, tm=256):
    m = qkv.shape[0]
    half = dh // 2

    def posmap(i):
        row = i * tm
        return (jnp.where(row >= n_ctx, ((row - n_ctx) % t_lat) // tm, 0), 0)

    return pl.pallas_call(
        functools.partial(_attn_prep_kernel, hq=hq, hkv=hkv, dh=dh, n_ctx=n_ctx, tm=tm),
        grid=(m // tm,),
        in_specs=[pl.BlockSpec((tm, (hq + 2 * hkv) * dh), lambda i: (i, 0)),
                  pl.BlockSpec((tm, half), posmap),
                  pl.BlockSpec((tm, half), posmap),
                  pl.BlockSpec((1, dh), lambda i: (0, 0)),
                  pl.BlockSpec((1, dh), lambda i: (0, 0))],
        out_specs=[pl.BlockSpec((tm, hq * dh), lambda i: (i, 0)),
                   pl.BlockSpec((tm, hkv * dh), lambda i: (i, 0)),
                   pl.BlockSpec((tm, hkv * dh), lambda i: (i, 0)),
                   pl.BlockSpec((tm, hkv * dh), lambda i: (i, 0))],
        out_shape=[jax.ShapeDtypeStruct((m, hq * dh), BF16),
                   jax.ShapeDtypeStruct((m, hkv * dh), BF16),
                   jax.ShapeDtypeStruct((m, hkv * dh), BF16),
                   jax.ShapeDtypeStruct((m, hkv * dh), F32)],
        compiler_params=_cp("parallel"),
        name="attn_prep",
    )(qkv, cos, sin, q_norm.reshape(1, dh), k_norm.reshape(1, dh))


def _attn_kernel(*refs, n_src, n_group, dh):
    q_ref = refs[0]
    kv = refs[1:1 + 2 * n_src]
    o_ref = refs[-1]
    nt = (((1,), (1,)), ((), ()))
    for g in range(n_group):
        qg = q_ref[:, g * dh:(g + 1) * dh]
        ss = [lax.dot_general(qg, kv[2 * i][...], nt, preferred_element_type=F32)
              for i in range(n_src)]
        mx = functools.reduce(jnp.maximum, [jnp.max(s, axis=-1, keepdims=True) for s in ss])
        ps = [jnp.exp(s - mx) for s in ss]
        den = functools.reduce(jnp.add, [jnp.sum(p, axis=-1, keepdims=True) for p in ps])
        o = functools.reduce(jnp.add, [
            jnp.dot(p.astype(BF16), kv[2 * i + 1][...], preferred_element_type=F32)
            for i, p in enumerate(ps)])
        o_ref[:, g * dh:(g + 1) * dh] = (o / den).astype(o_ref.dtype)


def attention(q, k, v, cache, *, batch, seq, row0, hq, hkv, dh, tq):
    n_group = hq // hkv
    nq = seq // tq
    qb0, kb0 = row0 // tq, row0 // seq
    in_specs = [pl.BlockSpec((tq, n_group * dh), lambda b, h, t: (qb0 + b * nq + t, h))]
    args = [q]
    if cache is not None:
        p_len = cache[0].shape[0] // batch
        in_specs += [pl.BlockSpec((p_len, dh), lambda b, h, t: (b, h))] * 2
        args += list(cache)
    in_specs += [pl.BlockSpec((seq, dh), lambda b, h, t: (kb0 + b, h))] * 2
    args += [k, v]
    n_src = (len(args) - 1) // 2
    return pl.pallas_call(
        functools.partial(_attn_kernel, n_src=n_src, n_group=n_group, dh=dh),
        grid=(batch, hkv, nq),
        in_specs=in_specs,
        out_specs=pl.BlockSpec((tq, n_group * dh), lambda b, h, t: (b * nq + t, h)),
        out_shape=jax.ShapeDtypeStruct((batch * seq, hq * dh), BF16),
        compiler_params=_cp("parallel", "parallel", "arbitrary"),
        name="attention",
    )(*args)


def _dn_conv_kernel(x_ref, w_ref, o_ref, pad_ref, *, t, tc, n_ctx_blocks, l_ctx, l_lat,
                    n_q_tiles, n_qk_tiles, dk):
    j = pl.program_id(1)
    zeros8 = jnp.zeros((8, tc), F32)
    pad_ref[0:8, :] = zeros8
    pad_ref[t + 8:t + 16, :] = zeros8
    pad_ref[8:t + 8, :] = x_ref[...]
    w = w_ref[...]
    seq_len = jnp.where(pl.program_id(0) < n_ctx_blocks, l_ctx, l_lat)
    pos = lax.broadcasted_iota(jnp.int32, (t, tc), 0) & (seq_len - 1)
    y = (w[0:1] * jnp.where(pos >= 1, pad_ref[7:t + 7, :], 0.0)
         + w[1:2] * pad_ref[8:t + 8, :]
         + w[2:3] * jnp.where(pos <= seq_len - 2, pad_ref[9:t + 9, :], 0.0)
         + w[3:4] * jnp.where(pos <= seq_len - 3, pad_ref[10:t + 10, :], 0.0))
    y = _silu(y)
    is_qk = j < n_qk_tiles
    q_scale = jnp.where(j < n_q_tiles, dk ** -0.5, 1.0)
    for h in range(tc // LANES):
        yh = y[:, h * LANES:(h + 1) * LANES]
        ss = jnp.sum(yh * yh, axis=-1, keepdims=True)
        inv = jnp.where(is_qk, lax.rsqrt(ss + EPS) * q_scale, 1.0)
        o_ref[:, h * LANES:(h + 1) * LANES] = (yh * inv).astype(o_ref.dtype)


def dn_short_conv(proj, conv_w, *, n_ctx, l_ctx, l_lat, qk_dim, conv_dim, dk, tc=512):
    m = proj.shape[0]
    assert l_lat % l_ctx == 0 and n_ctx % l_lat == 0 and m % l_lat == 0
    assert l_ctx & (l_ctx - 1) == 0 and l_lat & (l_lat - 1) == 0 and l_ctx >= 4
    return pl.pallas_call(
        functools.partial(_dn_conv_kernel, t=l_lat, tc=tc, n_ctx_blocks=n_ctx // l_lat,
                          l_ctx=l_ctx, l_lat=l_lat, n_q_tiles=qk_dim // tc,
                          n_qk_tiles=2 * qk_dim // tc, dk=dk),
        grid=(m // l_lat, conv_dim // tc),
        in_specs=[pl.BlockSpec((l_lat, tc), lambda b, j: (b, j)),
                  pl.BlockSpec((conv_w.shape[0], tc), lambda b, j: (0, j))],
        out_specs=pl.BlockSpec((l_lat, tc), lambda b, j: (b, j)),
        out_shape=jax.ShapeDtypeStruct((m, conv_dim), BF16),
        scratch_shapes=[pltpu.VMEM((l_lat + 16, tc), F32)],
        compiler_params=_cp("parallel", "parallel"),
        name="dn_conv",
    )(proj, conv_w)


def _softplus(x):
    return jnp.maximum(x, 0.0) + jnp.log1p(jnp.exp(-jnp.abs(x)))


def _pair_masks(n):
    r = lax.broadcasted_iota(jnp.int32, (n, n), 0)
    c = lax.broadcasted_iota(jnp.int32, (n, n), 1)
    return [((r >> (k + 1)) == (c >> (k + 1))) & ((r >> k) != (c >> k))
            for k in range(int(math.log2(n)))]


def _unit_tri_inverses(lows, eye, masks):
    ds = [eye - jnp.where(masks[0], low, 0.0) for low in lows]
    lows_b = [low.astype(BF16) for low in lows]
    zero = jnp.zeros((), BF16)
    for mask in masks[1:]:
        dbs = [d.astype(BF16) for d in ds]
        ts = [jnp.dot(db, jnp.where(mask, lb, zero), preferred_element_type=F32)
              for db, lb in zip(dbs, lows_b)]
        ds = [d - jnp.dot(t.astype(BF16), db, preferred_element_type=F32)
              for d, t, db in zip(ds, ts, dbs)]
    return ds


def _dn_scan_kernel(rowf_ref, rowb_ref, flag_ref, seq_ref,
                    qf_ref, kf_ref, vf_ref, qb_ref, kb_ref, vb_ref, baf_ref, bab_ref,
                    nega_ref, dtb_ref, s0_ref, of_ref, ob_ref, sout_ref, s_scr, *, hb, chunk):
    step = pl.program_id(1)
    flag = flag_ref[step]
    first = (flag & 1) == 1
    last = (flag & 2) == 2
    is_lat = (flag & 4) == 4
    dh = DN_HEAD_DIM
    nt = (((1,), (1,)), ((), ()))

    @pl.when(first)
    def _():
        s_scr[...] = jnp.where(is_lat, s0_ref[0], 0.0)

    r = lax.broadcasted_iota(jnp.int32, (chunk, chunk), 0)
    c = lax.broadcasted_iota(jnp.int32, (chunk, chunk), 1)
    eye = (r == c).astype(F32)
    masks = _pair_masks(chunk)

    heads = []
    lows = []
    for d, q_ref, k_ref, v_ref, ba_ref, o_ref in ((0, qf_ref, kf_ref, vf_ref, baf_ref, of_ref),
                                                   (1, qb_ref, kb_ref, vb_ref, bab_ref, ob_ref)):
        incl = (r >= c) if d == 0 else (r <= c)
        strict = (r > c) if d == 0 else (r < c)
        last_row = chunk - 1 if d == 0 else 0
        ba = ba_ref[...]
        sig = jax.nn.sigmoid(ba)
        g = nega_ref[0] * _softplus(ba + dtb_ref[0])
        gc = jnp.dot(incl.astype(F32), g, preferred_element_type=F32,
                     precision=lax.Precision.HIGHEST)
        gct = gc.T
        egc = jnp.exp(gc)
        glast = gc[last_row:last_row + 1, :]
        eglast = jnp.exp(glast)
        ekg = jnp.exp(glast - gc)
        for kh in range(hb // 2):
            qh = q_ref[:, kh * dh:(kh + 1) * dh]
            kk_ = k_ref[:, kh * dh:(kh + 1) * dh]
            kf32 = kk_.astype(F32)
            qf32 = qh.astype(F32)
            kkt = lax.dot_general(kk_, kk_, nt, preferred_element_type=F32)
            qkt = lax.dot_general(qh, kk_, nt, preferred_element_type=F32)
            for rr in range(2):
                hh = kh * 2 + rr
                cb = d * 2 * hb + hh
                ca = d * 2 * hb + hb + hh
                bcol = sig[:, cb:cb + 1]
                ecol = egc[:, ca:ca + 1]
                decay = jnp.exp(jnp.where(incl, gc[:, ca:ca + 1] - gct[ca:ca + 1, :], -jnp.inf))
                lows.append(jnp.where(strict, bcol * kkt * decay, 0.0))
                heads.append(dict(
                    d=d, hh=hh, o_ref=o_ref,
                    vbm=(v_ref[:, hh * dh:(hh + 1) * dh].astype(F32) * bcol).astype(BF16),
                    kbg=(kf32 * (bcol * ecol)).astype(BF16),
                    a=jnp.where(incl, qkt * decay, 0.0).astype(BF16),
                    qg=(qf32 * ecol).astype(BF16),
                    kgt=(kf32 * ekg[:, ca:ca + 1]).T.astype(BF16),
                    egl=eglast[:, ca:ca + 1]))

    tinvs = [t.astype(BF16) for t in _unit_tri_inverses(lows, eye, masks)]
    uws = [jnp.dot(t, jnp.concatenate([h["vbm"], h["kbg"]], axis=1), preferred_element_type=F32)
           for t, h in zip(tinvs, heads)]
    ss = [s_scr[h["d"], h["hh"]] for h in heads]
    wqs = [jnp.dot(jnp.concatenate([uw[:, dh:].astype(BF16), h["qg"]], axis=0), s.astype(BF16),
                   preferred_element_type=F32)
           for uw, h, s in zip(uws, heads, ss)]
    vnbs = [(uw[:, :dh] - wq[:chunk]).astype(BF16) for uw, wq in zip(uws, wqs)]
    avs = [jnp.dot(jnp.concatenate([h["a"], h["kgt"]], axis=0), vnb, preferred_element_type=F32)
           for h, vnb in zip(heads, vnbs)]
    for h, wq, av in zip(heads, wqs, avs):
        hh = h["hh"]
        h["o_ref"][:, hh * dh:(hh + 1) * dh] = wq[chunk:] + av[:chunk]
    for h, s, av in zip(heads, ss, avs):
        s_scr[h["d"], h["hh"]] = s * h["egl"] + av[chunk:]

    @pl.when(last)
    def _():
        sout_ref[0] = s_scr[...]


def dn_scan(qkvc, ba, nega, dtb, s0, seq_lens, *, qk_dim, n_vheads, hb=DN_HEADS_PER_STEP):
    m = qkvc.shape[0]
    chunk = DN_CHUNK
    dh = DN_HEAD_DIM
    n_hg = n_vheads // hb
    rowf, rowb, flags, seqs = [], [], [], []
    row, sid, n_zero_seq = 0, 0, 0
    for n_seq, length, uses_s0 in seq_lens:
        n_chunks = length // chunk
        for _ in range(n_seq):
            for n in range(n_chunks):
                rowf.append(row + n)
                rowb.append(row + n_chunks - 1 - n)
                flags.append((n == 0) * 1 + (n == n_chunks - 1) * 2 + (4 if uses_s0 else 0))
                seqs.append(sid)
            row += n_chunks
            sid += 1
        if not uses_s0:
            n_zero_seq += n_seq
    n_steps = len(rowf)
    n_seq_total = sid
    tabs = [jnp.asarray(np.array(t, np.int32)) for t in (rowf, rowb, flags, seqs)]
    qw, vw = (hb // 2) * dh, hb * dh
    kb0, vb0 = qk_dim // qw, 2 * qk_dim // vw

    def spec(width, col0, rows_idx):
        return pl.BlockSpec((chunk, width),
                            lambda hg, s, rf, rb, fl, sq: ((rf, rb)[rows_idx][s], col0 + hg))

    in_specs = [spec(qw, 0, 0), spec(qw, kb0, 0), spec(vw, vb0, 0),
                spec(qw, 0, 1), spec(qw, kb0, 1), spec(vw, vb0, 1),
                pl.BlockSpec((chunk, LANES), lambda hg, s, rf, rb, fl, sq: (rf[s], hg)),
                pl.BlockSpec((chunk, LANES), lambda hg, s, rf, rb, fl, sq: (rb[s], hg)),
                pl.BlockSpec((1, 1, LANES), lambda hg, s, rf, rb, fl, sq: (hg, 0, 0)),
                pl.BlockSpec((1, 1, LANES), lambda hg, s, rf, rb, fl, sq: (hg, 0, 0)),
                pl.BlockSpec((1, 2, hb, dh, dh),
                             lambda hg, s, rf, rb, fl, sq:
                             (jnp.maximum(sq[s] - n_zero_seq, 0), 0, hg, 0, 0))]
    out_specs = [pl.BlockSpec((chunk, vw), lambda hg, s, rf, rb, fl, sq: (rf[s], hg)),
                 pl.BlockSpec((chunk, vw), lambda hg, s, rf, rb, fl, sq: (rb[s], hg)),
                 pl.BlockSpec((1, 2, hb, dh, dh),
                              lambda hg, s, rf, rb, fl, sq: (sq[s], 0, hg, 0, 0))]
    return pl.pallas_call(
        functools.partial(_dn_scan_kernel, hb=hb, chunk=chunk),
        grid_spec=pltpu.PrefetchScalarGridSpec(
            num_scalar_prefetch=4,
            grid=(n_hg, n_steps),
            in_specs=in_specs,
            out_specs=out_specs,
            scratch_shapes=[pltpu.VMEM((2, hb, dh, dh), F32)],
        ),
        out_shape=[jax.ShapeDtypeStruct((m, n_vheads * dh), F32),
                   jax.ShapeDtypeStruct((m, n_vheads * dh), F32),
                   jax.ShapeDtypeStruct((n_seq_total, 2, n_vheads, dh, dh), F32)],
        compiler_params=_cp("parallel", "arbitrary"),
        name="dn_scan",
    )(*tabs, qkvc, qkvc, qkvc, qkvc, qkvc, qkvc, ba, ba, nega, dtb, s0)


def _dn_gate_kernel(of_ref, ob_ref, z_ref, w_ref, o_ref, *, n_heads):
    w = w_ref[...]
    for h in range(n_heads):
        sl = slice(h * DN_HEAD_DIM, (h + 1) * DN_HEAD_DIM)
        o = of_ref[:, sl] + ob_ref[:, sl]
        y = o * lax.rsqrt(jnp.mean(o * o, axis=-1, keepdims=True) + EPS) * w
        o_ref[:, sl] = (y * _silu(z_ref[:, sl])).astype(o_ref.dtype)


def dn_gate(o_f, o_b, proj, out_norm, z_col0, tm=256, tn=1024):
    m, n = o_f.shape
    zb0 = z_col0 // tn
    return pl.pallas_call(
        functools.partial(_dn_gate_kernel, n_heads=tn // DN_HEAD_DIM),
        grid=(m // tm, n // tn),
        in_specs=[pl.BlockSpec((tm, tn), lambda i, j: (i, j)),
                  pl.BlockSpec((tm, tn), lambda i, j: (i, j)),
                  pl.BlockSpec((tm, tn), lambda i, j: (i, zb0 + j)),
                  pl.BlockSpec((1, DN_HEAD_DIM), lambda i, j: (0, 0))],
        out_specs=pl.BlockSpec((tm, tn), lambda i, j: (i, j)),
        out_shape=jax.ShapeDtypeStruct((m, n), BF16),
        compiler_params=_cp("parallel", "parallel"),
        name="dn_gate",
    )(o_f, o_b, proj, out_norm.reshape(1, DN_HEAD_DIM))


def _router_kernel(x_ref, whi_ref, wlo_ref, info_ref, cnt_ref, *, n_experts, tm):
    @pl.when(pl.program_id(0) == 0)
    def _():
        cnt_ref[...] = jnp.zeros_like(cnt_ref)

    x = x_ref[...]
    xhi = x.astype(BF16)
    xlo = (x - xhi.astype(F32)).astype(BF16)
    lg = (jnp.dot(xhi, whi_ref[...], preferred_element_type=F32)
          + jnp.dot(xlo, whi_ref[...], preferred_element_type=F32)
          + jnp.dot(xhi, wlo_ref[...], preferred_element_type=F32))
    lane = lax.broadcasted_iota(jnp.int32, lg.shape, 1).astype(F32)
    neg = -jnp.inf
    lg = jnp.where(lane < n_experts, lg, neg)
    m1 = jnp.max(lg, axis=-1, keepdims=True)
    i1 = jnp.min(jnp.where(lg == m1, lane, float(LANES)), axis=-1, keepdims=True)
    mk1 = lane == i1
    lg2 = jnp.where(mk1, neg, lg)
    m2 = jnp.max(lg2, axis=-1, keepdims=True)
    i2 = jnp.min(jnp.where(lg2 == m2, lane, float(LANES)), axis=-1, keepdims=True)
    mk2 = lane == i2
    e = jnp.exp(m2 - m1)
    w1 = 1.0 / (1.0 + e)
    sel = jnp.where(mk1 | mk2, 1.0, 0.0)
    r = lax.broadcasted_iota(jnp.int32, (tm, tm), 0)
    c = lax.broadcasted_iota(jnp.int32, (tm, tm), 1)
    before = jnp.dot(jnp.where(r > c, 1.0, 0.0).astype(BF16), sel.astype(BF16),
                     preferred_element_type=F32)
    rank = cnt_ref[...] + before
    r1 = jnp.sum(jnp.where(mk1, rank, 0.0), axis=-1, keepdims=True)
    r2 = jnp.sum(jnp.where(mk2, rank, 0.0), axis=-1, keepdims=True)
    cnt_ref[...] += jnp.sum(sel, axis=0, keepdims=True)
    info = jnp.where(lane == 0, i1, 0.0)
    for k, val in enumerate((i2, r1, r2, w1, e * w1), start=1):
        info = jnp.where(lane == k, val, info)
    info_ref[...] = info


def router(x, w_router, tm=512):
    m, d = x.shape
    e = w_router.shape[1]
    wp = jnp.zeros((d, LANES), F32).at[:, :e].set(w_router)
    whi = wp.astype(BF16)
    wlo = (wp - whi.astype(F32)).astype(BF16)
    return pl.pallas_call(
        functools.partial(_router_kernel, n_experts=e, tm=tm),
        grid=(m // tm,),
        in_specs=[pl.BlockSpec((tm, d), lambda i: (i, 0)),
                  pl.BlockSpec((d, LANES), lambda i: (0, 0)),
                  pl.BlockSpec((d, LANES), lambda i: (0, 0))],
        out_specs=[pl.BlockSpec((tm, LANES), lambda i: (i, 0)),
                   pl.BlockSpec((1, LANES), lambda i: (0, 0))],
        out_shape=[jax.ShapeDtypeStruct((m, LANES), F32),
                   jax.ShapeDtypeStruct((1, LANES), F32)],
        compiler_params=_cp("arbitrary"),
        name="router",
    )(x, whi, wlo)


def _row_copy(src_hbm, row, dst_ref, r, sem):
    return pltpu.make_async_copy(src_hbm.at[pl.ds(row, 1), :], dst_ref.at[pl.ds(r, 1), :], sem)


def _gather_kernel(idx_ref, x_hbm, o_ref, buf, sem, *, tg):
    def start(r, carry):
        _row_copy(x_hbm, idx_ref[0, 0, r], buf, r, sem).start()
        return carry

    def wait(r, carry):
        _row_copy(x_hbm, idx_ref[0, 0, r], buf, r, sem).wait()
        return carry

    lax.fori_loop(0, tg, start, 0, unroll=8)
    lax.fori_loop(0, tg, wait, 0, unroll=8)
    o_ref[...] = buf[...].astype(o_ref.dtype)


def gather_rows(x, idx, out_dtype, tg=256):
    ms = idx.shape[0]
    d = x.shape[1]
    return pl.pallas_call(
        functools.partial(_gather_kernel, tg=tg),
        grid=(ms // tg,),
        in_specs=[pl.BlockSpec((1, 1, tg), lambda i: (i, 0, 0), memory_space=pltpu.SMEM),
                  pl.BlockSpec(memory_space=pl.ANY)],
        out_specs=pl.BlockSpec((tg, d), lambda i: (i, 0)),
        out_shape=jax.ShapeDtypeStruct((ms, d), out_dtype),
        scratch_shapes=[pltpu.VMEM((tg, d), x.dtype), pltpu.SemaphoreType.DMA(())],
        compiler_params=_cp("arbitrary"),
        name="gather_rows",
    )(idx.reshape(ms // tg, 1, tg), x)


def _combine_kernel(pos_ref, ys_hbm, g_ref, res_ref, gm_ref, o_ref, buf, sem, *, tc, n_sel):
    def start(r, carry):
        for j in range(n_sel):
            _row_copy(ys_hbm, pos_ref[0, 0, r * n_sel + j], buf.at[j], r, sem).start()
        return carry

    def wait(r, carry):
        for j in range(n_sel):
            _row_copy(ys_hbm, pos_ref[0, 0, r * n_sel + j], buf.at[j], r, sem).wait()
        return carry

    lax.fori_loop(0, tc, start, 0)
    lax.fori_loop(0, tc, wait, 0)
    acc = g_ref[:, 0:1] * buf[0]
    for j in range(1, n_sel):
        acc = acc + g_ref[:, j:j + 1] * buf[j]
    o_ref[...] = res_ref[...] + gm_ref[0] * acc


def moe_combine(ys, pos, gates, res, modtab, gate_row, n_ctx, t_lat, tc=128):
    m, d = res.shape
    n_sel = pos.shape[1]
    return pl.pallas_call(
        functools.partial(_combine_kernel, tc=tc, n_sel=n_sel),
        grid=(m // tc,),
        in_specs=[pl.BlockSpec((1, 1, tc * n_sel), lambda i: (i, 0, 0), memory_space=pltpu.SMEM),
                  pl.BlockSpec(memory_space=pl.ANY),
                  pl.BlockSpec((tc, n_sel), lambda i: (i, 0)),
                  pl.BlockSpec((tc, d), lambda i: (i, 0)),
                  pl.BlockSpec((1, 1, d),
                               lambda i: (_group_of_tile(i, tc, n_ctx, t_lat) * 6 + gate_row, 0, 0))],
        out_specs=pl.BlockSpec((tc, d), lambda i: (i, 0)),
        out_shape=jax.ShapeDtypeStruct((m, d), F32),
        scratch_shapes=[pltpu.VMEM((n_sel, tc, d), F32), pltpu.SemaphoreType.DMA(())],
        compiler_params=_cp("arbitrary"),
        name="moe_combine",
    )(pos.reshape(m // tc, 1, tc * n_sel), ys, gates, res, modtab)


def _invert_kernel(pos_ref, tok_ref, *, n_pairs, n_sel, n_rows):
    def zero(r, carry):
        tok_ref[r] = 0
        return carry

    def put(p, carry):
        tok_ref[pos_ref[p]] = p // n_sel
        return carry

    lax.fori_loop(0, n_rows, zero, 0, unroll=8)
    lax.fori_loop(0, n_pairs, put, 0, unroll=8)


def invert_positions(pos_flat, n_sel, n_rows):
    n_pairs = pos_flat.shape[0]
    return pl.pallas_call(
        functools.partial(_invert_kernel, n_pairs=n_pairs, n_sel=n_sel, n_rows=n_rows),
        in_specs=[pl.BlockSpec(memory_space=pltpu.SMEM)],
        out_specs=pl.BlockSpec(memory_space=pltpu.SMEM),
        out_shape=jax.ShapeDtypeStruct((n_rows,), jnp.int32),
        name="invert_positions",
    )(pos_flat)


def routing_tables(info, counts, n_experts, tm):
    m = info.shape[0]
    cnt = counts[0, :n_experts].astype(jnp.int32)
    padded = ((cnt + tm - 1) // tm) * tm
    ends = jnp.cumsum(padded)
    starts = ends - padded
    ms = TOP_K * m + n_experts * tm
    ids = info[:, 0:TOP_K].astype(jnp.int32)
    ranks = info[:, TOP_K:2 * TOP_K].astype(jnp.int32)
    g_sel = info[:, 2 * TOP_K:3 * TOP_K]
    onehot = ids[:, :, None] == jnp.arange(n_experts, dtype=jnp.int32)[None, None, :]
    pos_sel = ranks + jnp.sum(jnp.where(onehot, starts[None, None, :], 0), axis=-1)
    tok_sorted = invert_positions(pos_sel.reshape(-1), TOP_K, ms)
    tile_start = jnp.arange(ms // tm, dtype=jnp.int32) * tm
    tile_expert = jnp.minimum(
        jnp.sum(tile_start[:, None] >= ends[None, :], axis=1), n_experts - 1).astype(jnp.int32)
    n_valid = (ends[-1] // tm).astype(jnp.int32).reshape(1)
    return tok_sorted, tile_expert, n_valid, pos_sel.astype(jnp.int32), g_sel


def moe_layer(h, x_res, w_router, wg, wu, layer, wd, modtab, gate_row, n_ctx, t_lat, tm=512):
    n_experts = wg.shape[1]
    info, counts = router(h, w_router)
    tok_sorted, te, nv, pos_sel, g_sel = routing_tables(info, counts, n_experts, tm)
    xs = gather_rows(h, tok_sorted, BF16)
    act = moe_swiglu(xs, wg, wu, layer, te, nv, tm=tm, tn=1024)
    ys = moe_down(act, wd, te, nv, tm=tm, tn=min(1024, wd.shape[2]))
    return moe_combine(ys, pos_sel, g_sel, x_res, modtab, gate_row, n_ctx, t_lat)


def _dn_ba_layout(hb):
    n_hg = DN_V_HEADS // hb
    idx = np.full((n_hg, LANES), -1, np.int64)
    for hg in range(n_hg):
        for d in range(2):
            for ab in range(2):
                for hh in range(hb):
                    idx[hg, d * 2 * hb + ab * hb + hh] = d * 2 * DN_V_HEADS + ab * DN_V_HEADS + hg * hb + hh
    return idx.reshape(-1)


def _permute_cols(a, idx):
    valid = jnp.asarray(idx >= 0)
    return jnp.where(valid, jnp.take(a, jnp.asarray(np.maximum(idx, 0)), axis=-1), 0.0)


def kernel(x_prompt, x_sample, state_dn, cache_k, cache_v, c, c_ctx, w_mod, b_mod, norm_mix, norm_ffn, norm_final, dn_w_in, dn_conv, dn_A_log, dn_dt_bias, dn_out_norm, dn_w_out, att_w_in, att_q_norm, att_k_norm, att_w_out, ffn_w_gate, ffn_w_up, ffn_w_down, moe_router, moe_w_gate, moe_w_up, moe_w_down):
    bc, tc_, d = x_prompt.shape
    bx, tx, _ = x_sample.shape
    depth = w_mod.shape[0]
    n_ctx = bc * tc_
    m = n_ctx + bx * tx
    past = cache_k.shape[2]
    qk_dim = DN_K_HEADS * DN_HEAD_DIM
    v_dim = DN_V_HEADS * DN_HEAD_DIM
    conv_dim = 2 * qk_dim + v_dim
    kv_dim = ATT_KV_HEADS * ATT_HEAD_DIM
    q_dim = ATT_Q_HEADS * ATT_HEAD_DIM

    x = jnp.concatenate([x_prompt.reshape(n_ctx, d), x_sample.reshape(bx * tx, d)], axis=0)
    n_groups = 16
    cvec = jnp.zeros((n_groups, d), F32).at[0].set(c_ctx).at[1:1 + bx].set(c)
    mods = adaln_all(cvec, w_mod, b_mod)
    cos, sin = rope_tables(tx // GRID_W, ATT_HEAD_DIM)
    ba_idx = _dn_ba_layout(DN_HEADS_PER_STEP)
    n_hg = DN_V_HEADS // DN_HEADS_PER_STEP

    new_dn, new_k, new_v = [], [], []
    for i in range(depth):
        j = i // 2
        modtab = mods[i].reshape(n_groups * 6, 1, d)
        h = norm_mod(x, norm_mix[i], modtab, (0, 1), n_ctx, tx, BF16)
        if i % 2 == 0:
            w_in = dn_w_in[j]
            proj = mm(h, w_in[:, :conv_dim + v_dim].astype(BF16), F32)
            ba = mm(h, _permute_cols(w_in[:, conv_dim + v_dim:], ba_idx).astype(BF16), F32)
            old = jnp.zeros((2, 2, DN_V_HEADS), F32)
            nega_old = old.at[:, 1].set(-jnp.exp(dn_A_log[j].astype(F32))).reshape(-1)
            dtb_old = old.at[:, 1].set(dn_dt_bias[j].astype(F32)).reshape(-1)
            nega = _permute_cols(nega_old, ba_idx).reshape(n_hg, 1, LANES)
            dtb = _permute_cols(dtb_old, ba_idx).reshape(n_hg, 1, LANES)
            qkvc = dn_short_conv(proj, dn_conv[j], n_ctx=n_ctx, l_ctx=tc_, l_lat=tx,
                                 qk_dim=qk_dim, conv_dim=conv_dim, dk=DN_HEAD_DIM)
            o_f, o_b, s_out = dn_scan(qkvc, ba, nega, dtb, state_dn[:, j].astype(F32),
                                      [(bc, tc_, False), (bx, tx, True)],
                                      qk_dim=qk_dim, n_vheads=DN_V_HEADS)
            new_dn.append(s_out[:bc])
            og = dn_gate(o_f, o_b, proj, dn_out_norm[j], conv_dim)
            te, nv = _one_expert_tables(m, 512)
            x = mm_k(og, dn_w_out[j].astype(BF16)[None], te, nv, 512, 1024, v_dim,
                     res=x, modtab=modtab, gate_row=2, n_ctx=n_ctx, t_lat=tx)
        else:
            qkv = mm(h, att_w_in[j].astype(BF16), F32)
            q, k, v, kf = attn_prep(qkv, cos, sin, att_q_norm[j], att_k_norm[j], n_ctx, tx,
                                    ATT_Q_HEADS, ATT_KV_HEADS, ATT_HEAD_DIM)
            o_ctx = attention(q, k, v, None, batch=bc, seq=tc_, row0=0,
                              hq=ATT_Q_HEADS, hkv=ATT_KV_HEADS, dh=ATT_HEAD_DIM, tq=tc_)
            ck = cache_k[:, j].reshape(bx * past, kv_dim).astype(BF16)
            cv = cache_v[:, j].reshape(bx * past, kv_dim).astype(BF16)
            o_lat = attention(q, k, v, (ck, cv), batch=bx, seq=tx, row0=n_ctx,
                              hq=ATT_Q_HEADS, hkv=ATT_KV_HEADS, dh=ATT_HEAD_DIM, tq=256)
            o = jnp.concatenate([o_ctx, o_lat], axis=0)
            new_k.append(kf[:n_ctx].reshape(bc, tc_, ATT_KV_HEADS, ATT_HEAD_DIM))
            new_v.append(qkv[:n_ctx, q_dim + kv_dim:].reshape(bc, tc_, ATT_KV_HEADS, ATT_HEAD_DIM))
            te, nv = _one_expert_tables(m, 512)
            x = mm_k(o, att_w_out[j].astype(BF16)[None], te, nv, 512, 1024, q_dim,
                     res=x, modtab=modtab, gate_row=2, n_ctx=n_ctx, t_lat=tx)
        if i % 2 == 0:
            h = norm_mod(x, norm_ffn[i], modtab, (3, 4), n_ctx, tx, BF16)
            f = ffn_w_gate.shape[2]
            te, nv = _one_expert_tables(m, 1024)
            act = mm_swiglu(h, ffn_w_gate[j].astype(BF16)[None], ffn_w_up[j].astype(BF16)[None],
                            te, nv, tm=1024, tn=512)
            te, nv = _one_expert_tables(m, 512)
            x = mm_k(act, ffn_w_down[j].astype(BF16)[None], te, nv, 512, 1024, f // 2,
                     res=x, modtab=modtab, gate_row=5, n_ctx=n_ctx, t_lat=tx)
        else:
            h = norm_mod(x, norm_ffn[i], modtab, (3, 4), n_ctx, tx, F32)
            x = moe_layer(h, x, moe_router[j], moe_w_gate, moe_w_up, j,
                          moe_w_down[j].astype(BF16),
                          modtab, 5, n_ctx, tx)
    y = norm_mod(x, norm_final, None, (), n_ctx, tx, F32)
    y_prompt = y[:n_ctx].reshape(bc, tc_, d)
    y_sample = y[n_ctx:].reshape(bx, tx, d)
    return (y_prompt, y_sample, jnp.stack(new_dn, axis=1),
            jnp.stack(new_k, axis=1), jnp.stack(new_v, axis=1))
```

```python
import functools
import math

import numpy as np
import jax
import jax.numpy as jnp
from jax import lax
from jax.experimental import pallas as pl
from jax.experimental.pallas import tpu as pltpu

F32 = jnp.float32
BF16 = jnp.bfloat16
EPS = 1e-6

GRID_W = 64
ROPE_THETA = 10000.0
DN_K_HEADS = 16
DN_V_HEADS = 32
DN_HEAD_DIM = 128
DN_CHUNK = 64
ATT_Q_HEADS = 16
ATT_KV_HEADS = 2
ATT_HEAD_DIM = 256
N_EXPERTS = 8
TOP_K = 2

LANES = 128
VMEM_LIMIT_BYTES = 56 * 1024 * 1024

DN_HEADS_PER_STEP = 32


def _cp(*sem):
    return pltpu.CompilerParams(dimension_semantics=sem, vmem_limit_bytes=VMEM_LIMIT_BYTES)


def _silu(x):
    return x * jax.nn.sigmoid(x)


def _group_of_tile(i, tm, n_ctx, t_lat):
    row = i * tm
    return jnp.where(row < n_ctx, 0, 1 + (row - n_ctx) // t_lat)


def _adaln_kernel(c_ref, w_ref, b_ref, o_ref):
    s = _silu(c_ref[...]).astype(BF16)
    o_ref[0] = jnp.dot(s, w_ref[0].astype(BF16), preferred_element_type=F32) + b_ref[0]


def adaln_all(cvec, w_mod, b_mod, tn=1024):
    n_layers, d, n = w_mod.shape
    g = cvec.shape[0]
    return pl.pallas_call(
        _adaln_kernel,
        grid=(n_layers, n // tn),
        in_specs=[
            pl.BlockSpec((g, d), lambda l, j: (0, 0)),
            pl.BlockSpec((1, d, tn), lambda l, j: (l, 0, j)),
            pl.BlockSpec((1, 1, tn), lambda l, j: (l, 0, j)),
        ],
        out_specs=pl.BlockSpec((1, g, tn), lambda l, j: (l, 0, j)),
        out_shape=jax.ShapeDtypeStruct((n_layers, g, n), F32),
        compiler_params=_cp("parallel", "parallel"),
        name="adaln",
    )(cvec, w_mod, b_mod.reshape(n_layers, 1, n))


def _norm_mod_kernel(x_ref, w_ref, *rest, modulated):
    x = x_ref[...]
    y = x * lax.rsqrt(jnp.mean(x * x, axis=-1, keepdims=True) + EPS) * w_ref[...]
    if modulated:
        sh_ref, sc_ref, o_ref = rest
        y = y * (1.0 + sc_ref[0]) + sh_ref[0]
    else:
        (o_ref,) = rest
    o_ref[...] = y.astype(o_ref.dtype)


def norm_mod(x, w, modtab, rows, n_ctx, t_lat, out_dtype, tm=512):
    m, d = x.shape
    in_specs = [pl.BlockSpec((tm, d), lambda i: (i, 0)), pl.BlockSpec((1, d), lambda i: (0, 0))]
    args = [x, w.reshape(1, d)]
    if modtab is not None:
        for r in rows:
            in_specs.append(pl.BlockSpec(
                (1, 1, d), lambda i, r=r: (_group_of_tile(i, tm, n_ctx, t_lat) * 6 + r, 0, 0)))
            args.append(modtab)
    return pl.pallas_call(
        functools.partial(_norm_mod_kernel, modulated=modtab is not None),
        grid=(m // tm,),
        in_specs=in_specs,
        out_specs=pl.BlockSpec((tm, d), lambda i: (i, 0)),
        out_shape=jax.ShapeDtypeStruct((m, d), out_dtype),
        compiler_params=_cp("parallel"),
        name="norm_mod",
    )(*args)


def _mm_kernel(x_ref, w_ref, o_ref):
    o_ref[...] = jnp.dot(x_ref[...], w_ref[...], preferred_element_type=F32).astype(o_ref.dtype)


def mm(x, w, out_dtype, tm=1024, tn=1024):
    m, k = x.shape
    n = w.shape[1]
    tm, tn = min(tm, m), min(tn, n)
    return pl.pallas_call(
        _mm_kernel,
        grid=(m // tm, n // tn),
        in_specs=[pl.BlockSpec((tm, k), lambda i, j: (i, 0)),
                  pl.BlockSpec((k, tn), lambda i, j: (0, j))],
        out_specs=pl.BlockSpec((tm, tn), lambda i, j: (i, j)),
        out_shape=jax.ShapeDtypeStruct((m, n), out_dtype),
        compiler_params=_cp("parallel", "parallel"),
        name="mm",
    )(x, w)


def _swiglu_kernel(te_ref, nv_ref, x_ref, wg_ref, wu_ref, o_ref):
    i = pl.program_id(0)

    @pl.when(i < nv_ref[0])
    def _():
        x = x_ref[...].astype(BF16)
        g = jnp.dot(x, wg_ref[0], preferred_element_type=F32)
        u = jnp.dot(x, wu_ref[0], preferred_element_type=F32)
        o_ref[...] = (_silu(g) * u).astype(o_ref.dtype)

    @pl.when(i >= nv_ref[0])
    def _():
        o_ref[...] = jnp.zeros_like(o_ref)


def mm_swiglu(x, wg, wu, tile_expert, n_valid, tm, tn):
    m, k = x.shape
    f = wg.shape[2]

    def xmap(i, j, te, nv):
        return (jnp.minimum(i, nv[0] - 1), 0)

    def wmap(i, j, te, nv):
        return (te[jnp.minimum(i, nv[0] - 1)], 0, j)

    return pl.pallas_call(
        _swiglu_kernel,
        grid_spec=pltpu.PrefetchScalarGridSpec(
            num_scalar_prefetch=2,
            grid=(m // tm, f // tn),
            in_specs=[pl.BlockSpec((tm, k), xmap),
                      pl.BlockSpec((1, k, tn), wmap),
                      pl.BlockSpec((1, k, tn), wmap)],
            out_specs=pl.BlockSpec((tm, tn), lambda i, j, te, nv: (i, j)),
        ),
        out_shape=jax.ShapeDtypeStruct((m, f), BF16),
        compiler_params=_cp("parallel", "arbitrary"),
        name="mm_swiglu",
    )(tile_expert, n_valid, x, wg, wu)


def _moe_swiglu_kernel(te_ref, nv_ref, x_ref, wg_ref, wu_ref, o_ref, wgb_ref, wub_ref):
    i = pl.program_id(1)
    valid = i < nv_ref[0]
    changed = (i == 0) | (te_ref[i] != te_ref[jnp.maximum(i, 1) - 1])

    @pl.when(valid & changed)
    def _():
        wgb_ref[...] = wg_ref[0, 0].astype(BF16)
        wub_ref[...] = wu_ref[0, 0].astype(BF16)

    @pl.when(valid)
    def _():
        x = x_ref[...]
        g = jnp.dot(x, wgb_ref[...], preferred_element_type=F32)
        u = jnp.dot(x, wub_ref[...], preferred_element_type=F32)
        o_ref[...] = (_silu(g) * u).astype(o_ref.dtype)

    @pl.when(jnp.logical_not(valid))
    def _():
        o_ref[...] = jnp.zeros_like(o_ref)


def moe_swiglu(x, wg, wu, layer, tile_expert, n_valid, tm, tn):
    m, k = x.shape
    f = wg.shape[3]

    def xmap(j, i, te, nv):
        return (jnp.minimum(i, nv[0] - 1), 0)

    def wmap(j, i, te, nv):
        return (layer, te[jnp.minimum(i, nv[0] - 1)], 0, j)

    return pl.pallas_call(
        _moe_swiglu_kernel,
        grid_spec=pltpu.PrefetchScalarGridSpec(
            num_scalar_prefetch=2,
            grid=(f // tn, m // tm),
            in_specs=[pl.BlockSpec((tm, k), xmap),
                      pl.BlockSpec((1, 1, k, tn), wmap),
                      pl.BlockSpec((1, 1, k, tn), wmap)],
            out_specs=pl.BlockSpec((tm, tn), lambda j, i, te, nv: (i, j)),
            scratch_shapes=[pltpu.VMEM((k, tn), BF16), pltpu.VMEM((k, tn), BF16)],
        ),
        out_shape=jax.ShapeDtypeStruct((m, f), BF16),
        compiler_params=_cp("arbitrary", "arbitrary"),
        name="moe_swiglu",
    )(tile_expert, n_valid, x, wg, wu)


def _moe_down_kernel(te_ref, nv_ref, x_ref, w_ref, o_ref):
    valid = pl.program_id(1) < nv_ref[0]

    @pl.when(valid)
    def _():
        o_ref[...] = jnp.dot(x_ref[...], w_ref[0], preferred_element_type=F32)

    @pl.when(jnp.logical_not(valid))
    def _():
        o_ref[...] = jnp.zeros_like(o_ref)


def moe_down(x, w, tile_expert, n_valid, tm, tn):
    m, k = x.shape
    n = w.shape[2]

    def xmap(j, i, te, nv):
        return (jnp.minimum(i, nv[0] - 1), 0)

    def wmap(j, i, te, nv):
        return (te[jnp.minimum(i, nv[0] - 1)], 0, j)

    return pl.pallas_call(
        _moe_down_kernel,
        grid_spec=pltpu.PrefetchScalarGridSpec(
            num_scalar_prefetch=2,
            grid=(n // tn, m // tm),
            in_specs=[pl.BlockSpec((tm, k), xmap), pl.BlockSpec((1, k, tn), wmap)],
            out_specs=pl.BlockSpec((tm, tn), lambda j, i, te, nv: (i, j)),
        ),
        out_shape=jax.ShapeDtypeStruct((m, n), F32),
        compiler_params=_cp("arbitrary", "arbitrary"),
        name="moe_down",
    )(tile_expert, n_valid, x, w)


def _mmk_kernel(te_ref, nv_ref, x_ref, w_ref, *rest, nk, has_res):
    if has_res:
        res_ref, g_ref, o_ref = rest
    else:
        (o_ref,) = rest
    i = pl.program_id(0)
    k = pl.program_id(2)

    def finish(acc):
        if has_res:
            return res_ref[...] + g_ref[0] * acc
        return acc

    @pl.when(i < nv_ref[0])
    def _():
        part = jnp.dot(x_ref[...].astype(BF16), w_ref[0], preferred_element_type=F32)
        if nk == 1:
            o_ref[...] = finish(part)
        else:
            @pl.when(k == 0)
            def _():
                o_ref[...] = part

            if nk > 2:
                @pl.when((k > 0) & (k < nk - 1))
                def _():
                    o_ref[...] += part

            @pl.when(k == nk - 1)
            def _():
                o_ref[...] = finish(o_ref[...] + part)

    @pl.when(i >= nv_ref[0])
    def _():
        o_ref[...] = jnp.zeros_like(o_ref)


def mm_k(x, w, tile_expert, n_valid, tm, tn, tk, res=None, modtab=None, gate_row=None,
         n_ctx=0, t_lat=1):
    m, k = x.shape
    n = w.shape[2]
    nk = k // tk

    def xmap(i, j, kk, te, nv):
        return (jnp.minimum(i, nv[0] - 1), kk)

    def wmap(i, j, kk, te, nv):
        return (te[jnp.minimum(i, nv[0] - 1)], kk, j)

    in_specs = [pl.BlockSpec((tm, tk), xmap), pl.BlockSpec((1, tk, tn), wmap)]
    args = [x, w]
    if res is not None:
        in_specs.append(pl.BlockSpec((tm, tn), lambda i, j, kk, te, nv: (i, j)))
        in_specs.append(pl.BlockSpec(
            (1, 1, tn),
            lambda i, j, kk, te, nv: (_group_of_tile(i, tm, n_ctx, t_lat) * 6 + gate_row, 0, j)))
        args += [res, modtab]
    return pl.pallas_call(
        functools.partial(_mmk_kernel, nk=nk, has_res=res is not None),
        grid_spec=pltpu.PrefetchScalarGridSpec(
            num_scalar_prefetch=2,
            grid=(m // tm, n // tn, nk),
            in_specs=in_specs,
            out_specs=pl.BlockSpec((tm, tn), lambda i, j, kk, te, nv: (i, j)),
        ),
        out_shape=jax.ShapeDtypeStruct((m, n), F32),
        compiler_params=_cp("parallel", "parallel", "arbitrary"),
        name="mm_k",
    )(tile_expert, n_valid, *args)


def _one_expert_tables(m, tm):
    return jnp.zeros((m // tm,), jnp.int32), jnp.full((1,), m // tm, jnp.int32)


def rope_tables(rows, head_dim):
    n_freq = head_dim // 4
    inv = ROPE_THETA ** (-jnp.arange(n_freq, dtype=F32) / n_freq)
    r = jnp.repeat(jnp.arange(rows, dtype=F32), GRID_W)
    cl = jnp.tile(jnp.arange(GRID_W, dtype=F32), rows)
    ang = jnp.concatenate([r[:, None] * inv, cl[:, None] * inv], axis=-1)
    return jnp.cos(ang), jnp.sin(ang)


def _attn_prep_kernel(x_ref, cos_ref, sin_ref, qn_ref, kn_ref, q_ref, k_ref, v_ref, kf_ref,
                      *, hq, hkv, dh, n_ctx, tm):
    is_lat = pl.program_id(0) * tm >= n_ctx
    c = jnp.where(is_lat, cos_ref[...], 1.0)
    s = jnp.where(is_lat, sin_ref[...], 0.0)
    half = dh // 2
    for h in range(hq + hkv):
        xh = x_ref[:, h * dh:(h + 1) * dh]
        w = qn_ref[...] if h < hq else kn_ref[...]
        y = xh * lax.rsqrt(jnp.mean(xh * xh, axis=-1, keepdims=True) + EPS) * w
        y1, y2 = y[:, :half], y[:, half:]
        o1 = y1 * c - y2 * s
        o2 = y2 * c + y1 * s
        if h < hq:
            scale = dh ** -0.5
            q_ref[:, h * dh:h * dh + half] = (o1 * scale).astype(BF16)
            q_ref[:, h * dh + half:(h + 1) * dh] = (o2 * scale).astype(BF16)
        else:
            b = (h - hq) * dh
            k_ref[:, b:b + half] = o1.astype(BF16)
            k_ref[:, b + half:b + dh] = o2.astype(BF16)
            kf_ref[:, b:b + half] = o1
            kf_ref[:, b + half:b + dh] = o2
    v_ref[...] = x_ref[:, (hq + hkv) * dh:].astype(BF16)


def attn_prep(qkv, cos, sin, q_norm, k_norm, n_ctx, t_lat, hq, hkv, dh, tm=256):
    m = qkv.shape[0]
    half = dh // 2

    def posmap(i):
        row = i * tm
        return (jnp.where(row >= n_ctx, ((row - n_ctx) % t_lat) // tm, 0), 0)

    return pl.pallas_call(
        functools.partial(_attn_prep_kernel, hq=hq, hkv=hkv, dh=dh, n_ctx=n_ctx, tm=tm),
        grid=(m // tm,),
        in_specs=[pl.BlockSpec((tm, (hq + 2 * hkv) * dh), lambda i: (i, 0)),
                  pl.BlockSpec((tm, half), posmap),
                  pl.BlockSpec((tm, half), posmap),
                  pl.BlockSpec((1, dh), lambda i: (0, 0)),
                  pl.BlockSpec((1, dh), lambda i: (0, 0))],
        out_specs=[pl.BlockSpec((tm, hq * dh), lambda i: (i, 0)),
                   pl.BlockSpec((tm, hkv * dh), lambda i: (i, 0)),
                   pl.BlockSpec((tm, hkv * dh), lambda i: (i, 0)),
                   pl.BlockSpec((tm, hkv * dh), lambda i: (i, 0))],
        out_shape=[jax.ShapeDtypeStruct((m, hq * dh), BF16),
                   jax.ShapeDtypeStruct((m, hkv * dh), BF16),
                   jax.ShapeDtypeStruct((m, hkv * dh), BF16),
                   jax.ShapeDtypeStruct((m, hkv * dh), F32)],
        compiler_params=_cp("parallel"),
        name="attn_prep",
    )(qkv, cos, sin, q_norm.reshape(1, dh), k_norm.reshape(1, dh))


def _attn_kernel(*refs, n_src, n_group, dh):
    q_ref = refs[0]
    kv = refs[1:1 + 2 * n_src]
    o_ref = refs[-1]
    nt = (((1,), (1,)), ((), ()))
    for g in range(n_group):
        qg = q_ref[:, g * dh:(g + 1) * dh]
        ss = [lax.dot_general(qg, kv[2 * i][...], nt, preferred_element_type=F32)
              for i in range(n_src)]
        mx = functools.reduce(jnp.maximum, [jnp.max(s, axis=-1, keepdims=True) for s in ss])
        ps = [jnp.exp(s - mx) for s in ss]
        den = functools.reduce(jnp.add, [jnp.sum(p, axis=-1, keepdims=True) for p in ps])
        o = functools.reduce(jnp.add, [
            jnp.dot(p.astype(BF16), kv[2 * i + 1][...], preferred_element_type=F32)
            for i, p in enumerate(ps)])
        o_ref[:, g * dh:(g + 1) * dh] = (o / den).astype(o_ref.dtype)


def attention(q, k, v, cache, *, batch, seq, row0, hq, hkv, dh, tq):
    n_group = hq // hkv
    nq = seq // tq
    qb0, kb0 = row0 // tq, row0 // seq
    in_specs = [pl.BlockSpec((tq, n_group * dh), lambda b, h, t: (qb0 + b * nq + t, h))]
    args = [q]
    if cache is not None:
        p_len = cache[0].shape[0] // batch
        in_specs += [pl.BlockSpec((p_len, dh), lambda b, h, t: (b, h))] * 2
        args += list(cache)
    in_specs += [pl.BlockSpec((seq, dh), lambda b, h, t: (kb0 + b, h))] * 2
    args += [k, v]
    n_src = (len(args) - 1) // 2
    return pl.pallas_call(
        functools.partial(_attn_kernel, n_src=n_src, n_group=n_group, dh=dh),
        grid=(batch, hkv, nq),
        in_specs=in_specs,
        out_specs=pl.BlockSpec((tq, n_group * dh), lambda b, h, t: (b * nq + t, h)),
        out_shape=jax.ShapeDtypeStruct((batch * seq, hq * dh), BF16),
        compiler_params=_cp("parallel", "parallel", "arbitrary"),
        name="attention",
    )(*args)


def _dn_conv_kernel(x_ref, w_ref, o_ref, pad_ref, *, t, tc, n_ctx_blocks, l_ctx, l_lat,
                    n_q_tiles, n_qk_tiles, dk):
    j = pl.program_id(1)
    zeros8 = jnp.zeros((8, tc), F32)
    pad_ref[0:8, :] = zeros8
    pad_ref[t + 8:t + 16, :] = zeros8
    pad_ref[8:t + 8, :] = x_ref[...]
    w = w_ref[...]
    seq_len = jnp.where(pl.program_id(0) < n_ctx_blocks, l_ctx, l_lat)
    pos = lax.broadcasted_iota(jnp.int32, (t, tc), 0) & (seq_len - 1)
    y = (w[0:1] * jnp.where(pos >= 1, pad_ref[7:t + 7, :], 0.0)
         + w[1:2] * pad_ref[8:t + 8, :]
         + w[2:3] * jnp.where(pos <= seq_len - 2, pad_ref[9:t + 9, :], 0.0)
         + w[3:4] * jnp.where(pos <= seq_len - 3, pad_ref[10:t + 10, :], 0.0))
    y = _silu(y)
    is_qk = j < n_qk_tiles
    q_scale = jnp.where(j < n_q_tiles, dk ** -0.5, 1.0)
    for h in range(tc // LANES):
        yh = y[:, h * LANES:(h + 1) * LANES]
        ss = jnp.sum(yh * yh, axis=-1, keepdims=True)
        inv = jnp.where(is_qk, lax.rsqrt(ss + EPS) * q_scale, 1.0)
        o_ref[:, h * LANES:(h + 1) * LANES] = (yh * inv).astype(o_ref.dtype)


def dn_short_conv(proj, conv_w, *, n_ctx, l_ctx, l_lat, qk_dim, conv_dim, dk, tc=512):
    m = proj.shape[0]
    assert l_lat % l_ctx == 0 and n_ctx % l_lat == 0 and m % l_lat == 0
    assert l_ctx & (l_ctx - 1) == 0 and l_lat & (l_lat - 1) == 0 and l_ctx >= 4
    return pl.pallas_call(
        functools.partial(_dn_conv_kernel, t=l_lat, tc=tc, n_ctx_blocks=n_ctx // l_lat,
                          l_ctx=l_ctx, l_lat=l_lat, n_q_tiles=qk_dim // tc,
                          n_qk_tiles=2 * qk_dim // tc, dk=dk),
        grid=(m // l_lat, conv_dim // tc),
        in_specs=[pl.BlockSpec((l_lat, tc), lambda b, j: (b, j)),
                  pl.BlockSpec((conv_w.shape[0], tc), lambda b, j: (0, j))],
        out_specs=pl.BlockSpec((l_lat, tc), lambda b, j: (b, j)),
        out_shape=jax.ShapeDtypeStruct((m, conv_dim), BF16),
        scratch_shapes=[pltpu.VMEM((l_lat + 16, tc), F32)],
        compiler_params=_cp("parallel", "parallel"),
        name="dn_conv",
    )(proj, conv_w)


def _softplus(x):
    return jnp.maximum(x, 0.0) + jnp.log1p(jnp.exp(-jnp.abs(x)))


def _pair_masks(n):
    r = lax.broadcasted_iota(jnp.int32, (n, n), 0)
    c = lax.broadcasted_iota(jnp.int32, (n, n), 1)
    return [((r >> (k + 1)) == (c >> (k + 1))) & ((r >> k) != (c >> k))
            for k in range(int(math.log2(n)))]


def _unit_tri_inverses(lows, eye, masks):
    ds = [eye - jnp.where(masks[0], low, 0.0) for low in lows]
    lows_b = [low.astype(BF16) for low in lows]
    zero = jnp.zeros((), BF16)
    for mask in masks[1:]:
        dbs = [d.astype(BF16) for d in ds]
        ts = [jnp.dot(db, jnp.where(mask, lb, zero), preferred_element_type=F32)
              for db, lb in zip(dbs, lows_b)]
        ds = [d - jnp.dot(t.astype(BF16), db, preferred_element_type=F32)
              for d, t, db in zip(ds, ts, dbs)]
    return ds


def _dn_scan_kernel(rowf_ref, rowb_ref, flag_ref, seq_ref,
                    qf_ref, kf_ref, vf_ref, qb_ref, kb_ref, vb_ref, baf_ref, bab_ref,
                    nega_ref, dtb_ref, s0_ref, of_ref, ob_ref, sout_ref, s_scr, *, hb, chunk):
    step = pl.program_id(1)
    flag = flag_ref[step]
    first = (flag & 1) == 1
    last = (flag & 2) == 2
    is_lat = (flag & 4) == 4
    dh = DN_HEAD_DIM
    nt = (((1,), (1,)), ((), ()))

    @pl.when(first)
    def _():
        s_scr[...] = jnp.where(is_lat, s0_ref[0], 0.0)

    r = lax.broadcasted_iota(jnp.int32, (chunk, chunk), 0)
    c = lax.broadcasted_iota(jnp.int32, (chunk, chunk), 1)
    eye = (r == c).astype(F32)
    masks = _pair_masks(chunk)

    heads = []
    lows = []
    for d, q_ref, k_ref, v_ref, ba_ref, o_ref in ((0, qf_ref, kf_ref, vf_ref, baf_ref, of_ref),
                                                   (1, qb_ref, kb_ref, vb_ref, bab_ref, ob_ref)):
        incl = (r >= c) if d == 0 else (r <= c)
        strict = (r > c) if d == 0 else (r < c)
        last_row = chunk - 1 if d == 0 else 0
        ba = ba_ref[...]
        sig = jax.nn.sigmoid(ba)
        g = nega_ref[0] * _softplus(ba + dtb_ref[0])
        gc = jnp.dot(incl.astype(F32), g, preferred_element_type=F32,
                     precision=lax.Precision.HIGHEST)
        gct = gc.T
        egc = jnp.exp(gc)
        glast = gc[last_row:last_row + 1, :]
        eglast = jnp.exp(glast)
        ekg = jnp.exp(glast - gc)
        for kh in range(hb // 2):
            qh = q_ref[:, kh * dh:(kh + 1) * dh]
            kk_ = k_ref[:, kh * dh:(kh + 1) * dh]
            kf32 = kk_.astype(F32)
            qf32 = qh.astype(F32)
            kkt = lax.dot_general(kk_, kk_, nt, preferred_element_type=F32)
            qkt = lax.dot_general(qh, kk_, nt, preferred_element_type=F32)
            for rr in range(2):
                hh = kh * 2 + rr
                cb = d * 2 * hb + hh
                ca = d * 2 * hb + hb + hh
                bcol = sig[:, cb:cb + 1]
                ecol = egc[:, ca:ca + 1]
                decay = jnp.exp(jnp.where(incl, gc[:, ca:ca + 1] - gct[ca:ca + 1, :], -jnp.inf))
                lows.append(jnp.where(strict, bcol * kkt * decay, 0.0))
                heads.append(dict(
                    d=d, hh=hh, o_ref=o_ref,
                    vbm=(v_ref[:, hh * dh:(hh + 1) * dh].astype(F32) * bcol).astype(BF16),
                    kbg=(kf32 * (bcol * ecol)).astype(BF16),
                    a=jnp.where(incl, qkt * decay, 0.0).astype(BF16),
                    qg=(qf32 * ecol).astype(BF16),
                    kgt=(kf32 * ekg[:, ca:ca + 1]).T.astype(BF16),
                    egl=eglast[:, ca:ca + 1]))

    tinvs = [t.astype(BF16) for t in _unit_tri_inverses(lows, eye, masks)]
    uws = [jnp.dot(t, jnp.concatenate([h["vbm"], h["kbg"]], axis=1), preferred_element_type=F32)
           for t, h in zip(tinvs, heads)]
    ss = [s_scr[h["d"], h["hh"]] for h in heads]
    wqs = [jnp.dot(jnp.concatenate([uw[:, dh:].astype(BF16), h["qg"]], axis=0), s.astype(BF16),
                   preferred_element_type=F32)
           for uw, h, s in zip(uws, heads, ss)]
    vnbs = [(uw[:, :dh] - wq[:chunk]).astype(BF16) for uw, wq in zip(uws, wqs)]
    avs = [jnp.dot(jnp.concatenate([h["a"], h["kgt"]], axis=0), vnb, preferred_element_type=F32)
           for h, vnb in zip(heads, vnbs)]
    for h, wq, av in zip(heads, wqs, avs):
        hh = h["hh"]
        h["o_ref"][:, hh * dh:(hh + 1) * dh] = (wq[chunk:] + av[:chunk]).astype(BF16)
    for h, s, av in zip(heads, ss, avs):
        s_scr[h["d"], h["hh"]] = s * h["egl"] + av[chunk:]

    @pl.when(last)
    def _():
        sout_ref[0] = s_scr[...]


def dn_scan(qkvc, ba, nega, dtb, s0, seq_lens, *, qk_dim, n_vheads, hb=DN_HEADS_PER_STEP):
    m = qkvc.shape[0]
    chunk = DN_CHUNK
    dh = DN_HEAD_DIM
    n_hg = n_vheads // hb
    rowf, rowb, flags, seqs = [], [], [], []
    row, sid, n_zero_seq = 0, 0, 0
    for n_seq, length, uses_s0 in seq_lens:
        n_chunks = length // chunk
        for _ in range(n_seq):
            for n in range(n_chunks):
                rowf.append(row + n)
                rowb.append(row + n_chunks - 1 - n)
                flags.append((n == 0) * 1 + (n == n_chunks - 1) * 2 + (4 if uses_s0 else 0))
                seqs.append(sid)
            row += n_chunks
            sid += 1
        if not uses_s0:
            n_zero_seq += n_seq
    n_steps = len(rowf)
    n_seq_total = sid
    tabs = [jnp.asarray(np.array(t, np.int32)) for t in (rowf, rowb, flags, seqs)]
    qw, vw = (hb // 2) * dh, hb * dh
    kb0, vb0 = qk_dim // qw, 2 * qk_dim // vw

    def spec(width, col0, rows_idx):
        return pl.BlockSpec((chunk, width),
                            lambda hg, s, rf, rb, fl, sq: ((rf, rb)[rows_idx][s], col0 + hg))

    in_specs = [spec(qw, 0, 0), spec(qw, kb0, 0), spec(vw, vb0, 0),
                spec(qw, 0, 1), spec(qw, kb0, 1), spec(vw, vb0, 1),
                pl.BlockSpec((chunk, LANES), lambda hg, s, rf, rb, fl, sq: (rf[s], hg)),
                pl.BlockSpec((chunk, LANES), lambda hg, s, rf, rb, fl, sq: (rb[s], hg)),
                pl.BlockSpec((1, 1, LANES), lambda hg, s, rf, rb, fl, sq: (hg, 0, 0)),
                pl.BlockSpec((1, 1, LANES), lambda hg, s, rf, rb, fl, sq: (hg, 0, 0)),
                pl.BlockSpec((1, 2, hb, dh, dh),
                             lambda hg, s, rf, rb, fl, sq:
                             (jnp.maximum(sq[s] - n_zero_seq, 0), 0, hg, 0, 0))]
    out_specs = [pl.BlockSpec((chunk, vw), lambda hg, s, rf, rb, fl, sq: (rf[s], hg)),
                 pl.BlockSpec((chunk, vw), lambda hg, s, rf, rb, fl, sq: (rb[s], hg)),
                 pl.BlockSpec((1, 2, hb, dh, dh),
                              lambda hg, s, rf, rb, fl, sq: (sq[s], 0, hg, 0, 0))]
    return pl.pallas_call(
        functools.partial(_dn_scan_kernel, hb=hb, chunk=chunk),
        grid_spec=pltpu.PrefetchScalarGridSpec(
            num_scalar_prefetch=4,
            grid=(n_hg, n_steps),
            in_specs=in_specs,
            out_specs=out_specs,
            scratch_shapes=[pltpu.VMEM((2, hb, dh, dh), F32)],
        ),
        out_shape=[jax.ShapeDtypeStruct((m, n_vheads * dh), BF16),
                   jax.ShapeDtypeStruct((m, n_vheads * dh), BF16),
                   jax.ShapeDtypeStruct((n_seq_total, 2, n_vheads, dh, dh), F32)],
        compiler_params=_cp("parallel", "arbitrary"),
        name="dn_scan",
    )(*tabs, qkvc, qkvc, qkvc, qkvc, qkvc, qkvc, ba, ba, nega, dtb, s0)


def _dn_gate_kernel(of_ref, ob_ref, z_ref, w_ref, o_ref, *, n_heads):
    w = w_ref[...]
    for h in range(n_heads):
        sl = slice(h * DN_HEAD_DIM, (h + 1) * DN_HEAD_DIM)
        o = of_ref[:, sl].astype(F32) + ob_ref[:, sl].astype(F32)
        y = o * lax.rsqrt(jnp.mean(o * o, axis=-1, keepdims=True) + EPS) * w
        o_ref[:, sl] = (y * _silu(z_ref[:, sl])).astype(o_ref.dtype)


def dn_gate(o_f, o_b, proj, out_norm, z_col0, tm=256, tn=1024):
    m, n = o_f.shape
    zb0 = z_col0 // tn
    return pl.pallas_call(
        functools.partial(_dn_gate_kernel, n_heads=tn // DN_HEAD_DIM),
        grid=(m // tm, n // tn),
        in_specs=[pl.BlockSpec((tm, tn), lambda i, j: (i, j)),
                  pl.BlockSpec((tm, tn), lambda i, j: (i, j)),
                  pl.BlockSpec((tm, tn), lambda i, j: (i, zb0 + j)),
                  pl.BlockSpec((1, DN_HEAD_DIM), lambda i, j: (0, 0))],
        out_specs=pl.BlockSpec((tm, tn), lambda i, j: (i, j)),
        out_shape=jax.ShapeDtypeStruct((m, n), BF16),
        compiler_params=_cp("parallel", "parallel"),
        name="dn_gate",
    )(o_f, o_b, proj, out_norm.reshape(1, DN_HEAD_DIM))


def _router_kernel(x_ref, whi_ref, wlo_ref, info_ref, cnt_ref, *, n_experts, tm):
    @pl.when(pl.program_id(0) == 0)
    def _():
        cnt_ref[...] = jnp.zeros_like(cnt_ref)

    x = x_ref[...]
    xhi = x.astype(BF16)
    xlo = (x - xhi.astype(F32)).astype(BF16)
    lg = (jnp.dot(xhi, whi_ref[...], preferred_element_type=F32)
          + jnp.dot(xlo, whi_ref[...], preferred_element_type=F32)
          + jnp.dot(xhi, wlo_ref[...], preferred_element_type=F32))
    lane = lax.broadcasted_iota(jnp.int32, lg.shape, 1).astype(F32)
    neg = -jnp.inf
    lg = jnp.where(lane < n_experts, lg, neg)
    m1 = jnp.max(lg, axis=-1, keepdims=True)
    i1 = jnp.min(jnp.where(lg == m1, lane, float(LANES)), axis=-1, keepdims=True)
    mk1 = lane == i1
    lg2 = jnp.where(mk1, neg, lg)
    m2 = jnp.max(lg2, axis=-1, keepdims=True)
    i2 = jnp.min(jnp.where(lg2 == m2, lane, float(LANES)), axis=-1, keepdims=True)
    mk2 = lane == i2
    e = jnp.exp(m2 - m1)
    w1 = 1.0 / (1.0 + e)
    sel = jnp.where(mk1 | mk2, 1.0, 0.0)
    r = lax.broadcasted_iota(jnp.int32, (tm, tm), 0)
    c = lax.broadcasted_iota(jnp.int32, (tm, tm), 1)
    before = jnp.dot(jnp.where(r > c, 1.0, 0.0).astype(BF16), sel.astype(BF16),
                     preferred_element_type=F32)
    rank = cnt_ref[...] + before
    r1 = jnp.sum(jnp.where(mk1, rank, 0.0), axis=-1, keepdims=True)
    r2 = jnp.sum(jnp.where(mk2, rank, 0.0), axis=-1, keepdims=True)
    cnt_ref[...] += jnp.sum(sel, axis=0, keepdims=True)
    info = jnp.where(lane == 0, i1, 0.0)
    for k, val in enumerate((i2, r1, r2, w1, e * w1), start=1):
        info = jnp.where(lane == k, val, info)
    info_ref[...] = info


def router(x, w_router, tm=512):
    m, d = x.shape
    e = w_router.shape[1]
    wp = jnp.zeros((d, LANES), F32).at[:, :e].set(w_router)
    whi = wp.astype(BF16)
    wlo = (wp - whi.astype(F32)).astype(BF16)
    return pl.pallas_call(
        functools.partial(_router_kernel, n_experts=e, tm=tm),
        grid=(m // tm,),
        in_specs=[pl.BlockSpec((tm, d), lambda i: (i, 0)),
                  pl.BlockSpec((d, LANES), lambda i: (0, 0)),
                  pl.BlockSpec((d, LANES), lambda i: (0, 0))],
        out_specs=[pl.BlockSpec((tm, LANES), lambda i: (i, 0)),
                   pl.BlockSpec((1, LANES), lambda i: (0, 0))],
        out_shape=[jax.ShapeDtypeStruct((m, LANES), F32),
                   jax.ShapeDtypeStruct((1, LANES), F32)],
        compiler_params=_cp("arbitrary"),
        name="router",
    )(x, whi, wlo)


def _row_copy(src_hbm, row, dst_ref, r, sem):
    return pltpu.make_async_copy(src_hbm.at[pl.ds(row, 1), :], dst_ref.at[pl.ds(r, 1), :], sem)


def _gather_kernel(idx_ref, x_hbm, o_ref, buf, sem, *, tg):
    def start(r, carry):
        _row_copy(x_hbm, idx_ref[0, 0, r], buf, r, sem).start()
        return carry

    def wait(r, carry):
        _row_copy(x_hbm, idx_ref[0, 0, r], buf, r, sem).wait()
        return carry

    lax.fori_loop(0, tg, start, 0, unroll=8)
    lax.fori_loop(0, tg, wait, 0, unroll=8)
    o_ref[...] = buf[...].astype(o_ref.dtype)


def gather_rows(x, idx, out_dtype, tg=256):
    ms = idx.shape[0]
    d = x.shape[1]
    return pl.pallas_call(
        functools.partial(_gather_kernel, tg=tg),
        grid=(ms // tg,),
        in_specs=[pl.BlockSpec((1, 1, tg), lambda i: (i, 0, 0), memory_space=pltpu.SMEM),
                  pl.BlockSpec(memory_space=pl.ANY)],
        out_specs=pl.BlockSpec((tg, d), lambda i: (i, 0)),
        out_shape=jax.ShapeDtypeStruct((ms, d), out_dtype),
        scratch_shapes=[pltpu.VMEM((tg, d), x.dtype), pltpu.SemaphoreType.DMA(())],
        compiler_params=_cp("arbitrary"),
        name="gather_rows",
    )(idx.reshape(ms // tg, 1, tg), x)


def _combine_kernel(pos_ref, ys_hbm, g_ref, res_ref, gm_ref, o_ref, buf, sem, *, tc, n_sel):
    def start(r, carry):
        for j in range(n_sel):
            _row_copy(ys_hbm, pos_ref[0, 0, r * n_sel + j], buf.at[j], r, sem).start()
        return carry

    def wait(r, carry):
        for j in range(n_sel):
            _row_copy(ys_hbm, pos_ref[0, 0, r * n_sel + j], buf.at[j], r, sem).wait()
        return carry

    lax.fori_loop(0, tc, start, 0)
    lax.fori_loop(0, tc, wait, 0)
    acc = g_ref[:, 0:1] * buf[0]
    for j in range(1, n_sel):
        acc = acc + g_ref[:, j:j + 1] * buf[j]
    o_ref[...] = res_ref[...] + gm_ref[0] * acc


def moe_combine(ys, pos, gates, res, modtab, gate_row, n_ctx, t_lat, tc=128):
    m, d = res.shape
    n_sel = pos.shape[1]
    return pl.pallas_call(
        functools.partial(_combine_kernel, tc=tc, n_sel=n_sel),
        grid=(m // tc,),
        in_specs=[pl.BlockSpec((1, 1, tc * n_sel), lambda i: (i, 0, 0), memory_space=pltpu.SMEM),
                  pl.BlockSpec(memory_space=pl.ANY),
                  pl.BlockSpec((tc, n_sel), lambda i: (i, 0)),
                  pl.BlockSpec((tc, d), lambda i: (i, 0)),
                  pl.BlockSpec((1, 1, d),
                               lambda i: (_group_of_tile(i, tc, n_ctx, t_lat) * 6 + gate_row, 0, 0))],
        out_specs=pl.BlockSpec((tc, d), lambda i: (i, 0)),
        out_shape=jax.ShapeDtypeStruct((m, d), F32),
        scratch_shapes=[pltpu.VMEM((n_sel, tc, d), F32), pltpu.SemaphoreType.DMA(())],
        compiler_params=_cp("arbitrary"),
        name="moe_combine",
    )(pos.reshape(m // tc, 1, tc * n_sel), ys, gates, res, modtab)


def _invert_kernel(pos_ref, tok_ref, *, n_pairs, n_sel, n_rows):
    def zero(r, carry):
        tok_ref[r] = 0
        return carry

    def put(p, carry):
        tok_ref[pos_ref[p]] = p // n_sel
        return carry

    lax.fori_loop(0, n_rows, zero, 0, unroll=8)
    lax.fori_loop(0, n_pairs, put, 0, unroll=8)


def invert_positions(pos_flat, n_sel, n_rows):
    n_pairs = pos_flat.shape[0]
    return pl.pallas_call(
        functools.partial(_invert_kernel, n_pairs=n_pairs, n_sel=n_sel, n_rows=n_rows),
        in_specs=[pl.BlockSpec(memory_space=pltpu.SMEM)],
        out_specs=pl.BlockSpec(memory_space=pltpu.SMEM),
        out_shape=jax.ShapeDtypeStruct((n_rows,), jnp.int32),
        name="invert_positions",
    )(pos_flat)


def routing_tables(info, counts, n_experts, tm):
    m = info.shape[0]
    cnt = counts[0, :n_experts].astype(jnp.int32)
    padded = ((cnt + tm - 1) // tm) * tm
    ends = jnp.cumsum(padded)
    starts = ends - padded
    ms = TOP_K * m + n_experts * tm
    ids = info[:, 0:TOP_K].astype(jnp.int32)
    ranks = info[:, TOP_K:2 * TOP_K].astype(jnp.int32)
    g_sel = info[:, 2 * TOP_K:3 * TOP_K]
    onehot = ids[:, :, None] == jnp.arange(n_experts, dtype=jnp.int32)[None, None, :]
    pos_sel = ranks + jnp.sum(jnp.where(onehot, starts[None, None, :], 0), axis=-1)
    tok_sorted = invert_positions(pos_sel.reshape(-1), TOP_K, ms)
    tile_start = jnp.arange(ms // tm, dtype=jnp.int32) * tm
    tile_expert = jnp.minimum(
        jnp.sum(tile_start[:, None] >= ends[None, :], axis=1), n_experts - 1).astype(jnp.int32)
    n_valid = (ends[-1] // tm).astype(jnp.int32).reshape(1)
    return tok_sorted, tile_expert, n_valid, pos_sel.astype(jnp.int32), g_sel


def moe_layer(h, x_res, w_router, wg, wu, layer, wd, modtab, gate_row, n_ctx, t_lat, tm=512):
    n_experts = wg.shape[1]
    info, counts = router(h, w_router)
    tok_sorted, te, nv, pos_sel, g_sel = routing_tables(info, counts, n_experts, tm)
    xs = gather_rows(h, tok_sorted, BF16)
    act = moe_swiglu(xs, wg, wu, layer, te, nv, tm=tm, tn=1024)
    ys = moe_down(act, wd, te, nv, tm=tm, tn=min(1024, wd.shape[2]))
    return moe_combine(ys, pos_sel, g_sel, x_res, modtab, gate_row, n_ctx, t_lat)


def _dn_ba_layout(hb):
    n_hg = DN_V_HEADS // hb
    idx = np.full((n_hg, LANES), -1, np.int64)
    for hg in range(n_hg):
        for d in range(2):
            for ab in range(2):
                for hh in range(hb):
                    idx[hg, d * 2 * hb + ab * hb + hh] = d * 2 * DN_V_HEADS + ab * DN_V_HEADS + hg * hb + hh
    return idx.reshape(-1)


def _permute_cols(a, idx):
    valid = jnp.asarray(idx >= 0)
    return jnp.where(valid, jnp.take(a, jnp.asarray(np.maximum(idx, 0)), axis=-1), 0.0)


def kernel(x_prompt, x_sample, state_dn, cache_k, cache_v, c, c_ctx, w_mod, b_mod, norm_mix, norm_ffn, norm_final, dn_w_in, dn_conv, dn_A_log, dn_dt_bias, dn_out_norm, dn_w_out, att_w_in, att_q_norm, att_k_norm, att_w_out, ffn_w_gate, ffn_w_up, ffn_w_down, moe_router, moe_w_gate, moe_w_up, moe_w_down):
    bc, tc_, d = x_prompt.shape
    bx, tx, _ = x_sample.shape
    depth = w_mod.shape[0]
    n_ctx = bc * tc_
    m = n_ctx + bx * tx
    past = cache_k.shape[2]
    qk_dim = DN_K_HEADS * DN_HEAD_DIM
    v_dim = DN_V_HEADS * DN_HEAD_DIM
    conv_dim = 2 * qk_dim + v_dim
    kv_dim = ATT_KV_HEADS * ATT_HEAD_DIM
    q_dim = ATT_Q_HEADS * ATT_HEAD_DIM

    x = jnp.concatenate([x_prompt.reshape(n_ctx, d), x_sample.reshape(bx * tx, d)], axis=0)
    n_groups = 16
    cvec = jnp.zeros((n_groups, d), F32).at[0].set(c_ctx).at[1:1 + bx].set(c)
    mods = adaln_all(cvec, w_mod, b_mod)
    cos, sin = rope_tables(tx // GRID_W, ATT_HEAD_DIM)
    ba_idx = _dn_ba_layout(DN_HEADS_PER_STEP)
    n_hg = DN_V_HEADS // DN_HEADS_PER_STEP

    new_dn, new_k, new_v = [], [], []
    for i in range(depth):
        j = i // 2
        modtab = mods[i].reshape(n_groups * 6, 1, d)
        h = norm_mod(x, norm_mix[i], modtab, (0, 1), n_ctx, tx, BF16)
        if i % 2 == 0:
            w_in = dn_w_in[j]
            proj = mm(h, w_in[:, :conv_dim + v_dim].astype(BF16), F32)
            ba = mm(h, _permute_cols(w_in[:, conv_dim + v_dim:], ba_idx).astype(BF16), F32)
            old = jnp.zeros((2, 2, DN_V_HEADS), F32)
            nega_old = old.at[:, 1].set(-jnp.exp(dn_A_log[j].astype(F32))).reshape(-1)
            dtb_old = old.at[:, 1].set(dn_dt_bias[j].astype(F32)).reshape(-1)
            nega = _permute_cols(nega_old, ba_idx).reshape(n_hg, 1, LANES)
            dtb = _permute_cols(dtb_old, ba_idx).reshape(n_hg, 1, LANES)
            qkvc = dn_short_conv(proj, dn_conv[j], n_ctx=n_ctx, l_ctx=tc_, l_lat=tx,
                                 qk_dim=qk_dim, conv_dim=conv_dim, dk=DN_HEAD_DIM)
            o_f, o_b, s_out = dn_scan(qkvc, ba, nega, dtb, state_dn[:, j].astype(F32),
                                      [(bc, tc_, False), (bx, tx, True)],
                                      qk_dim=qk_dim, n_vheads=DN_V_HEADS)
            new_dn.append(s_out[:bc])
            og = dn_gate(o_f, o_b, proj, dn_out_norm[j], conv_dim)
            te, nv = _one_expert_tables(m, 512)
            x = mm_k(og, dn_w_out[j].astype(BF16)[None], te, nv, 512, 1024, v_dim,
                     res=x, modtab=modtab, gate_row=2, n_ctx=n_ctx, t_lat=tx)
        else:
            qkv = mm(h, att_w_in[j].astype(BF16), F32)
            q, k, v, kf = attn_prep(qkv, cos, sin, att_q_norm[j], att_k_norm[j], n_ctx, tx,
                                    ATT_Q_HEADS, ATT_KV_HEADS, ATT_HEAD_DIM)
            o_ctx = attention(q, k, v, None, batch=bc, seq=tc_, row0=0,
                              hq=ATT_Q_HEADS, hkv=ATT_KV_HEADS, dh=ATT_HEAD_DIM, tq=tc_)
            ck = cache_k[:, j].reshape(bx * past, kv_dim).astype(BF16)
            cv = cache_v[:, j].reshape(bx * past, kv_dim).astype(BF16)
            o_lat = attention(q, k, v, (ck, cv), batch=bx, seq=tx, row0=n_ctx,
                              hq=ATT_Q_HEADS, hkv=ATT_KV_HEADS, dh=ATT_HEAD_DIM, tq=512)
            o = jnp.concatenate([o_ctx, o_lat], axis=0)
            new_k.append(kf[:n_ctx].reshape(bc, tc_, ATT_KV_HEADS, ATT_HEAD_DIM))
            new_v.append(qkv[:n_ctx, q_dim + kv_dim:].reshape(bc, tc_, ATT_KV_HEADS, ATT_HEAD_DIM))
            te, nv = _one_expert_tables(m, 512)
            x = mm_k(o, att_w_out[j].astype(BF16)[None], te, nv, 512, 1024, q_dim,
                     res=x, modtab=modtab, gate_row=2, n_ctx=n_ctx, t_lat=tx)
        if i % 2 == 0:
            h = norm_mod(x, norm_ffn[i], modtab, (3, 4), n_ctx, tx, BF16)
            f = ffn_w_gate.shape[2]
            te, nv = _one_expert_tables(m, 1024)
            act = mm_swiglu(h, ffn_w_gate[j].astype(BF16)[None], ffn_w_up[j].astype(BF16)[None],
                            te, nv, tm=1024, tn=512)
            te, nv = _one_expert_tables(m, 512)
            x = mm_k(act, ffn_w_down[j].astype(BF16)[None], te, nv, 512, 1024, f // 2,
                     res=x, modtab=modtab, gate_row=5, n_ctx=n_ctx, t_lat=tx)
        else:
            h = norm_mod(x, norm_ffn[i], modtab, (3, 4), n_ctx, tx, F32)
            x = moe_layer(h, x, moe_router[j], moe_w_gate, moe_w_up, j,
                          moe_w_down[j].astype(BF16),
                          modtab, 5, n_ctx, tx)
    y = norm_mod(x, norm_final, None, (), n_ctx, tx, F32)
    y_prompt = y[:n_ctx].reshape(bc, tc_, d)
    y_sample = y[n_ctx:].reshape(bx, tx, d)
    return (y_prompt, y_sample, jnp.stack(new_dn, axis=1),
            jnp.stack(new_k, axis=1), jnp.stack(new_v, axis=1))
```
